```python
import math
import jax
import jax.numpy as jnp
from jax import lax
import numpy as np

D_MODEL = 1024
BATCH = 8
SEQ = 2048
DEPTH = 1
DEC_BATCH = 128
DEC_SEQ = 4
PAST_LEN = 16384
PAGE_SIZE = 128

GLA_HEADS = 4
GLA_DK = D_MODEL // 8
GLA_DV = D_MODEL // 4
GLA_LOWRANK = 16
GLA_TAU = 16.0
GLA_CHUNK = 16
RET_HEADS = 4
RET_DK = D_MODEL // 4
RET_DV = D_MODEL // 2
RET_CHUNK = 128
ROPE_BASE = 10000.0

GK = GLA_HEADS * GLA_DK
GV = GLA_HEADS * GLA_DV
RK = RET_HEADS * RET_DK
RV = RET_HEADS * RET_DV
IN_SIZES = (GK, GK, GV, GV, GLA_LOWRANK, RK, RK, RV, RV, D_MODEL, D_MODEL)
D_IN = 2 * GK + 2 * GV + GLA_LOWRANK + 2 * RK + 2 * RV + 2 * D_MODEL

DEEPNORM_ALPHA = (2.0 * DEPTH) ** 0.25
DEEPNORM_BETA = (8.0 * DEPTH) ** -0.25
LN_EPS = 1e-5
HEAD_NORM_EPS = 1e-5

kernel_name = 'gla_retnet_gated_hybrid_step'

F32 = jnp.float32


def _split_in(p):
    cuts = [int(c) for c in np.cumsum(IN_SIZES)[:-1]]
    return jnp.split(p, cuts, axis=-1)


def _layernorm(x, g, b):
    x32 = x.astype(F32)
    mu = jnp.mean(x32, -1, keepdims=True)
    var = jnp.mean(jnp.square(x32 - mu), -1, keepdims=True)
    return ((x32 - mu) * lax.rsqrt(var + LN_EPS)).astype(x.dtype) * g + b


def _head_rmsnorm(o):
    B, T, H, DV = o.shape
    o = o * lax.rsqrt(jnp.mean(jnp.square(o), -1, keepdims=True) + HEAD_NORM_EPS)
    return o.reshape(B, T, H * DV)


def _head_groupnorm(o):
    B, T, H, DV = o.shape
    mu = jnp.mean(o, -1, keepdims=True)
    var = jnp.mean(jnp.square(o - mu), -1, keepdims=True)
    return ((o - mu) * lax.rsqrt(var + HEAD_NORM_EPS)).reshape(B, T, H * DV)


def _rotary(x, pos):
    half = x.shape[-1] // 2
    inv_freq = ROPE_BASE ** (-jnp.arange(half, dtype=F32) / half)
    ang = pos.astype(F32)[:, None] * inv_freq[None, :]
    cos = jnp.cos(ang)[None, :, None, :]
    sin = jnp.sin(ang)[None, :, None, :]
    x32 = x.astype(F32)
    x1, x2 = x32[..., :half], x32[..., half:]
    return jnp.concatenate([x1 * cos - x2 * sin, x1 * sin + x2 * cos], axis=-1)


def _to_chunks(a, L):
    B, T, H, d = a.shape
    return a.astype(F32).reshape(B, T // L, L, H, d).transpose(1, 0, 3, 2, 4)


def _from_chunks(o):
    N, B, H, L, d = o.shape
    return o.transpose(1, 0, 3, 2, 4).reshape(B, N * L, H, d)


def _gla(q, k, v, log_alpha, s0):
    T = q.shape[1]
    L = math.gcd(T, GLA_CHUNK)
    causal = jnp.tril(jnp.ones((L, L), dtype=bool))

    def step(S, inp):
        qc, kc, vc, gc = inp
        b = jnp.cumsum(gc, axis=2)
        q_dec = qc * jnp.exp(b)
        k_dec = kc * jnp.exp(-b)
        A = jnp.where(causal, jnp.einsum('bhtd,bhsd->bhts', q_dec, k_dec), 0.0)
        o = jnp.einsum('bhtd,bhdv->bhtv', q_dec, S) + jnp.einsum('bhts,bhsv->bhtv', A, vc)
        b_last = b[:, :, -1:, :]
        S_new = jnp.exp(b_last[:, :, 0, :])[..., None] * S + jnp.einsum(
            'bhsd,bhsv->bhdv', kc * jnp.exp(b_last - b), vc)
        return S_new, o

    S, o = lax.scan(step, s0.astype(F32),
                    (_to_chunks(q, L), _to_chunks(k, L), _to_chunks(v, L), _to_chunks(log_alpha, L)))
    return _from_chunks(o), S


def _retention(q, k, v, s0):
    T, H = q.shape[1], q.shape[2]
    L = math.gcd(T, RET_CHUNK)
    log_gamma = jnp.log1p(-jnp.exp2(-5.0 - jnp.arange(H, dtype=F32)))
    idx = jnp.arange(L, dtype=F32)
    diff = idx[:, None] - idx[None, :]
    decay_intra = jnp.where(diff[None] >= 0, jnp.exp(jnp.maximum(diff, 0.0)[None] * log_gamma[:, None, None]), 0.0)
    decay_read = jnp.exp((idx + 1.0)[None, :] * log_gamma[:, None])
    decay_write = jnp.exp((L - 1.0 - idx)[None, :] * log_gamma[:, None])
    decay_chunk = jnp.exp(L * log_gamma)

    def step(S, inp):
        qc, kc, vc = inp
        A = jnp.einsum('bhtd,bhsd->bhts', qc, kc) * decay_intra
        o = jnp.einsum('bhts,bhsv->bhtv', A, vc) + jnp.einsum(
            'bhtd,bhdv->bhtv', qc, S) * decay_read[None, :, :, None]
        S_new = decay_chunk[None, :, None, None] * S + jnp.einsum(
            'bhsd,bhsv->bhdv', kc * decay_write[None, :, :, None], vc)
        return S_new, o

    S, o = lax.scan(step, s0.astype(F32), (_to_chunks(q, L), _to_chunks(k, L), _to_chunks(v, L)))
    return _from_chunks(o), S


def _layer(x, c, s_gla, s_ret, pos, w_ada, b_ada, w_in, w_lr2, b_lr2, gla_norm_g, ret_norm_g,
           w_branch_gla, w_branch_ret, w_out, ln_g, ln_b):
    B, T, _ = x.shape
    shift, scale, gate = jnp.split(c @ w_ada + b_ada, 3, axis=-1)
    h = x * (1.0 + scale[:, None, :]) + shift[:, None, :]
    qg, kg, vg, zg, lr, qr, kr, vr, zr, mg, mr = _split_in(h @ w_in)

    log_alpha = jax.nn.log_sigmoid((lr @ w_lr2 + b_lr2).astype(F32)) / GLA_TAU
    o_g, s_gla_new = _gla(qg.reshape(B, T, GLA_HEADS, GLA_DK) * (GLA_DK ** -0.5),
                          kg.reshape(B, T, GLA_HEADS, GLA_DK),
                          vg.reshape(B, T, GLA_HEADS, GLA_DV),
                          log_alpha.reshape(B, T, GLA_HEADS, GLA_DK), s_gla)
    o_g = _head_rmsnorm(o_g).astype(x.dtype) * gla_norm_g * jax.nn.silu(zg)

    o_r, s_ret_new = _retention(_rotary(qr.reshape(B, T, RET_HEADS, RET_DK), pos),
                                _rotary(kr.reshape(B, T, RET_HEADS, RET_DK), pos) * (RET_DK ** -0.5),
                                vr.reshape(B, T, RET_HEADS, RET_DV), s_ret)
    o_r = _head_groupnorm(o_r).astype(x.dtype) * ret_norm_g * jax.nn.silu(zr)

    merged = jax.nn.sigmoid(mg) * (o_g @ w_branch_gla) + jax.nn.sigmoid(mr) * (o_r @ w_branch_ret)
    y = _layernorm(DEEPNORM_ALPHA * x + gate[:, None, :] * (merged @ w_out), ln_g, ln_b)
    return y, s_gla_new.astype(s_gla.dtype), s_ret_new.astype(s_ret.dtype)


def setup_inputs(seed: int = 0) -> dict:
    key = jax.random.key(seed)
    ks = jax.random.split(key, 20)
    nrm = jax.random.normal
    col_scale = jnp.concatenate([
        jnp.full((s,), sc, dtype=F32) for s, sc in zip(
            IN_SIZES, (1.0, 1.0, DEEPNORM_BETA, 1.0, 1.0, 1.0, 1.0, DEEPNORM_BETA, 1.0, 1.0, 1.0))])
    return {
        'x_prompt': nrm(ks[0], (BATCH, SEQ, D_MODEL), F32),
        'x_sample': nrm(ks[1], (DEC_BATCH, DEC_SEQ, D_MODEL), F32),
        'state_gla': 0.1 * nrm(ks[2], (DEPTH, DEC_BATCH, GLA_HEADS, GLA_DK, GLA_DV), F32),
        'state_ret': 0.1 * nrm(ks[3], (DEPTH, DEC_BATCH, RET_HEADS, RET_DK, RET_DV), F32),
        'c_prompt': nrm(ks[4], (BATCH, D_MODEL), F32),
        'c_sample': nrm(ks[5], (DEC_BATCH, D_MODEL), F32),
        'w_ada': 0.5 * D_MODEL ** -0.5 * nrm(ks[6], (DEPTH, D_MODEL, 3 * D_MODEL), F32),
        'b_ada': 0.02 * nrm(ks[7], (DEPTH, 3 * D_MODEL), F32),
        'w_in': D_MODEL ** -0.5 * nrm(ks[8], (DEPTH, D_MODEL, D_IN), F32) * col_scale,
        'w_lr2': GLA_LOWRANK ** -0.5 * nrm(ks[9], (DEPTH, GLA_LOWRANK, GK), F32),
        'b_lr2': 0.1 * nrm(ks[10], (DEPTH, GK), F32),
        'gla_norm_g': 1.0 + 0.02 * nrm(ks[11], (DEPTH, GV), F32),
        'ret_norm_g': 1.0 + 0.02 * nrm(ks[12], (DEPTH, RV), F32),
        'w_branch_gla': DEEPNORM_BETA * GV ** -0.5 * nrm(ks[13], (DEPTH, GV, D_MODEL), F32),
        'w_branch_ret': DEEPNORM_BETA * RV ** -0.5 * nrm(ks[14], (DEPTH, RV, D_MODEL), F32),
        'w_out': DEEPNORM_BETA * D_MODEL ** -0.5 * nrm(ks[15], (DEPTH, D_MODEL, D_MODEL), F32),
        'ln_g': 1.0 + 0.02 * nrm(ks[16], (DEPTH, D_MODEL), F32),
        'ln_b': 0.02 * nrm(ks[17], (DEPTH, D_MODEL), F32),
    }


def reference(x_prompt, x_sample, state_gla, state_ret, c_prompt, c_sample, w_ada, b_ada, w_in,
              w_lr2, b_lr2, gla_norm_g, ret_norm_g, w_branch_gla, w_branch_ret, w_out, ln_g, ln_b):
    n_prompt, t_prompt = x_prompt.shape[0], x_prompt.shape[1]
    pos_prompt = jnp.arange(t_prompt, dtype=jnp.int32)
    pos_sample = PAST_LEN + jnp.arange(x_sample.shape[1], dtype=jnp.int32)
    y_p, y_s = x_prompt, x_sample
    gla_p, ret_p, gla_s, ret_s = [], [], [], []
    for l in range(DEPTH):
        weights = (w_ada[l], b_ada[l], w_in[l], w_lr2[l], b_lr2[l], gla_norm_g[l], ret_norm_g[l],
                   w_branch_gla[l], w_branch_ret[l], w_out[l], ln_g[l], ln_b[l])
        zero_gla = jnp.zeros((n_prompt, GLA_HEADS, GLA_DK, GLA_DV), x_prompt.dtype)
        zero_ret = jnp.zeros((n_prompt, RET_HEADS, RET_DK, RET_DV), x_prompt.dtype)
        y_p, sg_p, sr_p = _layer(y_p, c_prompt, zero_gla, zero_ret, pos_prompt, *weights)
        y_s, sg_s, sr_s = _layer(y_s, c_sample, state_gla[l], state_ret[l], pos_sample, *weights)
        gla_p.append(sg_p)
        ret_p.append(sr_p)
        gla_s.append(sg_s)
        ret_s.append(sr_s)
    return (y_p, y_s, jnp.stack(gla_p), jnp.stack(ret_p), jnp.stack(gla_s), jnp.stack(ret_s))
```

```python
import functools

import numpy as np
import jax
import jax.numpy as jnp
from jax import lax
from jax.experimental import pallas as pl
from jax.experimental.pallas import tpu as pltpu

F32 = jnp.float32
BF16 = jnp.bfloat16

D_MODEL = 1024
PAST_LEN = 16384
GLA_HEADS, GLA_DK, GLA_DV = 4, 128, 256
RET_HEADS, RET_DK, RET_DV = 4, 256, 512
GK, GV = GLA_HEADS * GLA_DK, GLA_HEADS * GLA_DV
RK, RV = RET_HEADS * RET_DK, RET_HEADS * RET_DV
GLA_LOWRANK = 16
GLA_TAU = 16.0
GLA_SUB = 16
ROPE_BASE = 10000.0
DEEPNORM_ALPHA = 2.0 ** 0.25
LN_EPS = 1e-5
HEAD_NORM_EPS = 1e-5

CHUNK = 128
SAMPLE_ROWS = 8
LANES = 128

P_COLS = 11 * 1024
_O_QG, _O_KG, _O_VG, _O_ZG, _O_LR = 0, GK, 2 * GK, 2 * GK + GV, 2 * GK + 2 * GV
_O_QR = _O_LR + GLA_LOWRANK
_O_KR = _O_QR + RK
_O_VR = _O_KR + RK
_O_ZR = _O_VR + RV
_O_MG = _O_ZR + RV
_O_MR = _O_MG + D_MODEL
_O_END = _O_MR + D_MODEL

VMEM_LIMIT = 56 * 1024 * 1024


def _dot(a, b):
    return jnp.dot(a, b, preferred_element_type=F32)


def _dot_nt(a, b):
    return lax.dot_general(a, b, (((1,), (1,)), ((), ())), preferred_element_type=F32)


def _log_sigmoid(x):
    return -(jnp.maximum(-x, 0.0) + jnp.log(1.0 + jnp.exp(-jnp.abs(x))))


def _silu(x):
    return x * jax.nn.sigmoid(x)


def _ada_kernel(c_ref, w_ref, b_ref, o_ref):
    o_ref[...] = _dot(c_ref[...].astype(BF16), w_ref[...].astype(BF16)) + b_ref[...]


def _ada(c, w_ada, b_ada):
    n = c.shape[0]
    return pl.pallas_call(
        _ada_kernel,
        grid=(3,),
        in_specs=[pl.BlockSpec((n, D_MODEL), lambda j: (0, 0)),
                  pl.BlockSpec((D_MODEL, D_MODEL), lambda j: (0, j)),
                  pl.BlockSpec((1, D_MODEL), lambda j: (0, j))],
        out_specs=pl.BlockSpec((n, D_MODEL), lambda j: (0, j)),
        out_shape=jax.ShapeDtypeStruct((n, 3 * D_MODEL), F32),
        compiler_params=pltpu.CompilerParams(dimension_semantics=("arbitrary",)),
        name="ada",
    )(c, w_ada, b_ada)


def _proj_kernel(x_ref, scale_ref, shift_ref, w_ref, wlr_ref, wlr2_ref, blr2_ref, p_ref, la_ref, h_ref):
    @pl.when(pl.program_id(1) == 0)
    def _():
        h = (x_ref[...] * (1.0 + scale_ref[...]) + shift_ref[...]).astype(BF16)
        h_ref[...] = h
        lr = _dot(h, wlr_ref[...])
        pre = _dot(lr.astype(BF16), wlr2_ref[...]) + blr2_ref[...]
        la_ref[...] = _log_sigmoid(pre) * (1.0 / GLA_TAU)

    p_ref[...] = _dot(h_ref[...], w_ref[...]).astype(p_ref.dtype)


def _proj(x2d, scale3, shift3, w_main, w_lr, w_lr2, b_lr2, *, tm, rows_per_mod, p_dtype):
    n = x2d.shape[0]
    per_row = scale3.shape[1] != 1
    if per_row:
        mod_spec = pl.BlockSpec((None, tm, D_MODEL), lambda i, j: (0, i, 0))
    else:
        tiles_per_mod = rows_per_mod // tm
        mod_spec = pl.BlockSpec((None, 1, D_MODEL), lambda i, j: (i // tiles_per_mod, 0, 0))
    tn = 1024
    return pl.pallas_call(
        _proj_kernel,
        grid=(n // tm, P_COLS // tn),
        in_specs=[pl.BlockSpec((tm, D_MODEL), lambda i, j: (i, 0)),
                  mod_spec, mod_spec,
                  pl.BlockSpec((D_MODEL, tn), lambda i, j: (0, j)),
                  pl.BlockSpec((D_MODEL, LANES), lambda i, j: (0, 0)),
                  pl.BlockSpec((LANES, GK), lambda i, j: (0, 0)),
                  pl.BlockSpec((1, GK), lambda i, j: (0, 0))],
        out_specs=[pl.BlockSpec((tm, tn), lambda i, j: (i, j)),
                   pl.BlockSpec((tm, GK), lambda i, j: (i, 0))],
        out_shape=[jax.ShapeDtypeStruct((n, P_COLS), p_dtype),
                   jax.ShapeDtypeStruct((n, GK), F32)],
        scratch_shapes=[pltpu.VMEM((tm, D_MODEL), BF16)],
        compiler_params=pltpu.CompilerParams(dimension_semantics=("parallel", "arbitrary"),
                                             vmem_limit_bytes=VMEM_LIMIT),
        name="proj",
    )(x2d, scale3, shift3, w_main, w_lr, w_lr2, b_lr2)


def _rotary(x, cos, sin):
    half = x.shape[1] // 2
    x1, x2 = x[:, :half], x[:, half:]
    return jnp.concatenate([x1 * cos - x2 * sin, x1 * sin + x2 * cos], axis=1)


def _rms_gate(o, gain, z):
    o = o * lax.rsqrt(jnp.mean(o * o, axis=-1, keepdims=True) + HEAD_NORM_EPS)
    return o * gain * _silu(z)


def _group_gate(o, gain, z):
    mu = jnp.mean(o, axis=-1, keepdims=True)
    c = o - mu
    var = jnp.mean(c * c, axis=-1, keepdims=True)
    return c * lax.rsqrt(var + HEAD_NORM_EPS) * gain * _silu(z)


def _cumsum_rows(x):
    n = x.shape[0]
    row = lax.broadcasted_iota(jnp.int32, x.shape, 0)
    s = 1
    while s < n:
        x = x + jnp.where(row >= s, pltpu.roll(x, s, 0), 0.0)
        s *= 2
    return x


def _block_edge(b, w, prev):
    n, m = b.shape
    pieces = []
    for i in range(n // w):
        r = i * w - 1 if prev else i * w + w - 1
        if r < 0:
            pieces.append(jnp.zeros((w, m), F32))
        else:
            pieces.append(jnp.broadcast_to(b[r:r + 1, :], (w, m)))
    return jnp.concatenate(pieces, axis=0)


def _gla_chunk(q, k, v, la, s0):
    c = q.shape[0]
    b = _cumsum_rows(la)
    btot = b[c - 1:c, :]
    vb = v.astype(BF16)
    row = lax.broadcasted_iota(jnp.int32, (c, c), 0)
    col = lax.broadcasted_iota(jnp.int32, (c, c), 1)

    o = _dot((q * jnp.exp(b)).astype(BF16), s0.astype(BF16))

    sh = GLA_SUB.bit_length() - 1
    rel = b - _block_edge(b, GLA_SUB, True)
    qd = (q * jnp.exp(rel)).astype(BF16)
    kd = (k * jnp.exp(-rel)).astype(BF16)
    diag = ((row >> sh) == (col >> sh)) & (col <= row)
    a = jnp.where(diag, _dot_nt(qd, kd), 0.0)
    w = GLA_SUB
    while w < c:
        sh = w.bit_length() - 1
        qw = qd if w == GLA_SUB else (q * jnp.exp(b - _block_edge(b, w, True))).astype(BF16)
        kw = (k * jnp.exp(_block_edge(b, w, False) - b)).astype(BF16)
        rb, cb = row >> sh, col >> sh
        valid = ((rb & 1) == 1) & (cb == rb - 1)
        a = a + jnp.where(valid, _dot_nt(qw, kw), 0.0)
        w *= 2
    o = o + _dot(a.astype(BF16), vb)

    k_out = (k * jnp.exp(btot - b)).astype(BF16)
    g_col = jnp.exp(jnp.sum(la.T, axis=1, keepdims=True))
    s_new = g_col * s0 + _dot(k_out.T, vb)
    return o, s_new


def _ret_chunk(q, k, v, s0, d_intra, d_read, d_write, d_chunk):
    vb = v.astype(BF16)
    qb = q.astype(BF16)
    a = _dot_nt(qb, k.astype(BF16)) * d_intra
    o = _dot(a.astype(BF16), vb) + _dot(qb, s0.astype(BF16)) * d_read
    s_new = d_chunk * s0 + _dot((k * d_write).astype(BF16).T, vb)
    return o, s_new


def _mixer_prompt_kernel(d_chunk, vr_ref, zr_ref, qr_ref, kr_ref, vg_ref, zg_ref, qkg_ref, la_ref,
                         cos_ref, sin_ref, dintra_ref, dread_ref, dwrite_ref, gg_ref, rg_ref,
                         og_ref, or_ref, sg_ref, sr_ref):
    @pl.when(pl.program_id(1) == 0)
    def _():
        sg_ref[...] = jnp.zeros_like(sg_ref)
        sr_ref[...] = jnp.zeros_like(sr_ref)

    cos = cos_ref[...]
    sin = sin_ref[...]
    for h in range(GLA_HEADS):
        kq = slice(h * GLA_DK, (h + 1) * GLA_DK)
        kk = slice(GK + h * GLA_DK, GK + (h + 1) * GLA_DK)
        vv = slice(h * GLA_DV, (h + 1) * GLA_DV)
        o, s_new = _gla_chunk(qkg_ref[:, kq].astype(F32) * (GLA_DK ** -0.5), qkg_ref[:, kk].astype(F32),
                              vg_ref[:, vv].astype(F32), la_ref[:, kq], sg_ref[h])
        sg_ref[h] = s_new
        og_ref[:, vv] = _rms_gate(o, gg_ref[:, vv], zg_ref[:, vv].astype(F32)).astype(og_ref.dtype)
    for h in range(RET_HEADS):
        kq = slice(h * RET_DK, (h + 1) * RET_DK)
        vv = slice(h * RET_DV, (h + 1) * RET_DV)
        q = _rotary(qr_ref[:, kq].astype(F32), cos, sin)
        k = _rotary(kr_ref[:, kq].astype(F32), cos, sin) * (RET_DK ** -0.5)
        o, s_new = _ret_chunk(q, k, vr_ref[:, vv].astype(F32), sr_ref[h],
                              dintra_ref[h], dread_ref[h], dwrite_ref[h], d_chunk[h])
        sr_ref[h] = s_new
        or_ref[:, vv] = _group_gate(o, rg_ref[:, vv], zr_ref[:, vv].astype(F32)).astype(or_ref.dtype)


def _ret_consts(length):
    h = np.arange(RET_HEADS, dtype=np.float64)
    log_gamma = np.log1p(-np.exp2(-5.0 - h))
    idx = np.arange(length, dtype=np.float64)
    diff = idx[:, None] - idx[None, :]
    d_intra = np.where(diff[None] >= 0, np.exp(np.maximum(diff, 0.0)[None] * log_gamma[:, None, None]), 0.0)
    d_read = np.exp((idx + 1.0)[None, :] * log_gamma[:, None])
    d_write = np.exp((length - 1.0 - idx)[None, :] * log_gamma[:, None])
    d_chunk = np.exp(length * log_gamma)
    return (d_intra.astype(np.float32), d_read.astype(np.float32)[..., None],
            d_write.astype(np.float32)[..., None], tuple(float(np.float32(x)) for x in d_chunk))


def _rope_tables(pos):
    half = RET_DK // 2
    inv_freq = ROPE_BASE ** (-jnp.arange(half, dtype=F32) / half)
    ang = pos.astype(F32)[:, None] * inv_freq[None, :]
    return jnp.cos(ang), jnp.sin(ang)


def _mixer_prompt(p, la, gla_gain, ret_gain, batch, seq):
    nchunk = seq // CHUNK
    d_intra, d_read, d_write, d_chunk = _ret_consts(CHUNK)
    cos, sin = _rope_tables(jnp.arange(seq, dtype=jnp.int32))

    def pcol(width, blk):
        return pl.BlockSpec((CHUNK, width), lambda b, t: (b * nchunk + t, blk))

    const3 = lambda shape: pl.BlockSpec(shape, lambda b, t: (0, 0, 0))
    tok = lambda width: pl.BlockSpec((CHUNK, width), lambda b, t: (b * nchunk + t, 0))
    return pl.pallas_call(
        functools.partial(_mixer_prompt_kernel, d_chunk),
        grid=(batch, nchunk),
        in_specs=[pcol(RV, 0), pcol(RV, 1), pcol(RK, 4), pcol(RK, 5), pcol(GV, 6), pcol(GV, 7),
                  pcol(2 * GK, 10), tok(GK),
                  pl.BlockSpec((CHUNK, RET_DK // 2), lambda b, t: (t, 0)),
                  pl.BlockSpec((CHUNK, RET_DK // 2), lambda b, t: (t, 0)),
                  const3((RET_HEADS, CHUNK, CHUNK)), const3((RET_HEADS, CHUNK, 1)),
                  const3((RET_HEADS, CHUNK, 1)),
                  pl.BlockSpec((1, GV), lambda b, t: (0, 0)),
                  pl.BlockSpec((1, RV), lambda b, t: (0, 0))],
        out_specs=[tok(GV), tok(RV),
                   pl.BlockSpec((None, GLA_HEADS, GLA_DK, GLA_DV), lambda b, t: (b, 0, 0, 0)),
                   pl.BlockSpec((None, RET_HEADS, RET_DK, RET_DV), lambda b, t: (b, 0, 0, 0))],
        out_shape=[jax.ShapeDtypeStruct((batch * seq, GV), BF16),
                   jax.ShapeDtypeStruct((batch * seq, RV), BF16),
                   jax.ShapeDtypeStruct((batch, GLA_HEADS, GLA_DK, GLA_DV), F32),
                   jax.ShapeDtypeStruct((batch, RET_HEADS, RET_DK, RET_DV), F32)],
        compiler_params=pltpu.CompilerParams(dimension_semantics=("parallel", "arbitrary"),
                                             vmem_limit_bytes=VMEM_LIMIT),
        name="mixer_prompt",
    )(p, p, p, p, p, p, p, la, cos, sin, jnp.asarray(d_intra), jnp.asarray(d_read), jnp.asarray(d_write),
      gla_gain, ret_gain)


def _pad_rows(x, rows):
    return jnp.concatenate([x, jnp.zeros((rows - x.shape[0], x.shape[1]), x.dtype)], axis=0)


def _mixer_sample_kernel(seq, gammas, d_chunk, vr_ref, zr_ref, qr_ref, kr_ref, vg_ref, zg_ref, qkg_ref,
                         la_ref, cos_ref, sin_ref, dread_ref, dwrite_ref, gg_ref, rg_ref,
                         sg_in, sr_in, og_ref, or_ref, sg_ref, sr_ref):
    rows = SAMPLE_ROWS
    n_el = rows // seq
    row1 = lax.broadcasted_iota(jnp.int32, (rows, 1), 0)
    pos1 = row1 % seq
    cos = cos_ref[...]
    sin = sin_ref[...]
    tcol = lax.broadcasted_iota(jnp.int32, (LANES, LANES), 1)

    def seg_cumsum(x):
        s = 1
        while s < seq:
            x = x + jnp.where(pos1 >= s, pltpu.roll(x, s, 0), 0.0)
            s *= 2
        return x

    def intra(q, k, v, gamma):
        out = jnp.zeros_like(v)
        for d in range(seq):
            kd = k if d == 0 else pltpu.roll(k, d, 0)
            vd = v if d == 0 else pltpu.roll(v, d, 0)
            a = jnp.sum(q * kd, axis=1, keepdims=True) * (gamma ** d)
            out = out + jnp.where(pos1 >= d, a, 0.0) * vd
        return out

    def inter(q, states):
        qb = _pad_rows(q, 16).astype(BF16)
        out = None
        for e in range(n_el):
            oe = _dot(qb, states[e].astype(BF16))[:rows]
            out = oe if out is None else jnp.where(row1 >= e * seq, oe, out)
        return out

    for h in range(GLA_HEADS):
        kq = slice(h * GLA_DK, (h + 1) * GLA_DK)
        kk = slice(GK + h * GLA_DK, GK + (h + 1) * GLA_DK)
        vv = slice(h * GLA_DV, (h + 1) * GLA_DV)
        q = qkg_ref[:, kq] * (GLA_DK ** -0.5)
        k = qkg_ref[:, kk]
        v = vg_ref[:, vv]
        la = la_ref[:, kq]
        b = seg_cumsum(la)
        q_dec = q * jnp.exp(b)
        k_dec = k * jnp.exp(-b)
        states = [sg_in[e, h] for e in range(n_el)]
        o = inter(q_dec, states) + intra(q_dec, k_dec, v, 1.0)
        og_ref[:, vv] = _rms_gate(o, gg_ref[:, vv], zg_ref[:, vv])
        b_last = jnp.zeros_like(b)
        for e in range(n_el):
            r = e * seq + seq - 1
            b_last = jnp.where((row1 >= e * seq) & (row1 < (e + 1) * seq),
                               jnp.broadcast_to(b[r:r + 1, :], b.shape), b_last)
        k_out = k * jnp.exp(b_last - b)
        k_outT = _pad_rows(k_out, LANES).T
        laT = _pad_rows(la, LANES).T
        vb = _pad_rows(v, LANES).astype(BF16)
        tl = tcol[:GLA_DK]
        for e in range(n_el):
            sel = (tl >= e * seq) & (tl < (e + 1) * seq)
            g_col = jnp.exp(jnp.sum(jnp.where(sel, laT, 0.0), axis=1, keepdims=True))
            sg_ref[e, h] = g_col * states[e] + _dot(jnp.where(sel, k_outT, 0.0).astype(BF16), vb)

    for h in range(RET_HEADS):
        kq = slice(h * RET_DK, (h + 1) * RET_DK)
        vv = slice(h * RET_DV, (h + 1) * RET_DV)
        q = _rotary(qr_ref[:, kq], cos, sin)
        k = _rotary(kr_ref[:, kq], cos, sin) * (RET_DK ** -0.5)
        v = vr_ref[:, vv]
        states = [sr_in[e, h] for e in range(n_el)]
        o = inter(q, states) * dread_ref[h] + intra(q, k, v, gammas[h])
        or_ref[:, vv] = _group_gate(o, rg_ref[:, vv], zr_ref[:, vv])
        kwT = _pad_rows(k * dwrite_ref[h], LANES).T
        vb = _pad_rows(v, LANES).astype(BF16)
        tl = jnp.concatenate([tcol, tcol], axis=0)
        for e in range(n_el):
            sel = (tl >= e * seq) & (tl < (e + 1) * seq)
            sr_ref[e, h] = d_chunk[h] * states[e] + _dot(jnp.where(sel, kwT, 0.0).astype(BF16), vb)


def _mixer_sample(p, la, gla_gain, ret_gain, state_gla, state_ret, batch, seq):
    rows = SAMPLE_ROWS
    n_el = rows // seq
    _, d_read, d_write, d_chunk = _ret_consts(seq)
    gammas = tuple(float(1.0 - 2.0 ** (-5.0 - h)) for h in range(RET_HEADS))
    cos, sin = _rope_tables(PAST_LEN + jnp.arange(seq, dtype=jnp.int32))
    cos, sin = jnp.tile(cos, (n_el, 1)), jnp.tile(sin, (n_el, 1))
    d_read = np.tile(d_read, (1, n_el, 1))
    d_write = np.tile(d_write, (1, n_el, 1))

    pcol = lambda width, blk: pl.BlockSpec((rows, width), lambda i: (i, blk))
    tok = lambda width: pl.BlockSpec((rows, width), lambda i: (i, 0))
    full2 = lambda shape: pl.BlockSpec(shape, lambda i: (0, 0))
    full3 = lambda shape: pl.BlockSpec(shape, lambda i: (0, 0, 0))
    sg_spec = pl.BlockSpec((n_el, GLA_HEADS, GLA_DK, GLA_DV), lambda i: (i, 0, 0, 0))
    sr_spec = pl.BlockSpec((n_el, RET_HEADS, RET_DK, RET_DV), lambda i: (i, 0, 0, 0))
    return pl.pallas_call(
        functools.partial(_mixer_sample_kernel, seq, gammas, d_chunk),
        grid=(batch * seq // rows,),
        in_specs=[pcol(RV, 0), pcol(RV, 1), pcol(RK, 4), pcol(RK, 5), pcol(GV, 6), pcol(GV, 7),
                  pcol(2 * GK, 10), tok(GK),
                  full2((rows, RET_DK // 2)), full2((rows, RET_DK // 2)),
                  full3((RET_HEADS, rows, 1)), full3((RET_HEADS, rows, 1)),
                  full2((1, GV)), full2((1, RV)), sg_spec, sr_spec],
        out_specs=[tok(GV), tok(RV), sg_spec, sr_spec],
        out_shape=[jax.ShapeDtypeStruct((batch * seq, GV), F32),
                   jax.ShapeDtypeStruct((batch * seq, RV), F32),
                   jax.ShapeDtypeStruct(state_gla.shape, F32),
                   jax.ShapeDtypeStruct(state_ret.shape, F32)],
        compiler_params=pltpu.CompilerParams(dimension_semantics=("parallel",),
                                             vmem_limit_bytes=VMEM_LIMIT),
        name="mixer_sample",
    )(p, p, p, p, p, p, p, la, cos, sin, jnp.asarray(d_read), jnp.asarray(d_write),
      gla_gain, ret_gain, state_gla, state_ret)


def _out_kernel(og_ref, or_ref, mg_ref, mr_ref, x_ref, gate_ref, wbg_ref, wbr_ref, wo_ref, lng_ref, lnb_ref,
                y_ref):
    bg = _dot(og_ref[...].astype(BF16), wbg_ref[...])
    br = _dot(or_ref[...].astype(BF16), wbr_ref[...])
    merged = jax.nn.sigmoid(mg_ref[...].astype(F32)) * bg + jax.nn.sigmoid(mr_ref[...].astype(F32)) * br
    out = _dot(merged.astype(BF16), wo_ref[...])
    r = DEEPNORM_ALPHA * x_ref[...] + gate_ref[...] * out
    mu = jnp.mean(r, axis=-1, keepdims=True)
    c = r - mu
    var = jnp.mean(c * c, axis=-1, keepdims=True)
    y_ref[...] = c * lax.rsqrt(var + LN_EPS) * lng_ref[...] + lnb_ref[...]


def _out_stage(og, orr, p, x2d, gate3, wbg, wbr, wo, ln_g, ln_b, *, tm, rows_per_mod):
    n = x2d.shape[0]
    if gate3.shape[1] != 1:
        gate_spec = pl.BlockSpec((None, tm, D_MODEL), lambda i: (0, i, 0))
    else:
        tiles_per_mod = rows_per_mod // tm
        gate_spec = pl.BlockSpec((None, 1, D_MODEL), lambda i: (i // tiles_per_mod, 0, 0))
    full = lambda shape: pl.BlockSpec(shape, lambda i: (0, 0))
    return pl.pallas_call(
        _out_kernel,
        grid=(n // tm,),
        in_specs=[pl.BlockSpec((tm, GV), lambda i: (i, 0)),
                  pl.BlockSpec((tm, RV), lambda i: (i, 0)),
                  pl.BlockSpec((tm, D_MODEL), lambda i: (i, 8)),
                  pl.BlockSpec((tm, D_MODEL), lambda i: (i, 9)),
                  pl.BlockSpec((tm, D_MODEL), lambda i: (i, 0)),
                  gate_spec,
                  full((GV, D_MODEL)), full((RV, D_MODEL)), full((D_MODEL, D_MODEL)),
                  full((1, D_MODEL)), full((1, D_MODEL))],
        out_specs=pl.BlockSpec((tm, D_MODEL), lambda i: (i, 0)),
        out_shape=jax.ShapeDtypeStruct((n, D_MODEL), F32),
        compiler_params=pltpu.CompilerParams(dimension_semantics=("parallel",),
                                             vmem_limit_bytes=VMEM_LIMIT),
        name="out_stage",
    )(og, orr, p, p, x2d, gate3, wbg, wbr, wo, ln_g, ln_b)


def _regroup_w_in(w):
    cols = [w[:, _O_VR:_O_ZR], w[:, _O_ZR:_O_MG], w[:, _O_QR:_O_KR], w[:, _O_KR:_O_VR],
            w[:, _O_VG:_O_ZG], w[:, _O_ZG:_O_LR], w[:, _O_MG:_O_MR], w[:, _O_MR:_O_END],
            w[:, _O_QG:_O_KG], w[:, _O_KG:_O_VG]]
    w_main = jnp.concatenate(cols, axis=1).astype(BF16)
    w_lr = jnp.pad(w[:, _O_LR:_O_QR], ((0, 0), (0, LANES - GLA_LOWRANK))).astype(BF16)
    return w_main, w_lr


def kernel(x_prompt, x_sample, state_gla, state_ret, c_prompt, c_sample, w_ada, b_ada, w_in, w_lr2, b_lr2,
           gla_norm_g, ret_norm_g, w_branch_gla, w_branch_ret, w_out, ln_g, ln_b):
    assert w_ada.shape[0] == 1, "one layer"
    bp, tp, _ = x_prompt.shape
    bs, ts, _ = x_sample.shape

    mod = _ada(jnp.concatenate([c_prompt, c_sample], axis=0), w_ada[0], b_ada)
    shift, scale, gate = mod[:, :D_MODEL], mod[:, D_MODEL:2 * D_MODEL], mod[:, 2 * D_MODEL:]
    per_group = lambda a: a[:bp][:, None, :]
    per_token = lambda a: jnp.repeat(a[bp:], ts, axis=0)[None]

    w_main, w_lr = _regroup_w_in(w_in[0])
    w_lr2p = jnp.pad(w_lr2[0], ((0, LANES - GLA_LOWRANK), (0, 0))).astype(BF16)
    wbg, wbr, wo = w_branch_gla[0].astype(BF16), w_branch_ret[0].astype(BF16), w_out[0].astype(BF16)

    xp = x_prompt.reshape(bp * tp, D_MODEL)
    p_p, la_p = _proj(xp, per_group(scale), per_group(shift), w_main, w_lr, w_lr2p, b_lr2,
                      tm=1024, rows_per_mod=tp, p_dtype=F32)
    og_p, or_p, sg_p, sr_p = _mixer_prompt(p_p, la_p, gla_norm_g, ret_norm_g, bp, tp)
    y_p = _out_stage(og_p, or_p, p_p, xp, per_group(gate), wbg, wbr, wo, ln_g, ln_b, tm=512, rows_per_mod=tp)

    xs = x_sample.reshape(bs * ts, D_MODEL)
    p_s, la_s = _proj(xs, per_token(scale), per_token(shift), w_main, w_lr, w_lr2p, b_lr2,
                      tm=bs * ts, rows_per_mod=ts, p_dtype=F32)
    og_s, or_s, sg_s, sr_s = _mixer_sample(p_s, la_s, gla_norm_g, ret_norm_g, state_gla[0], state_ret[0], bs, ts)
    y_s = _out_stage(og_s, or_s, p_s, xs, per_token(gate), wbg, wbr, wo, ln_g, ln_b,
                     tm=bs * ts, rows_per_mod=ts)

    return (y_p.reshape(bp, tp, D_MODEL), y_s.reshape(bs, ts, D_MODEL),
            sg_p[None], sr_p[None], sg_s[None], sr_s[None])
```

```python
import functools

import numpy as np
import jax
import jax.numpy as jnp
from jax import lax
from jax.experimental import pallas as pl
from jax.experimental.pallas import tpu as pltpu

F32 = jnp.float32
BF16 = jnp.bfloat16

D_MODEL = 1024
PAST_LEN = 16384
GLA_HEADS, GLA_DK, GLA_DV = 4, 128, 256
RET_HEADS, RET_DK, RET_DV = 4, 256, 512
GK, GV = GLA_HEADS * GLA_DK, GLA_HEADS * GLA_DV
RK, RV = RET_HEADS * RET_DK, RET_HEADS * RET_DV
GLA_LOWRANK = 16
GLA_TAU = 16.0
GLA_SUB = 16
ROPE_BASE = 10000.0
DEEPNORM_ALPHA = 2.0 ** 0.25
LN_EPS = 1e-5
HEAD_NORM_EPS = 1e-5

CHUNK = 128
SAMPLE_ROWS = 8
LANES = 128

PROJ_TN = 1024
PROJ_SUB = 512
P_COLS = 11 * PROJ_TN
BLK_VR, BLK_ZR = 0, 3
BLK_QR, BLK_KR, BLK_VG, BLK_QKG, BLK_ZG, BLK_MG, BLK_MR = 2, 3, 4, 5, 8, 9, 10
FIRST_SILU_BLK, FIRST_SIGMOID_BLK = 6, 9
_O_QG, _O_KG, _O_VG, _O_ZG, _O_LR = 0, GK, 2 * GK, 2 * GK + GV, 2 * GK + 2 * GV
_O_QR = _O_LR + GLA_LOWRANK
_O_KR = _O_QR + RK
_O_VR = _O_KR + RK
_O_ZR = _O_VR + RV
_O_MG = _O_ZR + RV
_O_MR = _O_MG + D_MODEL
_O_END = _O_MR + D_MODEL

VMEM_LIMIT = 56 * 1024 * 1024


def _dot(a, b):
    return jnp.dot(a, b, preferred_element_type=F32)


def _dot_nt(a, b):
    return lax.dot_general(a, b, (((1,), (1,)), ((), ())), preferred_element_type=F32)


def _log_sigmoid(x):
    return -(jnp.maximum(-x, 0.0) + jnp.log(1.0 + jnp.exp(-jnp.abs(x))))


def _sigmoid(x):
    return 0.5 * jnp.tanh(0.5 * x) + 0.5


def _silu(x):
    return x * _sigmoid(x)


def _ada_kernel(c_ref, w_ref, b_ref, o_ref):
    o_ref[...] = _dot(c_ref[...].astype(BF16), w_ref[...].astype(BF16)) + b_ref[...]


def _ada(c, w_ada, b_ada):
    n = c.shape[0]
    return pl.pallas_call(
        _ada_kernel,
        grid=(3,),
        in_specs=[pl.BlockSpec((n, D_MODEL), lambda j: (0, 0)),
                  pl.BlockSpec((D_MODEL, D_MODEL), lambda j: (0, j)),
                  pl.BlockSpec((1, D_MODEL), lambda j: (0, j))],
        out_specs=pl.BlockSpec((n, D_MODEL), lambda j: (0, j)),
        out_shape=jax.ShapeDtypeStruct((n, 3 * D_MODEL), F32),
        compiler_params=pltpu.CompilerParams(dimension_semantics=("arbitrary",)),
        name="ada",
    )(c, w_ada, b_ada)


def _proj_kernel(x_ref, scale_ref, shift_ref, w_ref, wlr_ref, wlr2_ref, blr2_ref, p_ref, la_ref, h_ref):
    j = pl.program_id(1)

    @pl.when(j == 0)
    def _():
        h = (x_ref[...] * (1.0 + scale_ref[...]) + shift_ref[...]).astype(BF16)
        h_ref[...] = h
        lr = _dot(h, wlr_ref[...])
        pre = _dot(lr.astype(BF16), wlr2_ref[...]) + blr2_ref[...]
        la_ref[...] = _log_sigmoid(pre) * (1.0 / GLA_TAU)

    def emit(epilogue):
        sub = min(PROJ_SUB, p_ref.shape[0])
        for r in range(0, p_ref.shape[0], sub):
            rs = slice(r, r + sub)
            p_ref[rs, :] = epilogue(_dot(h_ref[rs, :], w_ref[...])).astype(p_ref.dtype)

    pl.when(j < FIRST_SILU_BLK)(lambda: emit(lambda a: a))
    pl.when((j >= FIRST_SILU_BLK) & (j < FIRST_SIGMOID_BLK))(lambda: emit(_silu))
    pl.when(j >= FIRST_SIGMOID_BLK)(lambda: emit(_sigmoid))


def _proj(x2d, scale3, shift3, w_main, w_lr, w_lr2, b_lr2, *, tm, rows_per_mod, p_dtype):
    n = x2d.shape[0]
    per_row = scale3.shape[1] != 1
    if per_row:
        mod_spec = pl.BlockSpec((None, tm, D_MODEL), lambda i, j: (0, i, 0))
    else:
        tiles_per_mod = rows_per_mod // tm
        mod_spec = pl.BlockSpec((None, 1, D_MODEL), lambda i, j: (i // tiles_per_mod, 0, 0))
    tn = PROJ_TN
    return pl.pallas_call(
        _proj_kernel,
        grid=(n // tm, P_COLS // tn),
        in_specs=[pl.BlockSpec((tm, D_MODEL), lambda i, j: (i, 0)),
                  mod_spec, mod_spec,
                  pl.BlockSpec((D_MODEL, tn), lambda i, j: (0, j)),
                  pl.BlockSpec((D_MODEL, LANES), lambda i, j: (0, 0)),
                  pl.BlockSpec((LANES, GK), lambda i, j: (0, 0)),
                  pl.BlockSpec((1, GK), lambda i, j: (0, 0))],
        out_specs=[pl.BlockSpec((tm, tn), lambda i, j: (i, j)),
                   pl.BlockSpec((tm, GK), lambda i, j: (i, 0))],
        out_shape=[jax.ShapeDtypeStruct((n, P_COLS), p_dtype),
                   jax.ShapeDtypeStruct((n, GK), F32)],
        scratch_shapes=[pltpu.VMEM((tm, D_MODEL), BF16)],
        compiler_params=pltpu.CompilerParams(dimension_semantics=("parallel", "arbitrary"),
                                             vmem_limit_bytes=VMEM_LIMIT),
        name="proj",
    )(x2d, scale3, shift3, w_main, w_lr, w_lr2, b_lr2)


def _rotary(x, cos, sin):
    half = x.shape[1] // 2
    x1, x2 = x[:, :half], x[:, half:]
    return jnp.concatenate([x1 * cos - x2 * sin, x1 * sin + x2 * cos], axis=1)


def _rms_gate(o, gain, sz):
    o = o * lax.rsqrt(jnp.mean(o * o, axis=-1, keepdims=True) + HEAD_NORM_EPS)
    return o * gain * sz


def _group_gate(o, gain, sz):
    mu = jnp.mean(o, axis=-1, keepdims=True)
    c = o - mu
    var = jnp.mean(c * c, axis=-1, keepdims=True)
    return c * lax.rsqrt(var + HEAD_NORM_EPS) * gain * sz


def _cumsum_rows(x):
    n = x.shape[0]
    row = lax.broadcasted_iota(jnp.int32, x.shape, 0)
    s = 1
    while s < n:
        x = x + jnp.where(row >= s, pltpu.roll(x, s, 0), 0.0)
        s *= 2
    return x


def _block_edge(b, w, prev):
    n, m = b.shape
    pieces = []
    for i in range(n // w):
        r = i * w - 1 if prev else i * w + w - 1
        if r < 0:
            pieces.append(jnp.zeros((w, m), F32))
        else:
            pieces.append(jnp.broadcast_to(b[r:r + 1, :], (w, m)))
    return jnp.concatenate(pieces, axis=0)


def _gla_chunk(q, k, vb, la, s0):
    c = q.shape[0]
    b = _cumsum_rows(la)
    btot = b[c - 1:c, :]
    row = lax.broadcasted_iota(jnp.int32, (c, c), 0)
    col = lax.broadcasted_iota(jnp.int32, (c, c), 1)

    o = _dot((q * jnp.exp(b)).astype(BF16), s0.astype(BF16))

    sh = GLA_SUB.bit_length() - 1
    rel = b - _block_edge(b, GLA_SUB, True)
    qd = (q * jnp.exp(rel)).astype(BF16)
    kd = (k * jnp.exp(-rel)).astype(BF16)
    diag = ((row >> sh) == (col >> sh)) & (col <= row)
    a = jnp.where(diag, _dot_nt(qd, kd), 0.0)
    w = GLA_SUB
    while w < c:
        sh = w.bit_length() - 1
        qw = qd if w == GLA_SUB else (q * jnp.exp(b - _block_edge(b, w, True))).astype(BF16)
        kw = (k * jnp.exp(_block_edge(b, w, False) - b)).astype(BF16)
        rb, cb = row >> sh, col >> sh
        valid = ((rb & 1) == 1) & (cb == rb - 1)
        a = a + jnp.where(valid, _dot_nt(qw, kw), 0.0)
        w *= 2
    o = o + _dot(a.astype(BF16), vb)

    k_out = (k * jnp.exp(btot - b)).astype(BF16)
    g_col = jnp.exp(jnp.sum(la.T, axis=1, keepdims=True))
    s_new = g_col * s0 + _dot(k_out.T, vb)
    return o, s_new


def _ret_chunk(q, k, vb, s0, d_intra, d_read, d_write, d_chunk):
    qb = q.astype(BF16)
    a = _dot_nt(qb, k.astype(BF16)) * d_intra
    o = _dot(a.astype(BF16), vb) + _dot(qb, s0.astype(BF16)) * d_read
    s_new = d_chunk * s0 + _dot((k * d_write).astype(BF16).T, vb)
    return o, s_new


def _mixer_prompt_kernel(d_chunk, vr_ref, zr_ref, qr_ref, kr_ref, vg_ref, zg_ref, qkg_ref, la_ref,
                         cos_ref, sin_ref, dintra_ref, dread_ref, dwrite_ref, gg_ref, rg_ref,
                         og_ref, or_ref, sg_ref, sr_ref):
    @pl.when(pl.program_id(1) == 0)
    def _():
        sg_ref[...] = jnp.zeros_like(sg_ref)
        sr_ref[...] = jnp.zeros_like(sr_ref)

    cos = cos_ref[...]
    sin = sin_ref[...]
    for h in range(GLA_HEADS):
        kq = slice(h * GLA_DK, (h + 1) * GLA_DK)
        kk = slice(GK + h * GLA_DK, GK + (h + 1) * GLA_DK)
        vv = slice(h * GLA_DV, (h + 1) * GLA_DV)
        o, s_new = _gla_chunk(qkg_ref[:, kq].astype(F32) * (GLA_DK ** -0.5), qkg_ref[:, kk].astype(F32),
                              vg_ref[:, vv].astype(BF16), la_ref[:, kq], sg_ref[h])
        sg_ref[h] = s_new
        og_ref[:, vv] = _rms_gate(o, gg_ref[:, vv], zg_ref[:, vv].astype(F32)).astype(og_ref.dtype)
    for h in range(RET_HEADS):
        kq = slice(h * RET_DK, (h + 1) * RET_DK)
        vv = slice(h * RET_DV, (h + 1) * RET_DV)
        q = _rotary(qr_ref[:, kq].astype(F32), cos, sin)
        k = _rotary(kr_ref[:, kq].astype(F32), cos, sin) * (RET_DK ** -0.5)
        o, s_new = _ret_chunk(q, k, vr_ref[:, vv].astype(BF16), sr_ref[h],
                              dintra_ref[h], dread_ref[h], dwrite_ref[h], d_chunk[h])
        sr_ref[h] = s_new
        or_ref[:, vv] = _group_gate(o, rg_ref[:, vv], zr_ref[:, vv].astype(F32)).astype(or_ref.dtype)


def _ret_consts(length):
    h = np.arange(RET_HEADS, dtype=np.float64)
    log_gamma = np.log1p(-np.exp2(-5.0 - h))
    idx = np.arange(length, dtype=np.float64)
    diff = idx[:, None] - idx[None, :]
    d_intra = np.where(diff[None] >= 0, np.exp(np.maximum(diff, 0.0)[None] * log_gamma[:, None, None]), 0.0)
    d_read = np.exp((idx + 1.0)[None, :] * log_gamma[:, None])
    d_write = np.exp((length - 1.0 - idx)[None, :] * log_gamma[:, None])
    d_chunk = np.exp(length * log_gamma)
    return (d_intra.astype(np.float32), d_read.astype(np.float32)[..., None],
            d_write.astype(np.float32)[..., None], tuple(float(np.float32(x)) for x in d_chunk))


def _rope_tables(pos):
    half = RET_DK // 2
    inv_freq = ROPE_BASE ** (-jnp.arange(half, dtype=F32) / half)
    ang = pos.astype(F32)[:, None] * inv_freq[None, :]
    return jnp.cos(ang), jnp.sin(ang)


def _mixer_prompt(p, la, gla_gain, ret_gain, batch, seq):
    nchunk = seq // CHUNK
    d_intra, d_read, d_write, d_chunk = _ret_consts(CHUNK)
    cos, sin = _rope_tables(jnp.arange(seq, dtype=jnp.int32))

    def pcol(width, blk):
        return pl.BlockSpec((CHUNK, width), lambda b, t: (b * nchunk + t, blk))

    const3 = lambda shape: pl.BlockSpec(shape, lambda b, t: (0, 0, 0))
    tok = lambda width: pl.BlockSpec((CHUNK, width), lambda b, t: (b * nchunk + t, 0))
    return pl.pallas_call(
        functools.partial(_mixer_prompt_kernel, d_chunk),
        grid=(batch, nchunk),
        in_specs=[pcol(RV, BLK_VR), pcol(RV, BLK_ZR), pcol(RK, BLK_QR), pcol(RK, BLK_KR),
                  pcol(GV, BLK_VG), pcol(GV, BLK_ZG), pcol(2 * GK, BLK_QKG), tok(GK),
                  pl.BlockSpec((CHUNK, RET_DK // 2), lambda b, t: (t, 0)),
                  pl.BlockSpec((CHUNK, RET_DK // 2), lambda b, t: (t, 0)),
                  const3((RET_HEADS, CHUNK, CHUNK)), const3((RET_HEADS, CHUNK, 1)),
                  const3((RET_HEADS, CHUNK, 1)),
                  pl.BlockSpec((1, GV), lambda b, t: (0, 0)),
                  pl.BlockSpec((1, RV), lambda b, t: (0, 0))],
        out_specs=[tok(GV), tok(RV),
                   pl.BlockSpec((None, GLA_HEADS, GLA_DK, GLA_DV), lambda b, t: (b, 0, 0, 0)),
                   pl.BlockSpec((None, RET_HEADS, RET_DK, RET_DV), lambda b, t: (b, 0, 0, 0))],
        out_shape=[jax.ShapeDtypeStruct((batch * seq, GV), BF16),
                   jax.ShapeDtypeStruct((batch * seq, RV), BF16),
                   jax.ShapeDtypeStruct((batch, GLA_HEADS, GLA_DK, GLA_DV), F32),
                   jax.ShapeDtypeStruct((batch, RET_HEADS, RET_DK, RET_DV), F32)],
        compiler_params=pltpu.CompilerParams(dimension_semantics=("parallel", "arbitrary"),
                                             vmem_limit_bytes=VMEM_LIMIT),
        name="mixer_prompt",
    )(p, p, p, p, p, p, p, la, cos, sin, jnp.asarray(d_intra), jnp.asarray(d_read), jnp.asarray(d_write),
      gla_gain, ret_gain)


def _pad_rows(x, rows):
    return jnp.concatenate([x, jnp.zeros((rows - x.shape[0], x.shape[1]), x.dtype)], axis=0)


def _mixer_sample_kernel(seq, gammas, d_chunk, vr_ref, zr_ref, qr_ref, kr_ref, vg_ref, zg_ref, qkg_ref,
                         la_ref, cos_ref, sin_ref, dread_ref, dwrite_ref, gg_ref, rg_ref,
                         sg_in, sr_in, og_ref, or_ref, sg_ref, sr_ref):
    rows = SAMPLE_ROWS
    n_el = rows // seq
    row1 = lax.broadcasted_iota(jnp.int32, (rows, 1), 0)
    pos1 = row1 % seq
    cos = cos_ref[...]
    sin = sin_ref[...]
    tcol = lax.broadcasted_iota(jnp.int32, (LANES, LANES), 1)

    def seg_cumsum(x):
        s = 1
        while s < seq:
            x = x + jnp.where(pos1 >= s, pltpu.roll(x, s, 0), 0.0)
            s *= 2
        return x

    def intra(q, k, v, gamma):
        out = jnp.zeros_like(v)
        for d in range(seq):
            kd = k if d == 0 else pltpu.roll(k, d, 0)
            vd = v if d == 0 else pltpu.roll(v, d, 0)
            a = jnp.sum(q * kd, axis=1, keepdims=True) * (gamma ** d)
            out = out + jnp.where(pos1 >= d, a, 0.0) * vd
        return out

    def inter(q, states):
        qb = _pad_rows(q, 16).astype(BF16)
        out = None
        for e in range(n_el):
            oe = _dot(qb, states[e].astype(BF16))[:rows]
            out = oe if out is None else jnp.where(row1 >= e * seq, oe, out)
        return out

    for h in range(GLA_HEADS):
        kq = slice(h * GLA_DK, (h + 1) * GLA_DK)
        kk = slice(GK + h * GLA_DK, GK + (h + 1) * GLA_DK)
        vv = slice(h * GLA_DV, (h + 1) * GLA_DV)
        q = qkg_ref[:, kq] * (GLA_DK ** -0.5)
        k = qkg_ref[:, kk]
        v = vg_ref[:, vv]
        la = la_ref[:, kq]
        b = seg_cumsum(la)
        q_dec = q * jnp.exp(b)
        k_dec = k * jnp.exp(-b)
        states = [sg_in[e, h] for e in range(n_el)]
        o = inter(q_dec, states) + intra(q_dec, k_dec, v, 1.0)
        og_ref[:, vv] = _rms_gate(o, gg_ref[:, vv], zg_ref[:, vv])
        b_last = jnp.zeros_like(b)
        for e in range(n_el):
            r = e * seq + seq - 1
            b_last = jnp.where((row1 >= e * seq) & (row1 < (e + 1) * seq),
                               jnp.broadcast_to(b[r:r + 1, :], b.shape), b_last)
        k_out = k * jnp.exp(b_last - b)
        k_outT = _pad_rows(k_out, LANES).T
        laT = _pad_rows(la, LANES).T
        vb = _pad_rows(v, LANES).astype(BF16)
        tl = tcol[:GLA_DK]
        for e in range(n_el):
            sel = (tl >= e * seq) & (tl < (e + 1) * seq)
            g_col = jnp.exp(jnp.sum(jnp.where(sel, laT, 0.0), axis=1, keepdims=True))
            sg_ref[e, h] = g_col * states[e] + _dot(jnp.where(sel, k_outT, 0.0).astype(BF16), vb)

    for h in range(RET_HEADS):
        kq = slice(h * RET_DK, (h + 1) * RET_DK)
        vv = slice(h * RET_DV, (h + 1) * RET_DV)
        q = _rotary(qr_ref[:, kq], cos, sin)
        k = _rotary(kr_ref[:, kq], cos, sin) * (RET_DK ** -0.5)
        v = vr_ref[:, vv]
        states = [sr_in[e, h] for e in range(n_el)]
        o = inter(q, states) * dread_ref[h] + intra(q, k, v, gammas[h])
        or_ref[:, vv] = _group_gate(o, rg_ref[:, vv], zr_ref[:, vv])
        kwT = _pad_rows(k * dwrite_ref[h], LANES).T
        vb = _pad_rows(v, LANES).astype(BF16)
        tl = jnp.concatenate([tcol, tcol], axis=0)
        for e in range(n_el):
            sel = (tl >= e * seq) & (tl < (e + 1) * seq)
            sr_ref[e, h] = d_chunk[h] * states[e] + _dot(jnp.where(sel, kwT, 0.0).astype(BF16), vb)


def _mixer_sample(p, la, gla_gain, ret_gain, state_gla, state_ret, batch, seq):
    rows = SAMPLE_ROWS
    n_el = rows // seq
    _, d_read, d_write, d_chunk = _ret_consts(seq)
    gammas = tuple(float(1.0 - 2.0 ** (-5.0 - h)) for h in range(RET_HEADS))
    cos, sin = _rope_tables(PAST_LEN + jnp.arange(seq, dtype=jnp.int32))
    cos, sin = jnp.tile(cos, (n_el, 1)), jnp.tile(sin, (n_el, 1))
    d_read = np.tile(d_read, (1, n_el, 1))
    d_write = np.tile(d_write, (1, n_el, 1))

    pcol = lambda width, blk: pl.BlockSpec((rows, width), lambda i: (i, blk))
    tok = lambda width: pl.BlockSpec((rows, width), lambda i: (i, 0))
    full2 = lambda shape: pl.BlockSpec(shape, lambda i: (0, 0))
    full3 = lambda shape: pl.BlockSpec(shape, lambda i: (0, 0, 0))
    sg_spec = pl.BlockSpec((n_el, GLA_HEADS, GLA_DK, GLA_DV), lambda i: (i, 0, 0, 0))
    sr_spec = pl.BlockSpec((n_el, RET_HEADS, RET_DK, RET_DV), lambda i: (i, 0, 0, 0))
    return pl.pallas_call(
        functools.partial(_mixer_sample_kernel, seq, gammas, d_chunk),
        grid=(batch * seq // rows,),
        in_specs=[pcol(RV, BLK_VR), pcol(RV, BLK_ZR), pcol(RK, BLK_QR), pcol(RK, BLK_KR),
                  pcol(GV, BLK_VG), pcol(GV, BLK_ZG), pcol(2 * GK, BLK_QKG), tok(GK),
                  full2((rows, RET_DK // 2)), full2((rows, RET_DK // 2)),
                  full3((RET_HEADS, rows, 1)), full3((RET_HEADS, rows, 1)),
                  full2((1, GV)), full2((1, RV)), sg_spec, sr_spec],
        out_specs=[tok(GV), tok(RV), sg_spec, sr_spec],
        out_shape=[jax.ShapeDtypeStruct((batch * seq, GV), F32),
                   jax.ShapeDtypeStruct((batch * seq, RV), F32),
                   jax.ShapeDtypeStruct(state_gla.shape, F32),
                   jax.ShapeDtypeStruct(state_ret.shape, F32)],
        compiler_params=pltpu.CompilerParams(dimension_semantics=("parallel",),
                                             vmem_limit_bytes=VMEM_LIMIT),
        name="mixer_sample",
    )(p, p, p, p, p, p, p, la, cos, sin, jnp.asarray(d_read), jnp.asarray(d_write),
      gla_gain, ret_gain, state_gla, state_ret)


def _out_kernel(og_ref, or_ref, mg_ref, mr_ref, x_ref, gate_ref, wbg_ref, wbr_ref, wo_ref, lng_ref, lnb_ref,
                y_ref):
    bg = _dot(og_ref[...].astype(BF16), wbg_ref[...])
    br = _dot(or_ref[...].astype(BF16), wbr_ref[...])
    merged = mg_ref[...].astype(F32) * bg + mr_ref[...].astype(F32) * br
    out = _dot(merged.astype(BF16), wo_ref[...])
    r = DEEPNORM_ALPHA * x_ref[...] + gate_ref[...] * out
    mu = jnp.mean(r, axis=-1, keepdims=True)
    c = r - mu
    var = jnp.mean(c * c, axis=-1, keepdims=True)
    y_ref[...] = c * lax.rsqrt(var + LN_EPS) * lng_ref[...] + lnb_ref[...]


def _out_stage(og, orr, p, x2d, gate3, wbg, wbr, wo, ln_g, ln_b, *, tm, rows_per_mod):
    n = x2d.shape[0]
    if gate3.shape[1] != 1:
        gate_spec = pl.BlockSpec((None, tm, D_MODEL), lambda i: (0, i, 0))
    else:
        tiles_per_mod = rows_per_mod // tm
        gate_spec = pl.BlockSpec((None, 1, D_MODEL), lambda i: (i // tiles_per_mod, 0, 0))
    full = lambda shape: pl.BlockSpec(shape, lambda i: (0, 0))
    return pl.pallas_call(
        _out_kernel,
        grid=(n // tm,),
        in_specs=[pl.BlockSpec((tm, GV), lambda i: (i, 0)),
                  pl.BlockSpec((tm, RV), lambda i: (i, 0)),
                  pl.BlockSpec((tm, D_MODEL), lambda i: (i, BLK_MG)),
                  pl.BlockSpec((tm, D_MODEL), lambda i: (i, BLK_MR)),
                  pl.BlockSpec((tm, D_MODEL), lambda i: (i, 0)),
                  gate_spec,
                  full((GV, D_MODEL)), full((RV, D_MODEL)), full((D_MODEL, D_MODEL)),
                  full((1, D_MODEL)), full((1, D_MODEL))],
        out_specs=pl.BlockSpec((tm, D_MODEL), lambda i: (i, 0)),
        out_shape=jax.ShapeDtypeStruct((n, D_MODEL), F32),
        compiler_params=pltpu.CompilerParams(dimension_semantics=("parallel",),
                                             vmem_limit_bytes=VMEM_LIMIT),
        name="out_stage",
    )(og, orr, p, p, x2d, gate3, wbg, wbr, wo, ln_g, ln_b)


def _regroup_w_in(w):
    cols = [w[:, _O_VR:_O_ZR], w[:, _O_QR:_O_KR], w[:, _O_KR:_O_VR], w[:, _O_VG:_O_ZG],
            w[:, _O_QG:_O_KG], w[:, _O_KG:_O_VG],
            w[:, _O_ZR:_O_MG], w[:, _O_ZG:_O_LR],
            w[:, _O_MG:_O_MR], w[:, _O_MR:_O_END]]
    w_main = jnp.concatenate(cols, axis=1).astype(BF16)
    w_lr = jnp.pad(w[:, _O_LR:_O_QR], ((0, 0), (0, LANES - GLA_LOWRANK))).astype(BF16)
    return w_main, w_lr


def kernel(x_prompt, x_sample, state_gla, state_ret, c_prompt, c_sample, w_ada, b_ada, w_in, w_lr2, b_lr2,
           gla_norm_g, ret_norm_g, w_branch_gla, w_branch_ret, w_out, ln_g, ln_b):
    assert w_ada.shape[0] == 1, "one layer"
    bp, tp, _ = x_prompt.shape
    bs, ts, _ = x_sample.shape

    mod = _ada(jnp.concatenate([c_prompt, c_sample], axis=0), w_ada[0], b_ada)
    shift, scale, gate = mod[:, :D_MODEL], mod[:, D_MODEL:2 * D_MODEL], mod[:, 2 * D_MODEL:]
    per_group = lambda a: a[:bp][:, None, :]
    per_token = lambda a: jnp.repeat(a[bp:], ts, axis=0)[None]

    w_main, w_lr = _regroup_w_in(w_in[0])
    w_lr2p = jnp.pad(w_lr2[0], ((0, LANES - GLA_LOWRANK), (0, 0))).astype(BF16)
    wbg, wbr, wo = w_branch_gla[0].astype(BF16), w_branch_ret[0].astype(BF16), w_out[0].astype(BF16)

    xp = x_prompt.reshape(bp * tp, D_MODEL)
    p_p, la_p = _proj(xp, per_group(scale), per_group(shift), w_main, w_lr, w_lr2p, b_lr2,
                      tm=2048, rows_per_mod=tp, p_dtype=BF16)
    og_p, or_p, sg_p, sr_p = _mixer_prompt(p_p, la_p, gla_norm_g, ret_norm_g, bp, tp)
    y_p = _out_stage(og_p, or_p, p_p, xp, per_group(gate), wbg, wbr, wo, ln_g, ln_b, tm=512, rows_per_mod=tp)

    xs = x_sample.reshape(bs * ts, D_MODEL)
    p_s, la_s = _proj(xs, per_token(scale), per_token(shift), w_main, w_lr, w_lr2p, b_lr2,
                      tm=bs * ts, rows_per_mod=ts, p_dtype=F32)
    og_s, or_s, sg_s, sr_s = _mixer_sample(p_s, la_s, gla_norm_g, ret_norm_g, state_gla[0], state_ret[0], bs, ts)
    y_s = _out_stage(og_s, or_s, p_s, xs, per_token(gate), wbg, wbr, wo, ln_g, ln_b,
                     tm=bs * ts, rows_per_mod=ts)

    return (y_p.reshape(bp, tp, D_MODEL), y_s.reshape(bs, ts, D_MODEL),
            sg_p[None], sr_p[None], sg_s[None], sr_s[None])
```

```python
import functools

import numpy as np
import jax
import jax.numpy as jnp
from jax import lax
from jax.experimental import pallas as pl
from jax.experimental.pallas import tpu as pltpu

F32 = jnp.float32
BF16 = jnp.bfloat16

D_MODEL = 1024
PAST_LEN = 16384
GLA_HEADS, GLA_DK, GLA_DV = 4, 128, 256
RET_HEADS, RET_DK, RET_DV = 4, 256, 512
GK, GV = GLA_HEADS * GLA_DK, GLA_HEADS * GLA_DV
RK, RV = RET_HEADS * RET_DK, RET_HEADS * RET_DV
GLA_LOWRANK = 16
GLA_TAU = 16.0
GLA_SUB = 16
ROPE_BASE = 10000.0
DEEPNORM_ALPHA = 2.0 ** 0.25
LN_EPS = 1e-5
HEAD_NORM_EPS = 1e-5

CHUNK = 128
SAMPLE_ROWS = 8
LANES = 128

PROJ_TN = 1024
PROJ_SUB = 512
OUT_SUB = 512
P_COLS = 11 * PROJ_TN
BLK_VR, BLK_ZR = 0, 3
BLK_QR, BLK_KR, BLK_VG, BLK_QKG, BLK_ZG, BLK_MG, BLK_MR = 2, 3, 4, 5, 8, 9, 10
FIRST_SILU_BLK, FIRST_SIGMOID_BLK = 6, 9
_O_QG, _O_KG, _O_VG, _O_ZG, _O_LR = 0, GK, 2 * GK, 2 * GK + GV, 2 * GK + 2 * GV
_O_QR = _O_LR + GLA_LOWRANK
_O_KR = _O_QR + RK
_O_VR = _O_KR + RK
_O_ZR = _O_VR + RV
_O_MG = _O_ZR + RV
_O_MR = _O_MG + D_MODEL
_O_END = _O_MR + D_MODEL

VMEM_LIMIT = 56 * 1024 * 1024


def _dot(a, b):
    return jnp.dot(a, b, preferred_element_type=F32)


def _dot_nt(a, b):
    return lax.dot_general(a, b, (((1,), (1,)), ((), ())), preferred_element_type=F32)


def _log_sigmoid(x):
    return -(jnp.maximum(-x, 0.0) + jnp.log(1.0 + jnp.exp(-jnp.abs(x))))


def _sigmoid(x):
    return 0.5 * jnp.tanh(0.5 * x) + 0.5


def _silu(x):
    return x * _sigmoid(x)


def _ada_kernel(c_ref, w_ref, b_ref, o_ref):
    o_ref[...] = _dot(c_ref[...].astype(BF16), w_ref[...].astype(BF16)) + b_ref[...]


def _ada(c, w_ada, b_ada):
    n = c.shape[0]
    return pl.pallas_call(
        _ada_kernel,
        grid=(3,),
        in_specs=[pl.BlockSpec((n, D_MODEL), lambda j: (0, 0)),
                  pl.BlockSpec((D_MODEL, D_MODEL), lambda j: (0, j)),
                  pl.BlockSpec((1, D_MODEL), lambda j: (0, j))],
        out_specs=pl.BlockSpec((n, D_MODEL), lambda j: (0, j)),
        out_shape=jax.ShapeDtypeStruct((n, 3 * D_MODEL), F32),
        compiler_params=pltpu.CompilerParams(dimension_semantics=("arbitrary",)),
        name="ada",
    )(c, w_ada, b_ada)


def _proj_kernel(x_ref, scale_ref, shift_ref, w_ref, wlr_ref, wlr2_ref, blr2_ref, p_ref, la_ref, h_ref):
    j = pl.program_id(1)

    @pl.when(j == 0)
    def _():
        h = (x_ref[...] * (1.0 + scale_ref[...]) + shift_ref[...]).astype(BF16)
        h_ref[...] = h
        lr = _dot(h, wlr_ref[...])
        pre = _dot(lr.astype(BF16), wlr2_ref[...]) + blr2_ref[...]
        la_ref[...] = _log_sigmoid(pre) * (1.0 / GLA_TAU)

    def emit(epilogue):
        sub = min(PROJ_SUB, p_ref.shape[0])
        for r in range(0, p_ref.shape[0], sub):
            rs = slice(r, r + sub)
            p_ref[rs, :] = epilogue(_dot(h_ref[rs, :], w_ref[...])).astype(p_ref.dtype)

    pl.when(j < FIRST_SILU_BLK)(lambda: emit(lambda a: a))
    pl.when((j >= FIRST_SILU_BLK) & (j < FIRST_SIGMOID_BLK))(lambda: emit(_silu))
    pl.when(j >= FIRST_SIGMOID_BLK)(lambda: emit(_sigmoid))


def _proj(x2d, scale3, shift3, w_main, w_lr, w_lr2, b_lr2, *, tm, rows_per_mod, p_dtype):
    n = x2d.shape[0]
    per_row = scale3.shape[1] != 1
    if per_row:
        mod_spec = pl.BlockSpec((None, tm, D_MODEL), lambda i, j: (0, i, 0))
    else:
        tiles_per_mod = rows_per_mod // tm
        mod_spec = pl.BlockSpec((None, 1, D_MODEL), lambda i, j: (i // tiles_per_mod, 0, 0))
    tn = PROJ_TN
    return pl.pallas_call(
        _proj_kernel,
        grid=(n // tm, P_COLS // tn),
        in_specs=[pl.BlockSpec((tm, D_MODEL), lambda i, j: (i, 0)),
                  mod_spec, mod_spec,
                  pl.BlockSpec((D_MODEL, tn), lambda i, j: (0, j)),
                  pl.BlockSpec((D_MODEL, LANES), lambda i, j: (0, 0)),
                  pl.BlockSpec((LANES, GK), lambda i, j: (0, 0)),
                  pl.BlockSpec((1, GK), lambda i, j: (0, 0))],
        out_specs=[pl.BlockSpec((tm, tn), lambda i, j: (i, j)),
                   pl.BlockSpec((tm, GK), lambda i, j: (i, 0))],
        out_shape=[jax.ShapeDtypeStruct((n, P_COLS), p_dtype),
                   jax.ShapeDtypeStruct((n, GK), F32)],
        scratch_shapes=[pltpu.VMEM((tm, D_MODEL), BF16)],
        compiler_params=pltpu.CompilerParams(dimension_semantics=("parallel", "arbitrary"),
                                             vmem_limit_bytes=VMEM_LIMIT),
        name="proj",
    )(x2d, scale3, shift3, w_main, w_lr, w_lr2, b_lr2)


def _rotary(x, cos, sin):
    half = x.shape[1] // 2
    x1, x2 = x[:, :half], x[:, half:]
    return jnp.concatenate([x1 * cos - x2 * sin, x1 * sin + x2 * cos], axis=1)


def _rms_gate(o, gain, sz):
    o = o * lax.rsqrt(jnp.mean(o * o, axis=-1, keepdims=True) + HEAD_NORM_EPS)
    return o * gain * sz


def _group_gate(o, gain, sz):
    mu = jnp.mean(o, axis=-1, keepdims=True)
    c = o - mu
    var = jnp.mean(c * c, axis=-1, keepdims=True)
    return c * lax.rsqrt(var + HEAD_NORM_EPS) * gain * sz


def _cumsum_rows(x):
    n = x.shape[0]
    row = lax.broadcasted_iota(jnp.int32, x.shape, 0)
    s = 1
    while s < n:
        x = x + jnp.where(row >= s, pltpu.roll(x, s, 0), 0.0)
        s *= 2
    return x


def _block_edge(b, w, prev):
    n, m = b.shape
    pieces = []
    for i in range(n // w):
        r = i * w - 1 if prev else i * w + w - 1
        if r < 0:
            pieces.append(jnp.zeros((w, m), F32))
        else:
            pieces.append(jnp.broadcast_to(b[r:r + 1, :], (w, m)))
    return jnp.concatenate(pieces, axis=0)


def _gla_levels(c):
    w = GLA_SUB
    while w < c:
        yield w
        w *= 2


def _gla_operands(q, k, la):
    c = q.shape[0]
    b = _cumsum_rows(la)
    btot = b[c - 1:c, :]
    rel = b - _block_edge(b, GLA_SUB, True)
    ops = {"q_in": (q * jnp.exp(b)).astype(BF16),
           "k_out": (k * jnp.exp(btot - b)).astype(BF16),
           "qd": (q * jnp.exp(rel)).astype(BF16),
           "kd": (k * jnp.exp(-rel)).astype(BF16)}
    for w in _gla_levels(c):
        if w > GLA_SUB:
            ops["q%d" % w] = (q * jnp.exp(b - _block_edge(b, w, True))).astype(BF16)
        ops["k%d" % w] = (k * jnp.exp(_block_edge(b, w, False) - b)).astype(BF16)
    return ops


def _gla_masks(c):
    row = lax.broadcasted_iota(jnp.int32, (c, c), 0)
    col = lax.broadcasted_iota(jnp.int32, (c, c), 1)
    sh = GLA_SUB.bit_length() - 1
    masks = {"diag": ((row >> sh) == (col >> sh)) & (col <= row)}
    for w in _gla_levels(c):
        sh = w.bit_length() - 1
        rb, cb = row >> sh, col >> sh
        masks[w] = ((rb & 1) == 1) & (cb == rb - 1)
    return masks


def _gla_scores(ops, hs, masks, c):
    a = jnp.where(masks["diag"], _dot_nt(ops["qd"][:, hs], ops["kd"][:, hs]), 0.0)
    for w in _gla_levels(c):
        qw = ops["qd"] if w == GLA_SUB else ops["q%d" % w]
        a = jnp.where(masks[w], _dot_nt(qw[:, hs], ops["k%d" % w][:, hs]), a)
    return a.astype(BF16)


def _rotary_heads(x, cos, sin, heads):
    half = cos.shape[1]
    cos_t = jnp.concatenate([cos] * (2 * heads), axis=1)
    sin_t = jnp.concatenate([-sin, sin] * heads, axis=1)
    partner = [i + 1 - 2 * (i % 2) for i in range(2 * heads)]
    swapped = jnp.concatenate([x[:, j * half:(j + 1) * half] for j in partner], axis=1)
    return x * cos_t + swapped * sin_t


def _mixer_prompt_kernel(d_chunk, vr_ref, zr_ref, qr_ref, kr_ref, vg_ref, zg_ref, qkg_ref, la_ref,
                         cos_ref, sin_ref, dintra_ref, dread_ref, dwrite_ref, gg_ref, rg_ref,
                         og_ref, or_ref, sg_ref, sr_ref):
    @pl.when(pl.program_id(1) == 0)
    def _():
        sg_ref[...] = jnp.zeros_like(sg_ref)
        sr_ref[...] = jnp.zeros_like(sr_ref)

    c = CHUNK
    g_sl = [slice(h * GLA_DK, (h + 1) * GLA_DK) for h in range(GLA_HEADS)]
    gv_sl = [slice(h * GLA_DV, (h + 1) * GLA_DV) for h in range(GLA_HEADS)]
    r_sl = [slice(h * RET_DK, (h + 1) * RET_DK) for h in range(RET_HEADS)]
    rv_sl = [slice(h * RET_DV, (h + 1) * RET_DV) for h in range(RET_HEADS)]

    la = la_ref[...]
    gops = _gla_operands(qkg_ref[:, :GK].astype(F32) * (GLA_DK ** -0.5), qkg_ref[:, GK:].astype(F32), la)
    g_cols = [jnp.exp(jnp.sum(la[:, s].T, axis=1, keepdims=True)) for s in g_sl]
    cos = cos_ref[...]
    sin = sin_ref[...]
    q_rot = _rotary_heads(qr_ref[...].astype(F32), cos, sin, RET_HEADS)
    k_rot = _rotary_heads(kr_ref[...].astype(F32), cos, sin, RET_HEADS) * (RET_DK ** -0.5)
    qb = q_rot.astype(BF16)
    kb = k_rot.astype(BF16)
    q_read = (q_rot * dread_ref[...]).astype(BF16)
    k_write = (k_rot * dwrite_ref[...]).astype(BF16)

    masks = _gla_masks(c)
    a_g = [_gla_scores(gops, s, masks, c) for s in g_sl]
    a_r = [(_dot_nt(qb[:, s], kb[:, s]) * dintra_ref[h]).astype(BF16) for h, s in enumerate(r_sl)]

    for h in range(GLA_HEADS):
        o = _dot(gops["q_in"][:, g_sl[h]], sg_ref[h].astype(BF16)) + _dot(a_g[h], vg_ref[:, gv_sl[h]])
        og_ref[:, gv_sl[h]] = _rms_gate(o, gg_ref[:, gv_sl[h]],
                                        zg_ref[:, gv_sl[h]].astype(F32)).astype(og_ref.dtype)
    for h in range(RET_HEADS):
        o = _dot(a_r[h], vr_ref[:, rv_sl[h]]) + _dot(q_read[:, r_sl[h]], sr_ref[h].astype(BF16))
        or_ref[:, rv_sl[h]] = _group_gate(o, rg_ref[:, rv_sl[h]],
                                          zr_ref[:, rv_sl[h]].astype(F32)).astype(or_ref.dtype)

    for h in range(GLA_HEADS):
        sg_ref[h] = g_cols[h] * sg_ref[h] + _dot(gops["k_out"][:, g_sl[h]].T, vg_ref[:, gv_sl[h]])
    for h in range(RET_HEADS):
        sr_ref[h] = d_chunk[h] * sr_ref[h] + _dot(k_write[:, r_sl[h]].T, vr_ref[:, rv_sl[h]])


def _ret_consts(length):
    h = np.arange(RET_HEADS, dtype=np.float64)
    log_gamma = np.log1p(-np.exp2(-5.0 - h))
    idx = np.arange(length, dtype=np.float64)
    diff = idx[:, None] - idx[None, :]
    d_intra = np.where(diff[None] >= 0, np.exp(np.maximum(diff, 0.0)[None] * log_gamma[:, None, None]), 0.0)
    d_read = np.exp((idx + 1.0)[None, :] * log_gamma[:, None])
    d_write = np.exp((length - 1.0 - idx)[None, :] * log_gamma[:, None])
    d_chunk = np.exp(length * log_gamma)
    return (d_intra.astype(np.float32), d_read.astype(np.float32)[..., None],
            d_write.astype(np.float32)[..., None], tuple(float(np.float32(x)) for x in d_chunk))


def _rope_tables(pos):
    half = RET_DK // 2
    inv_freq = ROPE_BASE ** (-jnp.arange(half, dtype=F32) / half)
    ang = pos.astype(F32)[:, None] * inv_freq[None, :]
    return jnp.cos(ang), jnp.sin(ang)


def _mixer_prompt(p, la, gla_gain, ret_gain, batch, seq):
    assert p.dtype == BF16
    nchunk = seq // CHUNK
    d_intra, d_read, d_write, d_chunk = _ret_consts(CHUNK)
    d_read = np.repeat(d_read[..., 0].T, RET_DK, axis=1)
    d_write = np.repeat(d_write[..., 0].T, RET_DK, axis=1)
    cos, sin = _rope_tables(jnp.arange(seq, dtype=jnp.int32))

    def pcol(width, blk):
        return pl.BlockSpec((CHUNK, width), lambda b, t: (b * nchunk + t, blk))

    const3 = lambda shape: pl.BlockSpec(shape, lambda b, t: (0, 0, 0))
    tok = lambda width: pl.BlockSpec((CHUNK, width), lambda b, t: (b * nchunk + t, 0))
    return pl.pallas_call(
        functools.partial(_mixer_prompt_kernel, d_chunk),
        grid=(batch, nchunk),
        in_specs=[pcol(RV, BLK_VR), pcol(RV, BLK_ZR), pcol(RK, BLK_QR), pcol(RK, BLK_KR),
                  pcol(GV, BLK_VG), pcol(GV, BLK_ZG), pcol(2 * GK, BLK_QKG), tok(GK),
                  pl.BlockSpec((CHUNK, RET_DK // 2), lambda b, t: (t, 0)),
                  pl.BlockSpec((CHUNK, RET_DK // 2), lambda b, t: (t, 0)),
                  const3((RET_HEADS, CHUNK, CHUNK)),
                  pl.BlockSpec((CHUNK, RK), lambda b, t: (0, 0)),
                  pl.BlockSpec((CHUNK, RK), lambda b, t: (0, 0)),
                  pl.BlockSpec((1, GV), lambda b, t: (0, 0)),
                  pl.BlockSpec((1, RV), lambda b, t: (0, 0))],
        out_specs=[tok(GV), tok(RV),
                   pl.BlockSpec((None, GLA_HEADS, GLA_DK, GLA_DV), lambda b, t: (b, 0, 0, 0)),
                   pl.BlockSpec((None, RET_HEADS, RET_DK, RET_DV), lambda b, t: (b, 0, 0, 0))],
        out_shape=[jax.ShapeDtypeStruct((batch * seq, GV), BF16),
                   jax.ShapeDtypeStruct((batch * seq, RV), BF16),
                   jax.ShapeDtypeStruct((batch, GLA_HEADS, GLA_DK, GLA_DV), F32),
                   jax.ShapeDtypeStruct((batch, RET_HEADS, RET_DK, RET_DV), F32)],
        compiler_params=pltpu.CompilerParams(dimension_semantics=("parallel", "arbitrary"),
                                             vmem_limit_bytes=VMEM_LIMIT),
        name="mixer_prompt",
    )(p, p, p, p, p, p, p, la, cos, sin, jnp.asarray(d_intra), jnp.asarray(d_read), jnp.asarray(d_write),
      gla_gain, ret_gain)


def _pad_rows(x, rows):
    return jnp.concatenate([x, jnp.zeros((rows - x.shape[0], x.shape[1]), x.dtype)], axis=0)


def _mixer_sample_kernel(seq, gammas, d_chunk, vr_ref, zr_ref, qr_ref, kr_ref, vg_ref, zg_ref, qkg_ref,
                         la_ref, cos_ref, sin_ref, dread_ref, dwrite_ref, gg_ref, rg_ref,
                         sg_in, sr_in, og_ref, or_ref, sg_ref, sr_ref):
    rows = SAMPLE_ROWS
    n_el = rows // seq
    row1 = lax.broadcasted_iota(jnp.int32, (rows, 1), 0)
    pos1 = row1 % seq
    cos = cos_ref[...]
    sin = sin_ref[...]
    tcol = lax.broadcasted_iota(jnp.int32, (LANES, LANES), 1)

    def seg_cumsum(x):
        s = 1
        while s < seq:
            x = x + jnp.where(pos1 >= s, pltpu.roll(x, s, 0), 0.0)
            s *= 2
        return x

    def intra(q, k, v, gamma):
        out = jnp.zeros_like(v)
        for d in range(seq):
            kd = k if d == 0 else pltpu.roll(k, d, 0)
            vd = v if d == 0 else pltpu.roll(v, d, 0)
            a = jnp.sum(q * kd, axis=1, keepdims=True) * (gamma ** d)
            out = out + jnp.where(pos1 >= d, a, 0.0) * vd
        return out

    def inter(q, states):
        qb = _pad_rows(q, 16).astype(BF16)
        out = None
        for e in range(n_el):
            oe = _dot(qb, states[e].astype(BF16))[:rows]
            out = oe if out is None else jnp.where(row1 >= e * seq, oe, out)
        return out

    for h in range(GLA_HEADS):
        kq = slice(h * GLA_DK, (h + 1) * GLA_DK)
        kk = slice(GK + h * GLA_DK, GK + (h + 1) * GLA_DK)
        vv = slice(h * GLA_DV, (h + 1) * GLA_DV)
        q = qkg_ref[:, kq] * (GLA_DK ** -0.5)
        k = qkg_ref[:, kk]
        v = vg_ref[:, vv]
        la = la_ref[:, kq]
        b = seg_cumsum(la)
        q_dec = q * jnp.exp(b)
        k_dec = k * jnp.exp(-b)
        states = [sg_in[e, h] for e in range(n_el)]
        o = inter(q_dec, states) + intra(q_dec, k_dec, v, 1.0)
        og_ref[:, vv] = _rms_gate(o, gg_ref[:, vv], zg_ref[:, vv])
        b_last = jnp.zeros_like(b)
        for e in range(n_el):
            r = e * seq + seq - 1
            b_last = jnp.where((row1 >= e * seq) & (row1 < (e + 1) * seq),
                               jnp.broadcast_to(b[r:r + 1, :], b.shape), b_last)
        k_out = k * jnp.exp(b_last - b)
        k_outT = _pad_rows(k_out, LANES).T
        laT = _pad_rows(la, LANES).T
        vb = _pad_rows(v, LANES).astype(BF16)
        tl = tcol[:GLA_DK]
        for e in range(n_el):
            sel = (tl >= e * seq) & (tl < (e + 1) * seq)
            g_col = jnp.exp(jnp.sum(jnp.where(sel, laT, 0.0), axis=1, keepdims=True))
            sg_ref[e, h] = g_col * states[e] + _dot(jnp.where(sel, k_outT, 0.0).astype(BF16), vb)

    for h in range(RET_HEADS):
        kq = slice(h * RET_DK, (h + 1) * RET_DK)
        vv = slice(h * RET_DV, (h + 1) * RET_DV)
        q = _rotary(qr_ref[:, kq], cos, sin)
        k = _rotary(kr_ref[:, kq], cos, sin) * (RET_DK ** -0.5)
        v = vr_ref[:, vv]
        states = [sr_in[e, h] for e in range(n_el)]
        o = inter(q, states) * dread_ref[h] + intra(q, k, v, gammas[h])
        or_ref[:, vv] = _group_gate(o, rg_ref[:, vv], zr_ref[:, vv])
        kwT = _pad_rows(k * dwrite_ref[h], LANES).T
        vb = _pad_rows(v, LANES).astype(BF16)
        tl = jnp.concatenate([tcol, tcol], axis=0)
        for e in range(n_el):
            sel = (tl >= e * seq) & (tl < (e + 1) * seq)
            sr_ref[e, h] = d_chunk[h] * states[e] + _dot(jnp.where(sel, kwT, 0.0).astype(BF16), vb)


def _mixer_sample(p, la, gla_gain, ret_gain, state_gla, state_ret, batch, seq):
    rows = SAMPLE_ROWS
    n_el = rows // seq
    _, d_read, d_write, d_chunk = _ret_consts(seq)
    gammas = tuple(float(1.0 - 2.0 ** (-5.0 - h)) for h in range(RET_HEADS))
    cos, sin = _rope_tables(PAST_LEN + jnp.arange(seq, dtype=jnp.int32))
    cos, sin = jnp.tile(cos, (n_el, 1)), jnp.tile(sin, (n_el, 1))
    d_read = np.tile(d_read, (1, n_el, 1))
    d_write = np.tile(d_write, (1, n_el, 1))

    pcol = lambda width, blk: pl.BlockSpec((rows, width), lambda i: (i, blk))
    tok = lambda width: pl.BlockSpec((rows, width), lambda i: (i, 0))
    full2 = lambda shape: pl.BlockSpec(shape, lambda i: (0, 0))
    full3 = lambda shape: pl.BlockSpec(shape, lambda i: (0, 0, 0))
    sg_spec = pl.BlockSpec((n_el, GLA_HEADS, GLA_DK, GLA_DV), lambda i: (i, 0, 0, 0))
    sr_spec = pl.BlockSpec((n_el, RET_HEADS, RET_DK, RET_DV), lambda i: (i, 0, 0, 0))
    return pl.pallas_call(
        functools.partial(_mixer_sample_kernel, seq, gammas, d_chunk),
        grid=(batch * seq // rows,),
        in_specs=[pcol(RV, BLK_VR), pcol(RV, BLK_ZR), pcol(RK, BLK_QR), pcol(RK, BLK_KR),
                  pcol(GV, BLK_VG), pcol(GV, BLK_ZG), pcol(2 * GK, BLK_QKG), tok(GK),
                  full2((rows, RET_DK // 2)), full2((rows, RET_DK // 2)),
                  full3((RET_HEADS, rows, 1)), full3((RET_HEADS, rows, 1)),
                  full2((1, GV)), full2((1, RV)), sg_spec, sr_spec],
        out_specs=[tok(GV), tok(RV), sg_spec, sr_spec],
        out_shape=[jax.ShapeDtypeStruct((batch * seq, GV), F32),
                   jax.ShapeDtypeStruct((batch * seq, RV), F32),
                   jax.ShapeDtypeStruct(state_gla.shape, F32),
                   jax.ShapeDtypeStruct(state_ret.shape, F32)],
        compiler_params=pltpu.CompilerParams(dimension_semantics=("parallel",),
                                             vmem_limit_bytes=VMEM_LIMIT),
        name="mixer_sample",
    )(p, p, p, p, p, p, p, la, cos, sin, jnp.asarray(d_read), jnp.asarray(d_write),
      gla_gain, ret_gain, state_gla, state_ret)


def _out_kernel(og_ref, or_ref, mg_ref, mr_ref, x_ref, gate_ref, wbg_ref, wbr_ref, wo_ref, lng_ref, lnb_ref,
                y_ref):
    sub = min(OUT_SUB, y_ref.shape[0])
    blocks = [slice(r0, r0 + sub) for r0 in range(0, y_ref.shape[0], sub)]
    merged = []
    for rs in blocks:
        bg = _dot(og_ref[rs, :].astype(BF16), wbg_ref[...])
        br = _dot(or_ref[rs, :].astype(BF16), wbr_ref[...])
        merged.append((mg_ref[rs, :].astype(F32) * bg + mr_ref[rs, :].astype(F32) * br).astype(BF16))
    for rs, m in zip(blocks, merged):
        out = _dot(m, wo_ref[...])
        gate = gate_ref[...] if gate_ref.shape[0] == 1 else gate_ref[rs, :]
        r = DEEPNORM_ALPHA * x_ref[rs, :] + gate * out
        mu = jnp.mean(r, axis=-1, keepdims=True)
        c = r - mu
        var = jnp.mean(c * c, axis=-1, keepdims=True)
        y_ref[rs, :] = c * lax.rsqrt(var + LN_EPS) * lng_ref[...] + lnb_ref[...]


def _out_stage(og, orr, p, x2d, gate3, wbg, wbr, wo, ln_g, ln_b, *, tm, rows_per_mod):
    n = x2d.shape[0]
    if gate3.shape[1] != 1:
        gate_spec = pl.BlockSpec((None, tm, D_MODEL), lambda i: (0, i, 0))
    else:
        tiles_per_mod = rows_per_mod // tm
        gate_spec = pl.BlockSpec((None, 1, D_MODEL), lambda i: (i // tiles_per_mod, 0, 0))
    full = lambda shape: pl.BlockSpec(shape, lambda i: (0, 0), pipeline_mode=pl.Buffered(1))
    return pl.pallas_call(
        _out_kernel,
        grid=(n // tm,),
        in_specs=[pl.BlockSpec((tm, GV), lambda i: (i, 0)),
                  pl.BlockSpec((tm, RV), lambda i: (i, 0)),
                  pl.BlockSpec((tm, D_MODEL), lambda i: (i, BLK_MG)),
                  pl.BlockSpec((tm, D_MODEL), lambda i: (i, BLK_MR)),
                  pl.BlockSpec((tm, D_MODEL), lambda i: (i, 0)),
                  gate_spec,
                  full((GV, D_MODEL)), full((RV, D_MODEL)), full((D_MODEL, D_MODEL)),
                  full((1, D_MODEL)), full((1, D_MODEL))],
        out_specs=pl.BlockSpec((tm, D_MODEL), lambda i: (i, 0)),
        out_shape=jax.ShapeDtypeStruct((n, D_MODEL), F32),
        compiler_params=pltpu.CompilerParams(dimension_semantics=("parallel",),
                                             vmem_limit_bytes=VMEM_LIMIT),
        name="out_stage",
    )(og, orr, p, p, x2d, gate3, wbg, wbr, wo, ln_g, ln_b)


def _regroup_w_in(w):
    cols = [w[:, _O_VR:_O_ZR], w[:, _O_QR:_O_KR], w[:, _O_KR:_O_VR], w[:, _O_VG:_O_ZG],
            w[:, _O_QG:_O_KG], w[:, _O_KG:_O_VG],
            w[:, _O_ZR:_O_MG], w[:, _O_ZG:_O_LR],
            w[:, _O_MG:_O_MR], w[:, _O_MR:_O_END]]
    w_main = jnp.concatenate(cols, axis=1).astype(BF16)
    w_lr = jnp.pad(w[:, _O_LR:_O_QR], ((0, 0), (0, LANES - GLA_LOWRANK))).astype(BF16)
    return w_main, w_lr


def kernel(x_prompt, x_sample, state_gla, state_ret, c_prompt, c_sample, w_ada, b_ada, w_in, w_lr2, b_lr2,
           gla_norm_g, ret_norm_g, w_branch_gla, w_branch_ret, w_out, ln_g, ln_b):
    assert w_ada.shape[0] == 1, "one layer"
    bp, tp, _ = x_prompt.shape
    bs, ts, _ = x_sample.shape

    mod = _ada(jnp.concatenate([c_prompt, c_sample], axis=0), w_ada[0], b_ada)
    shift, scale, gate = mod[:, :D_MODEL], mod[:, D_MODEL:2 * D_MODEL], mod[:, 2 * D_MODEL:]
    per_group = lambda a: a[:bp][:, None, :]
    per_token = lambda a: jnp.repeat(a[bp:], ts, axis=0)[None]

    w_main, w_lr = _regroup_w_in(w_in[0])
    w_lr2p = jnp.pad(w_lr2[0], ((0, LANES - GLA_LOWRANK), (0, 0))).astype(BF16)
    wbg, wbr, wo = w_branch_gla[0].astype(BF16), w_branch_ret[0].astype(BF16), w_out[0].astype(BF16)

    xp = x_prompt.reshape(bp * tp, D_MODEL)
    p_p, la_p = _proj(xp, per_group(scale), per_group(shift), w_main, w_lr, w_lr2p, b_lr2,
                      tm=2048, rows_per_mod=tp, p_dtype=BF16)
    og_p, or_p, sg_p, sr_p = _mixer_prompt(p_p, la_p, gla_norm_g, ret_norm_g, bp, tp)
    y_p = _out_stage(og_p, or_p, p_p, xp, per_group(gate), wbg, wbr, wo, ln_g, ln_b, tm=1024, rows_per_mod=tp)

    xs = x_sample.reshape(bs * ts, D_MODEL)
    p_s, la_s = _proj(xs, per_token(scale), per_token(shift), w_main, w_lr, w_lr2p, b_lr2,
                      tm=bs * ts, rows_per_mod=ts, p_dtype=F32)
    og_s, or_s, sg_s, sr_s = _mixer_sample(p_s, la_s, gla_norm_g, ret_norm_g, state_gla[0], state_ret[0], bs, ts)
    y_s = _out_stage(og_s, or_s, p_s, xs, per_token(gate), wbg, wbr, wo, ln_g, ln_b,
                     tm=bs * ts, rows_per_mod=ts)

    return (y_p.reshape(bp, tp, D_MODEL), y_s.reshape(bs, ts, D_MODEL),
            sg_p[None], sr_p[None], sg_s[None], sr_s[None])
```

```python
import functools

import numpy as np
import jax
import jax.numpy as jnp
from jax import lax
from jax.experimental import pallas as pl
from jax.experimental.pallas import tpu as pltpu

F32 = jnp.float32
BF16 = jnp.bfloat16

D_MODEL = 1024
PAST_LEN = 16384
GLA_HEADS, GLA_DK, GLA_DV = 4, 128, 256
RET_HEADS, RET_DK, RET_DV = 4, 256, 512
GK, GV = GLA_HEADS * GLA_DK, GLA_HEADS * GLA_DV
RK, RV = RET_HEADS * RET_DK, RET_HEADS * RET_DV
GLA_LOWRANK = 16
GLA_TAU = 16.0
GLA_SUB = 16
ROPE_BASE = 10000.0
DEEPNORM_ALPHA = 2.0 ** 0.25
LN_EPS = 1e-5
HEAD_NORM_EPS = 1e-5

CHUNK = 128
SAMPLE_ROWS = 8
LANES = 128

PROJ_TN = 1024
PROJ_SUB = 512
OUT_SUB = 512
P_COLS = 11 * PROJ_TN
BLK_VR, BLK_ZR = 0, 3
BLK_QR, BLK_KR, BLK_VG, BLK_QKG, BLK_ZG, BLK_MG, BLK_MR = 2, 3, 4, 5, 8, 9, 10
FIRST_SILU_BLK, FIRST_SIGMOID_BLK = 6, 9
_O_QG, _O_KG, _O_VG, _O_ZG, _O_LR = 0, GK, 2 * GK, 2 * GK + GV, 2 * GK + 2 * GV
_O_QR = _O_LR + GLA_LOWRANK
_O_KR = _O_QR + RK
_O_VR = _O_KR + RK
_O_ZR = _O_VR + RV
_O_MG = _O_ZR + RV
_O_MR = _O_MG + D_MODEL
_O_END = _O_MR + D_MODEL

VMEM_LIMIT = 56 * 1024 * 1024


def _dot(a, b):
    return jnp.dot(a, b, preferred_element_type=F32)


def _dot_nt(a, b):
    return lax.dot_general(a, b, (((1,), (1,)), ((), ())), preferred_element_type=F32)


def _log_sigmoid(x):
    return -(jnp.maximum(-x, 0.0) + jnp.log(1.0 + jnp.exp(-jnp.abs(x))))


def _sigmoid(x):
    return 0.5 * jnp.tanh(0.5 * x) + 0.5


def _silu(x):
    return x * _sigmoid(x)


def _ada_kernel(c_ref, w_ref, b_ref, o_ref):
    o_ref[...] = _dot(c_ref[...].astype(BF16), w_ref[...].astype(BF16)) + b_ref[...]


def _ada(c, w_ada, b_ada):
    n = c.shape[0]
    return pl.pallas_call(
        _ada_kernel,
        grid=(3,),
        in_specs=[pl.BlockSpec((n, D_MODEL), lambda j: (0, 0)),
                  pl.BlockSpec((D_MODEL, D_MODEL), lambda j: (0, j)),
                  pl.BlockSpec((1, D_MODEL), lambda j: (0, j))],
        out_specs=pl.BlockSpec((n, D_MODEL), lambda j: (0, j)),
        out_shape=jax.ShapeDtypeStruct((n, 3 * D_MODEL), F32),
        compiler_params=pltpu.CompilerParams(dimension_semantics=("arbitrary",)),
        name="ada",
    )(c, w_ada, b_ada)


def _proj_kernel(cum_chunk, x_ref, scale_ref, shift_ref, w_ref, wlr_ref, wlr2_ref, blr2_ref, p_ref, la_ref,
                 h_ref):
    j = pl.program_id(1)

    @pl.when(j == 0)
    def _():
        h = (x_ref[...] * (1.0 + scale_ref[...]) + shift_ref[...]).astype(BF16)
        h_ref[...] = h
        lr = _dot_nt(h, wlr_ref[...])
        pre = _dot(lr.astype(BF16), wlr2_ref[...]) + blr2_ref[...]
        la = _log_sigmoid(pre) * (1.0 / GLA_TAU)
        if cum_chunk:
            for r in range(0, la.shape[0], cum_chunk):
                la_ref[r:r + cum_chunk, :] = _cumsum_rows_mxu(la[r:r + cum_chunk, :])
        else:
            la_ref[...] = la

    def emit(epilogue):
        sub = min(PROJ_SUB, p_ref.shape[0])
        for r in range(0, p_ref.shape[0], sub):
            rs = slice(r, r + sub)
            p_ref[rs, :] = epilogue(_dot_nt(h_ref[rs, :], w_ref[...])).astype(p_ref.dtype)

    pl.when(j < FIRST_SILU_BLK)(lambda: emit(lambda a: a))
    pl.when((j >= FIRST_SILU_BLK) & (j < FIRST_SIGMOID_BLK))(lambda: emit(_silu))
    pl.when(j >= FIRST_SIGMOID_BLK)(lambda: emit(_sigmoid))


def _proj(x2d, scale3, shift3, w_main, w_lr, w_lr2, b_lr2, *, tm, rows_per_mod, p_dtype, cum_chunk=0):
    assert cum_chunk == 0 or (tm % cum_chunk == 0 and rows_per_mod % cum_chunk == 0)
    n = x2d.shape[0]
    per_row = scale3.shape[1] != 1
    if per_row:
        mod_spec = pl.BlockSpec((None, tm, D_MODEL), lambda i, j: (0, i, 0))
    else:
        tiles_per_mod = rows_per_mod // tm
        mod_spec = pl.BlockSpec((None, 1, D_MODEL), lambda i, j: (i // tiles_per_mod, 0, 0))
    tn = PROJ_TN
    return pl.pallas_call(
        functools.partial(_proj_kernel, cum_chunk),
        grid=(n // tm, P_COLS // tn),
        in_specs=[pl.BlockSpec((tm, D_MODEL), lambda i, j: (i, 0)),
                  mod_spec, mod_spec,
                  pl.BlockSpec((tn, D_MODEL), lambda i, j: (j, 0)),
                  pl.BlockSpec((LANES, D_MODEL), lambda i, j: (0, 0)),
                  pl.BlockSpec((LANES, GK), lambda i, j: (0, 0)),
                  pl.BlockSpec((1, GK), lambda i, j: (0, 0))],
        out_specs=[pl.BlockSpec((tm, tn), lambda i, j: (i, j)),
                   pl.BlockSpec((tm, GK), lambda i, j: (i, 0))],
        out_shape=[jax.ShapeDtypeStruct((n, P_COLS), p_dtype),
                   jax.ShapeDtypeStruct((n, GK), F32)],
        scratch_shapes=[pltpu.VMEM((tm, D_MODEL), BF16)],
        compiler_params=pltpu.CompilerParams(dimension_semantics=("parallel", "arbitrary"),
                                             vmem_limit_bytes=VMEM_LIMIT),
        name="proj",
    )(x2d, scale3, shift3, w_main, w_lr, w_lr2, b_lr2)


def _rotary(x, cos, sin):
    half = x.shape[1] // 2
    x1, x2 = x[:, :half], x[:, half:]
    return jnp.concatenate([x1 * cos - x2 * sin, x1 * sin + x2 * cos], axis=1)


def _rms_gate(o, sz):
    o = o * lax.rsqrt(jnp.mean(o * o, axis=-1, keepdims=True) + HEAD_NORM_EPS)
    return o.astype(sz.dtype) * sz


def _group_gate(o, sz):
    mu = jnp.mean(o, axis=-1, keepdims=True)
    c = o - mu
    var = jnp.mean(c * c, axis=-1, keepdims=True)
    return (c * lax.rsqrt(var + HEAD_NORM_EPS)).astype(sz.dtype) * sz


def _cumsum_rows(x):
    n = x.shape[0]
    row = lax.broadcasted_iota(jnp.int32, x.shape, 0)
    s = 1
    while s < n:
        x = x + jnp.where(row >= s, pltpu.roll(x, s, 0), 0.0)
        s *= 2
    return x


def _cumsum_rows_mxu(x):
    n = x.shape[0]
    row = lax.broadcasted_iota(jnp.int32, (n, n), 0)
    col = lax.broadcasted_iota(jnp.int32, (n, n), 1)
    tri = jnp.where(col <= row, 1.0, 0.0).astype(BF16)
    hi = x.astype(BF16)
    rest = x - hi.astype(F32)
    mid = rest.astype(BF16)
    lo = (rest - mid.astype(F32)).astype(BF16)
    return _dot(tri, hi) + _dot(tri, mid) + _dot(tri, lo)


def _block_edge(b, w, prev):
    n, m = b.shape
    pieces = []
    for i in range(n // w):
        r = i * w - 1 if prev else i * w + w - 1
        if r < 0:
            pieces.append(jnp.zeros((w, m), F32))
        else:
            pieces.append(jnp.broadcast_to(b[r:r + 1, :], (w, m)))
    return jnp.concatenate(pieces, axis=0)


def _gla_levels(c):
    w = GLA_SUB
    while w < c:
        yield w
        w *= 2


def _gla_operands(q, k, b):
    c = q.shape[0]
    btot = b[c - 1:c, :]
    rel = b - _block_edge(b, GLA_SUB, True)
    ops = {"q_in": (q * jnp.exp(b)).astype(BF16),
           "k_out": (k * jnp.exp(btot - b)).astype(BF16),
           "qd": (q * jnp.exp(rel)).astype(BF16),
           "kd": (k * jnp.exp(-rel)).astype(BF16)}
    for w in _gla_levels(c):
        if w > GLA_SUB:
            ops["q%d" % w] = (q * jnp.exp(b - _block_edge(b, w, True))).astype(BF16)
        ops["k%d" % w] = (k * jnp.exp(_block_edge(b, w, False) - b)).astype(BF16)
    return ops


def _gla_masks(c):
    row = lax.broadcasted_iota(jnp.int32, (c, c), 0)
    col = lax.broadcasted_iota(jnp.int32, (c, c), 1)
    sh = GLA_SUB.bit_length() - 1
    masks = {"diag": ((row >> sh) == (col >> sh)) & (col <= row)}
    for w in _gla_levels(c):
        sh = w.bit_length() - 1
        rb, cb = row >> sh, col >> sh
        masks[w] = ((rb & 1) == 1) & (cb == rb - 1)
    return masks


def _gla_scores(ops, hs, masks, c):
    a = jnp.where(masks["diag"], _dot_nt(ops["qd"][:, hs], ops["kd"][:, hs]), 0.0)
    for w in _gla_levels(c):
        qw = ops["qd"] if w == GLA_SUB else ops["q%d" % w]
        a = jnp.where(masks[w], _dot_nt(qw[:, hs], ops["k%d" % w][:, hs]), a)
    return a.astype(BF16)


def _rotary_heads(x, cos, sin, heads):
    half = cos.shape[1]
    cos_t = jnp.concatenate([cos] * (2 * heads), axis=1)
    sin_t = jnp.concatenate([-sin, sin] * heads, axis=1)
    partner = [i + 1 - 2 * (i % 2) for i in range(2 * heads)]
    swapped = jnp.concatenate([x[:, j * half:(j + 1) * half] for j in partner], axis=1)
    return x * cos_t + swapped * sin_t


def _mixer_prompt_kernel(d_chunk, vr_ref, zr_ref, qr_ref, kr_ref, vg_ref, zg_ref, qkg_ref, b_ref,
                         cos_ref, sin_ref, dintra_ref, dread_ref, dwrite_ref,
                         og_ref, or_ref, sg_ref, sr_ref):
    @pl.when(pl.program_id(1) == 0)
    def _():
        sg_ref[...] = jnp.zeros_like(sg_ref)
        sr_ref[...] = jnp.zeros_like(sr_ref)

    c = CHUNK
    g_sl = [slice(h * GLA_DK, (h + 1) * GLA_DK) for h in range(GLA_HEADS)]
    gv_sl = [slice(h * GLA_DV, (h + 1) * GLA_DV) for h in range(GLA_HEADS)]
    r_sl = [slice(h * RET_DK, (h + 1) * RET_DK) for h in range(RET_HEADS)]
    rv_sl = [slice(h * RET_DV, (h + 1) * RET_DV) for h in range(RET_HEADS)]

    b = b_ref[...]
    cos = cos_ref[...]
    sin = sin_ref[...]
    q_rot = _rotary_heads(qr_ref[...].astype(F32), cos, sin, RET_HEADS)
    k_rot = _rotary_heads(kr_ref[...].astype(F32), cos, sin, RET_HEADS) * (RET_DK ** -0.5)
    qb = q_rot.astype(BF16)
    kb = k_rot.astype(BF16)
    q_read = (q_rot * dread_ref[...]).astype(BF16)
    k_write = (k_rot * dwrite_ref[...]).astype(BF16)
    g_rows = jnp.exp(b[c - 1:c, :])
    g_cols = [jnp.broadcast_to(g_rows[:, s], (GLA_DK, GLA_DK)).T[:, :1] for s in g_sl]
    gops = _gla_operands(qkg_ref[:, :GK].astype(F32) * (GLA_DK ** -0.5), qkg_ref[:, GK:].astype(F32), b)

    masks = _gla_masks(c)
    a_g = [_gla_scores(gops, s, masks, c) for s in g_sl]
    a_r = [(_dot_nt(qb[:, s], kb[:, s]) * dintra_ref[h]).astype(BF16) for h, s in enumerate(r_sl)]

    for h in range(GLA_HEADS):
        o = _dot(gops["q_in"][:, g_sl[h]], sg_ref[h].astype(BF16)) + _dot(a_g[h], vg_ref[:, gv_sl[h]])
        og_ref[:, gv_sl[h]] = _rms_gate(o, zg_ref[:, gv_sl[h]])
    for h in range(RET_HEADS):
        o = _dot(a_r[h], vr_ref[:, rv_sl[h]]) + _dot(q_read[:, r_sl[h]], sr_ref[h].astype(BF16))
        or_ref[:, rv_sl[h]] = _group_gate(o, zr_ref[:, rv_sl[h]])

    for h in range(GLA_HEADS):
        sg_ref[h] = g_cols[h] * sg_ref[h] + _dot(gops["k_out"][:, g_sl[h]].T, vg_ref[:, gv_sl[h]])
    for h in range(RET_HEADS):
        sr_ref[h] = d_chunk[h] * sr_ref[h] + _dot(k_write[:, r_sl[h]].T, vr_ref[:, rv_sl[h]])


def _ret_consts(length):
    h = np.arange(RET_HEADS, dtype=np.float64)
    log_gamma = np.log1p(-np.exp2(-5.0 - h))
    idx = np.arange(length, dtype=np.float64)
    diff = idx[:, None] - idx[None, :]
    d_intra = np.where(diff[None] >= 0, np.exp(np.maximum(diff, 0.0)[None] * log_gamma[:, None, None]), 0.0)
    d_read = np.exp((idx + 1.0)[None, :] * log_gamma[:, None])
    d_write = np.exp((length - 1.0 - idx)[None, :] * log_gamma[:, None])
    d_chunk = np.exp(length * log_gamma)
    return (d_intra.astype(np.float32), d_read.astype(np.float32)[..., None],
            d_write.astype(np.float32)[..., None], tuple(float(np.float32(x)) for x in d_chunk))


def _rope_tables(pos):
    half = RET_DK // 2
    inv_freq = ROPE_BASE ** (-jnp.arange(half, dtype=F32) / half)
    ang = pos.astype(F32)[:, None] * inv_freq[None, :]
    return jnp.cos(ang), jnp.sin(ang)


def _mixer_prompt(p, la, batch, seq):
    assert p.dtype == BF16
    nchunk = seq // CHUNK
    d_intra, d_read, d_write, d_chunk = _ret_consts(CHUNK)
    d_read = np.repeat(d_read[..., 0].T, RET_DK, axis=1)
    d_write = np.repeat(d_write[..., 0].T, RET_DK, axis=1)
    cos, sin = _rope_tables(jnp.arange(seq, dtype=jnp.int32))

    def pcol(width, blk):
        return pl.BlockSpec((CHUNK, width), lambda b, t: (b * nchunk + t, blk))

    const3 = lambda shape: pl.BlockSpec(shape, lambda b, t: (0, 0, 0))
    tok = lambda width: pl.BlockSpec((CHUNK, width), lambda b, t: (b * nchunk + t, 0))
    return pl.pallas_call(
        functools.partial(_mixer_prompt_kernel, d_chunk),
        grid=(batch, nchunk),
        in_specs=[pcol(RV, BLK_VR), pcol(RV, BLK_ZR), pcol(RK, BLK_QR), pcol(RK, BLK_KR),
                  pcol(GV, BLK_VG), pcol(GV, BLK_ZG), pcol(2 * GK, BLK_QKG), tok(GK),
                  pl.BlockSpec((CHUNK, RET_DK // 2), lambda b, t: (t, 0)),
                  pl.BlockSpec((CHUNK, RET_DK // 2), lambda b, t: (t, 0)),
                  const3((RET_HEADS, CHUNK, CHUNK)),
                  pl.BlockSpec((CHUNK, RK), lambda b, t: (0, 0)),
                  pl.BlockSpec((CHUNK, RK), lambda b, t: (0, 0))],
        out_specs=[tok(GV), tok(RV),
                   pl.BlockSpec((None, GLA_HEADS, GLA_DK, GLA_DV), lambda b, t: (b, 0, 0, 0)),
                   pl.BlockSpec((None, RET_HEADS, RET_DK, RET_DV), lambda b, t: (b, 0, 0, 0))],
        out_shape=[jax.ShapeDtypeStruct((batch * seq, GV), BF16),
                   jax.ShapeDtypeStruct((batch * seq, RV), BF16),
                   jax.ShapeDtypeStruct((batch, GLA_HEADS, GLA_DK, GLA_DV), F32),
                   jax.ShapeDtypeStruct((batch, RET_HEADS, RET_DK, RET_DV), F32)],
        compiler_params=pltpu.CompilerParams(dimension_semantics=("parallel", "arbitrary"),
                                             vmem_limit_bytes=VMEM_LIMIT),
        name="mixer_prompt",
    )(p, p, p, p, p, p, p, la, cos, sin, jnp.asarray(d_intra), jnp.asarray(d_read), jnp.asarray(d_write))


def _pad_rows(x, rows):
    return jnp.concatenate([x, jnp.zeros((rows - x.shape[0], x.shape[1]), x.dtype)], axis=0)


def _mixer_sample_kernel(seq, gammas, d_chunk, vr_ref, zr_ref, qr_ref, kr_ref, vg_ref, zg_ref, qkg_ref,
                         la_ref, cos_ref, sin_ref, dread_ref, dwrite_ref,
                         sg_in, sr_in, og_ref, or_ref, sg_ref, sr_ref):
    rows = SAMPLE_ROWS
    n_el = rows // seq
    row1 = lax.broadcasted_iota(jnp.int32, (rows, 1), 0)
    pos1 = row1 % seq
    cos = cos_ref[...]
    sin = sin_ref[...]
    tcol = lax.broadcasted_iota(jnp.int32, (LANES, LANES), 1)

    def seg_cumsum(x):
        s = 1
        while s < seq:
            x = x + jnp.where(pos1 >= s, pltpu.roll(x, s, 0), 0.0)
            s *= 2
        return x

    def intra(q, k, v, gamma):
        out = jnp.zeros_like(v)
        for d in range(seq):
            kd = k if d == 0 else pltpu.roll(k, d, 0)
            vd = v if d == 0 else pltpu.roll(v, d, 0)
            a = jnp.sum(q * kd, axis=1, keepdims=True) * (gamma ** d)
            out = out + jnp.where(pos1 >= d, a, 0.0) * vd
        return out

    def inter(q, states):
        qb = _pad_rows(q, 16).astype(BF16)
        out = None
        for e in range(n_el):
            oe = _dot(qb, states[e].astype(BF16))[:rows]
            out = oe if out is None else jnp.where(row1 >= e * seq, oe, out)
        return out

    for h in range(GLA_HEADS):
        kq = slice(h * GLA_DK, (h + 1) * GLA_DK)
        kk = slice(GK + h * GLA_DK, GK + (h + 1) * GLA_DK)
        vv = slice(h * GLA_DV, (h + 1) * GLA_DV)
        q = qkg_ref[:, kq] * (GLA_DK ** -0.5)
        k = qkg_ref[:, kk]
        v = vg_ref[:, vv]
        la = la_ref[:, kq]
        b = seg_cumsum(la)
        q_dec = q * jnp.exp(b)
        k_dec = k * jnp.exp(-b)
        states = [sg_in[e, h] for e in range(n_el)]
        o = inter(q_dec, states) + intra(q_dec, k_dec, v, 1.0)
        og_ref[:, vv] = _rms_gate(o, zg_ref[:, vv])
        b_last = jnp.zeros_like(b)
        for e in range(n_el):
            r = e * seq + seq - 1
            b_last = jnp.where((row1 >= e * seq) & (row1 < (e + 1) * seq),
                               jnp.broadcast_to(b[r:r + 1, :], b.shape), b_last)
        k_out = k * jnp.exp(b_last - b)
        k_outT = _pad_rows(k_out, LANES).T
        laT = _pad_rows(la, LANES).T
        vb = _pad_rows(v, LANES).astype(BF16)
        tl = tcol[:GLA_DK]
        for e in range(n_el):
            sel = (tl >= e * seq) & (tl < (e + 1) * seq)
            g_col = jnp.exp(jnp.sum(jnp.where(sel, laT, 0.0), axis=1, keepdims=True))
            sg_ref[e, h] = g_col * states[e] + _dot(jnp.where(sel, k_outT, 0.0).astype(BF16), vb)

    for h in range(RET_HEADS):
        kq = slice(h * RET_DK, (h + 1) * RET_DK)
        vv = slice(h * RET_DV, (h + 1) * RET_DV)
        q = _rotary(qr_ref[:, kq], cos, sin)
        k = _rotary(kr_ref[:, kq], cos, sin) * (RET_DK ** -0.5)
        v = vr_ref[:, vv]
        states = [sr_in[e, h] for e in range(n_el)]
        o = inter(q, states) * dread_ref[h] + intra(q, k, v, gammas[h])
        or_ref[:, vv] = _group_gate(o, zr_ref[:, vv])
        kwT = _pad_rows(k * dwrite_ref[h], LANES).T
        vb = _pad_rows(v, LANES).astype(BF16)
        tl = jnp.concatenate([tcol, tcol], axis=0)
        for e in range(n_el):
            sel = (tl >= e * seq) & (tl < (e + 1) * seq)
            sr_ref[e, h] = d_chunk[h] * states[e] + _dot(jnp.where(sel, kwT, 0.0).astype(BF16), vb)


def _mixer_sample(p, la, state_gla, state_ret, batch, seq):
    rows = SAMPLE_ROWS
    n_el = rows // seq
    _, d_read, d_write, d_chunk = _ret_consts(seq)
    gammas = tuple(float(1.0 - 2.0 ** (-5.0 - h)) for h in range(RET_HEADS))
    cos, sin = _rope_tables(PAST_LEN + jnp.arange(seq, dtype=jnp.int32))
    cos, sin = jnp.tile(cos, (n_el, 1)), jnp.tile(sin, (n_el, 1))
    d_read = np.tile(d_read, (1, n_el, 1))
    d_write = np.tile(d_write, (1, n_el, 1))

    pcol = lambda width, blk: pl.BlockSpec((rows, width), lambda i: (i, blk))
    tok = lambda width: pl.BlockSpec((rows, width), lambda i: (i, 0))
    full2 = lambda shape: pl.BlockSpec(shape, lambda i: (0, 0))
    full3 = lambda shape: pl.BlockSpec(shape, lambda i: (0, 0, 0))
    sg_spec = pl.BlockSpec((n_el, GLA_HEADS, GLA_DK, GLA_DV), lambda i: (i, 0, 0, 0))
    sr_spec = pl.BlockSpec((n_el, RET_HEADS, RET_DK, RET_DV), lambda i: (i, 0, 0, 0))
    return pl.pallas_call(
        functools.partial(_mixer_sample_kernel, seq, gammas, d_chunk),
        grid=(batch * seq // rows,),
        in_specs=[pcol(RV, BLK_VR), pcol(RV, BLK_ZR), pcol(RK, BLK_QR), pcol(RK, BLK_KR),
                  pcol(GV, BLK_VG), pcol(GV, BLK_ZG), pcol(2 * GK, BLK_QKG), tok(GK),
                  full2((rows, RET_DK // 2)), full2((rows, RET_DK // 2)),
                  full3((RET_HEADS, rows, 1)), full3((RET_HEADS, rows, 1)),
                  sg_spec, sr_spec],
        out_specs=[tok(GV), tok(RV), sg_spec, sr_spec],
        out_shape=[jax.ShapeDtypeStruct((batch * seq, GV), F32),
                   jax.ShapeDtypeStruct((batch * seq, RV), F32),
                   jax.ShapeDtypeStruct(state_gla.shape, F32),
                   jax.ShapeDtypeStruct(state_ret.shape, F32)],
        compiler_params=pltpu.CompilerParams(dimension_semantics=("parallel",),
                                             vmem_limit_bytes=VMEM_LIMIT),
        name="mixer_sample",
    )(p, p, p, p, p, p, p, la, cos, sin, jnp.asarray(d_read), jnp.asarray(d_write), state_gla, state_ret)


def _out_kernel(og_ref, or_ref, mg_ref, mr_ref, x_ref, gate_ref, wbg_ref, wbr_ref, wo_ref, lng_ref, lnb_ref,
                y_ref):
    sub = min(OUT_SUB, y_ref.shape[0])
    blocks = [slice(r0, r0 + sub) for r0 in range(0, y_ref.shape[0], sub)]
    merged = []
    for rs in blocks:
        bg = _dot(og_ref[rs, :].astype(BF16), wbg_ref[...])
        br = _dot(or_ref[rs, :].astype(BF16), wbr_ref[...])
        merged.append((mg_ref[rs, :].astype(F32) * bg + mr_ref[rs, :].astype(F32) * br).astype(BF16))
    for rs, m in zip(blocks, merged):
        out = _dot(m, wo_ref[...])
        gate = gate_ref[...] if gate_ref.shape[0] == 1 else gate_ref[rs, :]
        r = DEEPNORM_ALPHA * x_ref[rs, :] + gate * out
        mu = jnp.mean(r, axis=-1, keepdims=True)
        c = r - mu
        var = jnp.mean(c * c, axis=-1, keepdims=True)
        y_ref[rs, :] = c * lax.rsqrt(var + LN_EPS) * lng_ref[...] + lnb_ref[...]


def _out_stage(og, orr, p, x2d, gate3, wbg, wbr, wo, ln_g, ln_b, *, tm, rows_per_mod):
    n = x2d.shape[0]
    if gate3.shape[1] != 1:
        gate_spec = pl.BlockSpec((None, tm, D_MODEL), lambda i: (0, i, 0))
    else:
        tiles_per_mod = rows_per_mod // tm
        gate_spec = pl.BlockSpec((None, 1, D_MODEL), lambda i: (i // tiles_per_mod, 0, 0))
    full = lambda shape: pl.BlockSpec(shape, lambda i: (0, 0), pipeline_mode=pl.Buffered(1))
    return pl.pallas_call(
        _out_kernel,
        grid=(n // tm,),
        in_specs=[pl.BlockSpec((tm, GV), lambda i: (i, 0)),
                  pl.BlockSpec((tm, RV), lambda i: (i, 0)),
                  pl.BlockSpec((tm, D_MODEL), lambda i: (i, BLK_MG)),
                  pl.BlockSpec((tm, D_MODEL), lambda i: (i, BLK_MR)),
                  pl.BlockSpec((tm, D_MODEL), lambda i: (i, 0)),
                  gate_spec,
                  full((GV, D_MODEL)), full((RV, D_MODEL)), full((D_MODEL, D_MODEL)),
                  full((1, D_MODEL)), full((1, D_MODEL))],
        out_specs=pl.BlockSpec((tm, D_MODEL), lambda i: (i, 0)),
        out_shape=jax.ShapeDtypeStruct((n, D_MODEL), F32),
        compiler_params=pltpu.CompilerParams(dimension_semantics=("parallel",),
                                             vmem_limit_bytes=VMEM_LIMIT),
        name="out_stage",
    )(og, orr, p, p, x2d, gate3, wbg, wbr, wo, ln_g, ln_b)


_W_SRC_BLK = (_O_VR // PROJ_TN, _O_VR // PROJ_TN + 1, _O_QR // PROJ_TN, _O_KR // PROJ_TN, _O_VG // PROJ_TN,
              _O_QG // PROJ_TN, _O_ZR // PROJ_TN, _O_ZR // PROJ_TN + 1, _O_ZG // PROJ_TN, _O_MG // PROJ_TN,
              _O_MR // PROJ_TN)
_LR_BLK = _O_LR // PROJ_TN


def _prep_w_kernel(src_ref, a_ref, b_ref, o_ref, lr_ref):
    blk = src_ref[pl.program_id(0)]
    keep = PROJ_TN - GLA_LOWRANK

    @pl.when(blk >= _LR_BLK)
    def _():
        o_ref[:keep, :] = a_ref[GLA_LOWRANK:, :].astype(BF16)
        o_ref[keep:, :] = b_ref[...].astype(BF16)

    @pl.when(blk < _LR_BLK)
    def _():
        o_ref[...] = a_ref[...].astype(BF16)

    @pl.when(blk == _LR_BLK)
    def _():
        lr_ref[:GLA_LOWRANK, :] = a_ref[:GLA_LOWRANK, :].astype(BF16)
        lr_ref[GLA_LOWRANK:, :] = jnp.zeros((LANES - GLA_LOWRANK, D_MODEL), BF16)


def _prep_w_in(wt):
    assert all(o % PROJ_TN in (0, GLA_LOWRANK) for o in (_O_VR, _O_QR, _O_KR, _O_VG, _O_QG, _O_ZR, _O_ZG,
                                                         _O_MG, _O_MR)) and _O_LR % PROJ_TN == 0
    src = jnp.asarray(_W_SRC_BLK, jnp.int32)
    per_blk = PROJ_TN // GLA_LOWRANK
    return pl.pallas_call(
        _prep_w_kernel,
        grid_spec=pltpu.PrefetchScalarGridSpec(
            num_scalar_prefetch=1,
            grid=(len(_W_SRC_BLK),),
            in_specs=[pl.BlockSpec((PROJ_TN, D_MODEL), lambda j, src: (src[j], 0)),
                      pl.BlockSpec((GLA_LOWRANK, D_MODEL), lambda j, src: ((src[j] + 1) * per_blk, 0))],
            out_specs=[pl.BlockSpec((PROJ_TN, D_MODEL), lambda j, src: (j, 0)),
                       pl.BlockSpec((LANES, D_MODEL), lambda j, src: (0, 0))]),
        out_shape=[jax.ShapeDtypeStruct((P_COLS, D_MODEL), BF16),
                   jax.ShapeDtypeStruct((LANES, D_MODEL), BF16)],
        compiler_params=pltpu.CompilerParams(dimension_semantics=("arbitrary",),
                                             vmem_limit_bytes=VMEM_LIMIT),
        name="prep_w",
    )(src, wt, wt)


def kernel(x_prompt, x_sample, state_gla, state_ret, c_prompt, c_sample, w_ada, b_ada, w_in, w_lr2, b_lr2,
           gla_norm_g, ret_norm_g, w_branch_gla, w_branch_ret, w_out, ln_g, ln_b):
    assert w_ada.shape[0] == 1, "one layer"
    bp, tp, _ = x_prompt.shape
    bs, ts, _ = x_sample.shape

    mod = _ada(jnp.concatenate([c_prompt, c_sample], axis=0), w_ada[0], b_ada)
    shift, scale, gate = mod[:, :D_MODEL], mod[:, D_MODEL:2 * D_MODEL], mod[:, 2 * D_MODEL:]
    per_group = lambda a: a[:bp][:, None, :]
    per_token = lambda a: jnp.repeat(a[bp:], ts, axis=0)[None]

    w_main, w_lr = _prep_w_in(w_in[0].T)
    w_lr2p = jnp.pad(w_lr2[0], ((0, LANES - GLA_LOWRANK), (0, 0))).astype(BF16)
    wbg = (gla_norm_g[0][:, None] * w_branch_gla[0]).astype(BF16)
    wbr = (ret_norm_g[0][:, None] * w_branch_ret[0]).astype(BF16)
    wo = w_out[0].astype(BF16)

    xp = x_prompt.reshape(bp * tp, D_MODEL)
    p_p, la_p = _proj(xp, per_group(scale), per_group(shift), w_main, w_lr, w_lr2p, b_lr2,
                      tm=2048, rows_per_mod=tp, p_dtype=BF16, cum_chunk=CHUNK)
    og_p, or_p, sg_p, sr_p = _mixer_prompt(p_p, la_p, bp, tp)
    y_p = _out_stage(og_p, or_p, p_p, xp, per_group(gate), wbg, wbr, wo, ln_g, ln_b, tm=1024, rows_per_mod=tp)

    xs = x_sample.reshape(bs * ts, D_MODEL)
    p_s, la_s = _proj(xs, per_token(scale), per_token(shift), w_main, w_lr, w_lr2p, b_lr2,
                      tm=bs * ts, rows_per_mod=ts, p_dtype=F32)
    og_s, or_s, sg_s, sr_s = _mixer_sample(p_s, la_s, state_gla[0], state_ret[0], bs, ts)
    y_s = _out_stage(og_s, or_s, p_s, xs, per_token(gate), wbg, wbr, wo, ln_g, ln_b,
                     tm=bs * ts, rows_per_mod=ts)

    return (y_p.reshape(bp, tp, D_MODEL), y_s.reshape(bs, ts, D_MODEL),
            sg_p[None], sr_p[None], sg_s[None], sr_s[None])
```

```python
import functools

import numpy as np
import jax
import jax.numpy as jnp
from jax import lax
from jax.experimental import pallas as pl
from jax.experimental.pallas import tpu as pltpu

F32 = jnp.float32
BF16 = jnp.bfloat16

D_MODEL = 1024
PAST_LEN = 16384
GLA_HEADS, GLA_DK, GLA_DV = 4, 128, 256
RET_HEADS, RET_DK, RET_DV = 4, 256, 512
GK, GV = GLA_HEADS * GLA_DK, GLA_HEADS * GLA_DV
RK, RV = RET_HEADS * RET_DK, RET_HEADS * RET_DV
GLA_LOWRANK = 16
GLA_TAU = 16.0
GLA_SUB = 16
ROPE_BASE = 10000.0
DEEPNORM_ALPHA = 2.0 ** 0.25
LN_EPS = 1e-5
HEAD_NORM_EPS = 1e-5

CHUNK = 128
SAMPLE_ROWS = 8
RIDER_STEPS = 4
LANES = 128

PROJ_TN = 1024
PROJ_SUB = 512
OUT_SUB = 512
P_COLS = 11 * PROJ_TN
BLK_VR, BLK_ZR = 0, 3
BLK_QR, BLK_KR, BLK_VG, BLK_QKG, BLK_ZG, BLK_MG, BLK_MR = 2, 3, 4, 5, 8, 9, 10
FIRST_SILU_BLK, FIRST_SIGMOID_BLK = 6, 9
_O_QG, _O_KG, _O_VG, _O_ZG, _O_LR = 0, GK, 2 * GK, 2 * GK + GV, 2 * GK + 2 * GV
_O_QR = _O_LR + GLA_LOWRANK
_O_KR = _O_QR + RK
_O_VR = _O_KR + RK
_O_ZR = _O_VR + RV
_O_MG = _O_ZR + RV
_O_MR = _O_MG + D_MODEL
_O_END = _O_MR + D_MODEL

VMEM_LIMIT = 56 * 1024 * 1024


def _dot(a, b):
    return jnp.dot(a, b, preferred_element_type=F32)


def _dot_nt(a, b):
    return lax.dot_general(a, b, (((1,), (1,)), ((), ())), preferred_element_type=F32)


def _log_sigmoid(x):
    return -(jnp.maximum(-x, 0.0) + jnp.log(1.0 + jnp.exp(-jnp.abs(x))))


def _sigmoid(x):
    return 0.5 * jnp.tanh(0.5 * x) + 0.5


def _silu(x):
    return x * _sigmoid(x)


def _ada_kernel(c_ref, w_ref, b_ref, o_ref):
    one = jnp.where(pl.program_id(0) == 1, 1.0, 0.0)
    o_ref[...] = _dot(c_ref[...].astype(BF16), w_ref[...].astype(BF16)) + (b_ref[...] + one)


def _ada(c, w_ada, b_ada):
    n = c.shape[0]
    return pl.pallas_call(
        _ada_kernel,
        grid=(3,),
        in_specs=[pl.BlockSpec((n, D_MODEL), lambda j: (0, 0)),
                  pl.BlockSpec((D_MODEL, D_MODEL), lambda j: (0, j)),
                  pl.BlockSpec((1, D_MODEL), lambda j: (0, j))],
        out_specs=pl.BlockSpec((n, D_MODEL), lambda j: (0, j)),
        out_shape=jax.ShapeDtypeStruct((n, 3 * D_MODEL), F32),
        compiler_params=pltpu.CompilerParams(dimension_semantics=("arbitrary",)),
        name="ada",
    )(c, w_ada, b_ada)


def _proj_kernel(cum_chunk, rider, n_rider_in, x_ref, scale1_ref, shift_ref, w_ref, wlr_ref, wlr2_ref, blr2_ref,
                 cos_ref, sin_ref, *rest):
    rider_in, (p_ref, la_ref), rider_out, h_ref = (rest[:n_rider_in], rest[n_rider_in:n_rider_in + 2],
                                                   rest[n_rider_in + 2:-1], rest[-1])
    j = pl.program_id(1)

    if rider is not None:
        pl.when(((j & 1) == 1) & (j < 2 * RIDER_STEPS))(lambda: rider(*rider_in, *rider_out))

    @pl.when(j == 0)
    def _():
        h = (x_ref[...] * scale1_ref[...] + shift_ref[...]).astype(BF16)
        h_ref[...] = h
        lr = _dot_nt(h, wlr_ref[...])
        pre = _dot(lr.astype(BF16), wlr2_ref[...]) + blr2_ref[...]
        la = _log_sigmoid(pre) * (1.0 / GLA_TAU)
        if cum_chunk:
            for r in range(0, la.shape[0], cum_chunk):
                la_ref[r:r + cum_chunk, :] = _cumsum_rows_mxu(la[r:r + cum_chunk, :])
        else:
            la_ref[...] = la

    def emit(epilogue):
        sub = min(PROJ_SUB, p_ref.shape[0])
        for r in range(0, p_ref.shape[0], sub):
            rs = slice(r, r + sub)
            p_ref[rs, :] = epilogue(_dot_nt(h_ref[rs, :], w_ref[...]), rs).astype(p_ref.dtype)

    def rotate(scale):
        return lambda a, rs: _rotary_heads(a, cos_ref[rs, :], sin_ref[rs, :], RET_HEADS, scale)

    plain = (j < FIRST_SILU_BLK) & (j != BLK_QR) & (j != BLK_KR)
    pl.when(plain)(lambda: emit(lambda a, rs: a))
    pl.when(j == BLK_QR)(lambda: emit(rotate(1.0)))
    pl.when(j == BLK_KR)(lambda: emit(rotate(RET_DK ** -0.5)))
    pl.when((j >= FIRST_SILU_BLK) & (j < FIRST_SIGMOID_BLK))(lambda: emit(lambda a, rs: _silu(a)))
    pl.when(j >= FIRST_SIGMOID_BLK)(lambda: emit(lambda a, rs: _sigmoid(a)))


def _rider_block(i, j):
    return i * RIDER_STEPS + jnp.clip((j - 1) // 2, 0, RIDER_STEPS - 1)


def _proj(x2d, scale1_3, shift3, w_main, w_lr, w_lr2, b_lr2, cos, sin, *, tm, rows_per_mod, p_dtype, cum_chunk=0,
          rider=None):
    assert cum_chunk == 0 or (tm % cum_chunk == 0 and rows_per_mod % cum_chunk == 0)
    n = x2d.shape[0]
    if rider is None:
        rider = dict(kernel=None, operands=(), in_specs=[], out_specs=[], out_shape=[], n_blocks=0)
    else:
        assert rider["n_blocks"] == (n // tm) * RIDER_STEPS and P_COLS // PROJ_TN >= 2 * RIDER_STEPS
    rope_tiles = cos.shape[0] // tm
    rope_spec = pl.BlockSpec((tm, RET_DK // 2), lambda i, j: (i % rope_tiles, 0))
    per_row = shift3.shape[1] != 1
    if per_row:
        mod_spec = pl.BlockSpec((None, tm, D_MODEL), lambda i, j: (0, i, 0))
    else:
        tiles_per_mod = rows_per_mod // tm
        mod_spec = pl.BlockSpec((None, 1, D_MODEL), lambda i, j: (i // tiles_per_mod, 0, 0))
    tn = PROJ_TN
    return pl.pallas_call(
        functools.partial(_proj_kernel, cum_chunk, rider["kernel"], len(rider["operands"])),
        grid=(n // tm, P_COLS // tn),
        in_specs=[pl.BlockSpec((tm, D_MODEL), lambda i, j: (i, 0)),
                  mod_spec, mod_spec,
                  pl.BlockSpec((tn, D_MODEL), lambda i, j: (j, 0)),
                  pl.BlockSpec((LANES, D_MODEL), lambda i, j: (0, 0)),
                  pl.BlockSpec((LANES, GK), lambda i, j: (0, 0)),
                  pl.BlockSpec((1, GK), lambda i, j: (0, 0)),
                  rope_spec, rope_spec] + rider["in_specs"],
        out_specs=[pl.BlockSpec((tm, tn), lambda i, j: (i, j)),
                   pl.BlockSpec((tm, GK), lambda i, j: (i, 0))] + rider["out_specs"],
        out_shape=[jax.ShapeDtypeStruct((n, P_COLS), p_dtype),
                   jax.ShapeDtypeStruct((n, GK), F32)] + rider["out_shape"],
        scratch_shapes=[pltpu.VMEM((tm, D_MODEL), BF16)],
        compiler_params=pltpu.CompilerParams(dimension_semantics=("parallel", "arbitrary"),
                                             vmem_limit_bytes=VMEM_LIMIT),
        name="proj",
    )(x2d, scale1_3, shift3, w_main, w_lr, w_lr2, b_lr2, cos, sin, *rider["operands"])


def _rms_gate(o, sz):
    o = o * lax.rsqrt(jnp.mean(o * o, axis=-1, keepdims=True) + HEAD_NORM_EPS)
    return o.astype(sz.dtype) * sz


def _group_gate(o, sz):
    mu = jnp.mean(o, axis=-1, keepdims=True)
    c = o - mu
    var = jnp.mean(c * c, axis=-1, keepdims=True)
    return (c * lax.rsqrt(var + HEAD_NORM_EPS)).astype(sz.dtype) * sz


def _cumsum_rows_mxu(x):
    n = x.shape[0]
    row = lax.broadcasted_iota(jnp.int32, (n, n), 0)
    col = lax.broadcasted_iota(jnp.int32, (n, n), 1)
    tri = jnp.where(col <= row, 1.0, 0.0).astype(BF16)
    hi = x.astype(BF16)
    rest = x - hi.astype(F32)
    mid = rest.astype(BF16)
    lo = (rest - mid.astype(F32)).astype(BF16)
    return _dot(tri, hi) + _dot(tri, mid) + _dot(tri, lo)


def _block_edge(b, w, prev):
    n, m = b.shape
    pieces = []
    for i in range(n // w):
        r = i * w - 1 if prev else i * w + w - 1
        if r < 0:
            pieces.append(jnp.zeros((w, m), F32))
        else:
            pieces.append(jnp.broadcast_to(b[r:r + 1, :], (w, m)))
    return jnp.concatenate(pieces, axis=0)


def _gla_levels(c):
    w = GLA_SUB
    while w < c:
        yield w
        w *= 2


def _gla_operands(q, k, b):
    c = q.shape[0]
    btot = b[c - 1:c, :]
    rel = b - _block_edge(b, GLA_SUB, True)
    ops = {"q_in": (q * jnp.exp(b)).astype(BF16),
           "k_out": (k * jnp.exp(btot - b)).astype(BF16),
           "qd": (q * jnp.exp(rel)).astype(BF16),
           "kd": (k * jnp.exp(-rel)).astype(BF16)}
    for w in _gla_levels(c):
        if w > GLA_SUB:
            ops["q%d" % w] = (q * jnp.exp(b - _block_edge(b, w, True))).astype(BF16)
        ops["k%d" % w] = (k * jnp.exp(_block_edge(b, w, False) - b)).astype(BF16)
    return ops


def _gla_masks(c):
    row = lax.broadcasted_iota(jnp.int32, (c, c), 0)
    col = lax.broadcasted_iota(jnp.int32, (c, c), 1)
    sh = GLA_SUB.bit_length() - 1
    masks = {"diag": ((row >> sh) == (col >> sh)) & (col <= row)}
    for w in _gla_levels(c):
        sh = w.bit_length() - 1
        rb, cb = row >> sh, col >> sh
        masks[w] = ((rb & 1) == 1) & (cb == rb - 1)
    return masks


def _gla_scores(ops, hs, masks, c):
    a = jnp.where(masks["diag"], _dot_nt(ops["qd"][:, hs], ops["kd"][:, hs]), 0.0)
    for w in _gla_levels(c):
        qw = ops["qd"] if w == GLA_SUB else ops["q%d" % w]
        a = jnp.where(masks[w], _dot_nt(qw[:, hs], ops["k%d" % w][:, hs]), a)
    return a.astype(BF16)


def _rotary_heads(x, cos, sin, heads, scale=1.0):
    half = cos.shape[1]
    if scale != 1.0:
        cos, sin = cos * scale, sin * scale
    cos_t = jnp.concatenate([cos] * (2 * heads), axis=1)
    sin_t = jnp.concatenate([-sin, sin] * heads, axis=1)
    partner = [i + 1 - 2 * (i % 2) for i in range(2 * heads)]
    swapped = jnp.concatenate([x[:, j * half:(j + 1) * half] for j in partner], axis=1)
    return x * cos_t + swapped * sin_t


def _mixer_prompt_kernel(d_chunk, vr_ref, zr_ref, qr_ref, kr_ref, vg_ref, zg_ref, qkg_ref, b_ref,
                         dintra_ref, dread_ref, dwrite_ref,
                         og_ref, or_ref, sg_ref, sr_ref):
    @pl.when(pl.program_id(1) == 0)
    def _():
        sg_ref[...] = jnp.zeros_like(sg_ref)
        sr_ref[...] = jnp.zeros_like(sr_ref)

    c = CHUNK
    g_sl = [slice(h * GLA_DK, (h + 1) * GLA_DK) for h in range(GLA_HEADS)]
    gv_sl = [slice(h * GLA_DV, (h + 1) * GLA_DV) for h in range(GLA_HEADS)]
    r_sl = [slice(h * RET_DK, (h + 1) * RET_DK) for h in range(RET_HEADS)]
    rv_sl = [slice(h * RET_DV, (h + 1) * RET_DV) for h in range(RET_HEADS)]

    b = b_ref[...]
    qb = qr_ref[...]
    kb = kr_ref[...]
    q_read = qb * dread_ref[...].astype(BF16)
    k_write = kb * dwrite_ref[...].astype(BF16)
    g_rows = jnp.exp(b[c - 1:c, :])
    g_cols = [jnp.broadcast_to(g_rows[:, s], (GLA_DK, GLA_DK)).T[:, :1] for s in g_sl]
    gops = _gla_operands(qkg_ref[:, :GK].astype(F32) * (GLA_DK ** -0.5), qkg_ref[:, GK:].astype(F32), b)

    masks = _gla_masks(c)
    a_g = [_gla_scores(gops, s, masks, c) for s in g_sl]
    a_r = [(_dot_nt(qb[:, s], kb[:, s]) * dintra_ref[h]).astype(BF16) for h, s in enumerate(r_sl)]

    for h in range(GLA_HEADS):
        o = _dot(gops["q_in"][:, g_sl[h]], sg_ref[h].astype(BF16)) + _dot(a_g[h], vg_ref[:, gv_sl[h]])
        og_ref[:, gv_sl[h]] = _rms_gate(o, zg_ref[:, gv_sl[h]])
    for h in range(RET_HEADS):
        o = _dot(a_r[h], vr_ref[:, rv_sl[h]]) + _dot(q_read[:, r_sl[h]], sr_ref[h].astype(BF16))
        or_ref[:, rv_sl[h]] = _group_gate(o, zr_ref[:, rv_sl[h]])

    for h in range(GLA_HEADS):
        sg_ref[h] = g_cols[h] * sg_ref[h] + _dot(gops["k_out"][:, g_sl[h]].T, vg_ref[:, gv_sl[h]])
    for h in range(RET_HEADS):
        sr_ref[h] = d_chunk[h] * sr_ref[h] + _dot(k_write[:, r_sl[h]].T, vr_ref[:, rv_sl[h]])


def _ret_consts(length):
    h = np.arange(RET_HEADS, dtype=np.float64)
    log_gamma = np.log1p(-np.exp2(-5.0 - h))
    idx = np.arange(length, dtype=np.float64)
    diff = idx[:, None] - idx[None, :]
    d_intra = np.where(diff[None] >= 0, np.exp(np.maximum(diff, 0.0)[None] * log_gamma[:, None, None]), 0.0)
    d_read = np.exp((idx + 1.0)[None, :] * log_gamma[:, None])
    d_write = np.exp((length - 1.0 - idx)[None, :] * log_gamma[:, None])
    d_chunk = np.exp(length * log_gamma)
    return (d_intra.astype(np.float32), d_read.astype(np.float32)[..., None],
            d_write.astype(np.float32)[..., None], tuple(float(np.float32(x)) for x in d_chunk))


def _rope_tables(pos):
    half = RET_DK // 2
    inv_freq = ROPE_BASE ** (-jnp.arange(half, dtype=F32) / half)
    ang = pos.astype(F32)[:, None] * inv_freq[None, :]
    return jnp.cos(ang), jnp.sin(ang)


def _mixer_prompt(p, la, batch, seq):
    assert p.dtype == BF16
    nchunk = seq // CHUNK
    d_intra, d_read, d_write, d_chunk = _ret_consts(CHUNK)
    d_read = np.repeat(d_read[..., 0].T, RET_DK, axis=1)
    d_write = np.repeat(d_write[..., 0].T, RET_DK, axis=1)

    def pcol(width, blk):
        return pl.BlockSpec((CHUNK, width), lambda b, t: (b * nchunk + t, blk))

    const3 = lambda shape: pl.BlockSpec(shape, lambda b, t: (0, 0, 0))
    tok = lambda width: pl.BlockSpec((CHUNK, width), lambda b, t: (b * nchunk + t, 0))
    return pl.pallas_call(
        functools.partial(_mixer_prompt_kernel, d_chunk),
        grid=(batch, nchunk),
        in_specs=[pcol(RV, BLK_VR), pcol(RV, BLK_ZR), pcol(RK, BLK_QR), pcol(RK, BLK_KR),
                  pcol(GV, BLK_VG), pcol(GV, BLK_ZG), pcol(2 * GK, BLK_QKG), tok(GK),
                  const3((RET_HEADS, CHUNK, CHUNK)),
                  pl.BlockSpec((CHUNK, RK), lambda b, t: (0, 0)),
                  pl.BlockSpec((CHUNK, RK), lambda b, t: (0, 0))],
        out_specs=[tok(GV), tok(RV),
                   pl.BlockSpec((None, GLA_HEADS, GLA_DK, GLA_DV), lambda b, t: (b, 0, 0, 0)),
                   pl.BlockSpec((None, RET_HEADS, RET_DK, RET_DV), lambda b, t: (b, 0, 0, 0))],
        out_shape=[jax.ShapeDtypeStruct((batch * seq, GV), BF16),
                   jax.ShapeDtypeStruct((batch * seq, RV), BF16),
                   jax.ShapeDtypeStruct((batch, GLA_HEADS, GLA_DK, GLA_DV), F32),
                   jax.ShapeDtypeStruct((batch, RET_HEADS, RET_DK, RET_DV), F32)],
        compiler_params=pltpu.CompilerParams(dimension_semantics=("parallel", "arbitrary"),
                                             vmem_limit_bytes=VMEM_LIMIT),
        name="mixer_prompt",
    )(p, p, p, p, p, p, p, la, jnp.asarray(d_intra), jnp.asarray(d_read), jnp.asarray(d_write))


def _pad_rows(x, rows):
    return jnp.concatenate([x, jnp.zeros((rows - x.shape[0], x.shape[1]), x.dtype)], axis=0)


def _mixer_sample_kernel(seq, gammas, d_chunk, vr_ref, zr_ref, qr_ref, kr_ref, vg_ref, zg_ref, qkg_ref,
                         la_ref, dread_ref, dwrite_ref,
                         sg_in, sr_in, og_ref, or_ref, sg_ref, sr_ref):
    rows = SAMPLE_ROWS
    n_el = rows // seq
    row1 = lax.broadcasted_iota(jnp.int32, (rows, 1), 0)
    pos1 = row1 % seq
    tcol = lax.broadcasted_iota(jnp.int32, (LANES, LANES), 1)

    def seg_cumsum(x):
        s = 1
        while s < seq:
            x = x + jnp.where(pos1 >= s, pltpu.roll(x, s, 0), 0.0)
            s *= 2
        return x

    def intra(q, k, v, gamma):
        out = jnp.zeros_like(v)
        for d in range(seq):
            kd = k if d == 0 else pltpu.roll(k, d, 0)
            vd = v if d == 0 else pltpu.roll(v, d, 0)
            a = jnp.sum(q * kd, axis=1, keepdims=True) * (gamma ** d)
            out = out + jnp.where(pos1 >= d, a, 0.0) * vd
        return out

    def inter(q, states):
        qb = _pad_rows(q, 2 * rows).astype(BF16)
        out = None
        for e in range(n_el):
            oe = _dot(qb, states[e].astype(BF16))[:rows]
            out = oe if out is None else jnp.where(row1 >= e * seq, oe, out)
        return out

    for h in range(GLA_HEADS):
        kq = slice(h * GLA_DK, (h + 1) * GLA_DK)
        kk = slice(GK + h * GLA_DK, GK + (h + 1) * GLA_DK)
        vv = slice(h * GLA_DV, (h + 1) * GLA_DV)
        q = qkg_ref[:, kq] * (GLA_DK ** -0.5)
        k = qkg_ref[:, kk]
        v = vg_ref[:, vv]
        la = la_ref[:, kq]
        b = seg_cumsum(la)
        q_dec = q * jnp.exp(b)
        k_dec = k * jnp.exp(-b)
        states = [sg_in[e, h] for e in range(n_el)]
        o = inter(q_dec, states) + intra(q_dec, k_dec, v, 1.0)
        og_ref[:, vv] = _rms_gate(o, zg_ref[:, vv])
        b_last = jnp.zeros_like(b)
        for e in range(n_el):
            r = e * seq + seq - 1
            b_last = jnp.where((row1 >= e * seq) & (row1 < (e + 1) * seq),
                               jnp.broadcast_to(b[r:r + 1, :], b.shape), b_last)
        k_out = k * jnp.exp(b_last - b)
        k_outT = _pad_rows(k_out, LANES).T
        laT = _pad_rows(la, LANES).T
        vb = _pad_rows(v, LANES).astype(BF16)
        tl = tcol[:GLA_DK]
        for e in range(n_el):
            sel = (tl >= e * seq) & (tl < (e + 1) * seq)
            g_col = jnp.exp(jnp.sum(jnp.where(sel, laT, 0.0), axis=1, keepdims=True))
            sg_ref[e, h] = g_col * states[e] + _dot(jnp.where(sel, k_outT, 0.0).astype(BF16), vb)

    for h in range(RET_HEADS):
        kq = slice(h * RET_DK, (h + 1) * RET_DK)
        vv = slice(h * RET_DV, (h + 1) * RET_DV)
        q = qr_ref[:, kq]
        k = kr_ref[:, kq]
        v = vr_ref[:, vv]
        states = [sr_in[e, h] for e in range(n_el)]
        o = inter(q, states) * dread_ref[h] + intra(q, k, v, gammas[h])
        or_ref[:, vv] = _group_gate(o, zr_ref[:, vv])
        kwT = _pad_rows(k * dwrite_ref[h], LANES).T
        vb = _pad_rows(v, LANES).astype(BF16)
        tl = jnp.concatenate([tcol, tcol], axis=0)
        for e in range(n_el):
            sel = (tl >= e * seq) & (tl < (e + 1) * seq)
            sr_ref[e, h] = d_chunk[h] * states[e] + _dot(jnp.where(sel, kwT, 0.0).astype(BF16), vb)


def _decode_rider(p, la, state_gla, state_ret, batch, seq):
    rows = SAMPLE_ROWS
    n_el = rows // seq
    _, d_read, d_write, d_chunk = _ret_consts(seq)
    gammas = tuple(float(1.0 - 2.0 ** (-5.0 - h)) for h in range(RET_HEADS))
    d_read = np.tile(d_read, (1, n_el, 1))
    d_write = np.tile(d_write, (1, n_el, 1))

    blk = _rider_block
    pcol = lambda width, c: pl.BlockSpec((rows, width), lambda i, j: (blk(i, j), c))
    tok = lambda width: pl.BlockSpec((rows, width), lambda i, j: (blk(i, j), 0))
    full3 = lambda shape: pl.BlockSpec(shape, lambda i, j: (0, 0, 0))
    sg_spec = pl.BlockSpec((n_el, GLA_HEADS, GLA_DK, GLA_DV), lambda i, j: (blk(i, j), 0, 0, 0))
    sr_spec = pl.BlockSpec((n_el, RET_HEADS, RET_DK, RET_DV), lambda i, j: (blk(i, j), 0, 0, 0))
    return dict(
        kernel=functools.partial(_mixer_sample_kernel, seq, gammas, d_chunk),
        operands=(p, p, p, p, p, p, p, la, jnp.asarray(d_read), jnp.asarray(d_write), state_gla, state_ret),
        in_specs=[pcol(RV, BLK_VR), pcol(RV, BLK_ZR), pcol(RK, BLK_QR), pcol(RK, BLK_KR),
                  pcol(GV, BLK_VG), pcol(GV, BLK_ZG), pcol(2 * GK, BLK_QKG), tok(GK),
                  full3((RET_HEADS, rows, 1)), full3((RET_HEADS, rows, 1)),
                  sg_spec, sr_spec],
        out_specs=[tok(GV), tok(RV), sg_spec, sr_spec],
        out_shape=[jax.ShapeDtypeStruct((batch * seq, GV), F32),
                   jax.ShapeDtypeStruct((batch * seq, RV), F32),
                   jax.ShapeDtypeStruct(state_gla.shape, F32),
                   jax.ShapeDtypeStruct(state_ret.shape, F32)],
        n_blocks=batch * seq // rows)


def _out_kernel(og_ref, or_ref, mg_ref, mr_ref, x_ref, gate_ref, wbg_ref, wbr_ref, wo_ref, lng_ref, lnb_ref,
                y_ref):
    sub = min(OUT_SUB, y_ref.shape[0])
    blocks = [slice(r0, r0 + sub) for r0 in range(0, y_ref.shape[0], sub)]
    merged = []
    for rs in blocks:
        bg = _dot(og_ref[rs, :].astype(BF16), wbg_ref[...])
        br = _dot(or_ref[rs, :].astype(BF16), wbr_ref[...])
        merged.append((mg_ref[rs, :].astype(F32) * bg + mr_ref[rs, :].astype(F32) * br).astype(BF16))
    for rs, m in zip(blocks, merged):
        out = _dot(m, wo_ref[...])
        gate = gate_ref[...] if gate_ref.shape[0] == 1 else gate_ref[rs, :]
        r = DEEPNORM_ALPHA * x_ref[rs, :] + gate * out
        mu = jnp.mean(r, axis=-1, keepdims=True)
        c = r - mu
        var = jnp.mean(c * c, axis=-1, keepdims=True)
        y_ref[rs, :] = c * lax.rsqrt(var + LN_EPS) * lng_ref[...] + lnb_ref[...]


def _out_stage(og, orr, p, x2d, gate3, wbg, wbr, wo, ln_g, ln_b, *, tm, rows_per_mod):
    n = x2d.shape[0]
    if gate3.shape[1] != 1:
        gate_spec = pl.BlockSpec((None, tm, D_MODEL), lambda i: (0, i, 0))
    else:
        tiles_per_mod = rows_per_mod // tm
        gate_spec = pl.BlockSpec((None, 1, D_MODEL), lambda i: (i // tiles_per_mod, 0, 0))
    full = lambda shape: pl.BlockSpec(shape, lambda i: (0, 0), pipeline_mode=pl.Buffered(1))
    return pl.pallas_call(
        _out_kernel,
        grid=(n // tm,),
        in_specs=[pl.BlockSpec((tm, GV), lambda i: (i, 0)),
                  pl.BlockSpec((tm, RV), lambda i: (i, 0)),
                  pl.BlockSpec((tm, D_MODEL), lambda i: (i, BLK_MG)),
                  pl.BlockSpec((tm, D_MODEL), lambda i: (i, BLK_MR)),
                  pl.BlockSpec((tm, D_MODEL), lambda i: (i, 0)),
                  gate_spec,
                  full((GV, D_MODEL)), full((RV, D_MODEL)), full((D_MODEL, D_MODEL)),
                  full((1, D_MODEL)), full((1, D_MODEL))],
        out_specs=pl.BlockSpec((tm, D_MODEL), lambda i: (i, 0)),
        out_shape=jax.ShapeDtypeStruct((n, D_MODEL), F32),
        compiler_params=pltpu.CompilerParams(dimension_semantics=("parallel",),
                                             vmem_limit_bytes=VMEM_LIMIT),
        name="out_stage",
    )(og, orr, p, p, x2d, gate3, wbg, wbr, wo, ln_g, ln_b)


_W_SRC_BLK = (_O_VR // PROJ_TN, _O_VR // PROJ_TN + 1, _O_QR // PROJ_TN, _O_KR // PROJ_TN, _O_VG // PROJ_TN,
              _O_QG // PROJ_TN, _O_ZR // PROJ_TN, _O_ZR // PROJ_TN + 1, _O_ZG // PROJ_TN, _O_MG // PROJ_TN,
              _O_MR // PROJ_TN)
_LR_BLK = _O_LR // PROJ_TN


def _prep_w_kernel(src_ref, a_ref, b_ref, o_ref, lr_ref):
    blk = src_ref[pl.program_id(0)]
    keep = PROJ_TN - GLA_LOWRANK

    @pl.when(blk >= _LR_BLK)
    def _():
        o_ref[:keep, :] = a_ref[GLA_LOWRANK:, :].astype(BF16)
        o_ref[keep:, :] = b_ref[...].astype(BF16)

    @pl.when(blk < _LR_BLK)
    def _():
        o_ref[...] = a_ref[...].astype(BF16)

    @pl.when(blk == _LR_BLK)
    def _():
        lr_ref[:GLA_LOWRANK, :] = a_ref[:GLA_LOWRANK, :].astype(BF16)
        lr_ref[GLA_LOWRANK:, :] = jnp.zeros((LANES - GLA_LOWRANK, D_MODEL), BF16)


def _prep_w_in(wt):
    assert all(o % PROJ_TN in (0, GLA_LOWRANK) for o in (_O_VR, _O_QR, _O_KR, _O_VG, _O_QG, _O_ZR, _O_ZG,
                                                         _O_MG, _O_MR)) and _O_LR % PROJ_TN == 0
    src = jnp.asarray(_W_SRC_BLK, jnp.int32)
    per_blk = PROJ_TN // GLA_LOWRANK
    return pl.pallas_call(
        _prep_w_kernel,
        grid_spec=pltpu.PrefetchScalarGridSpec(
            num_scalar_prefetch=1,
            grid=(len(_W_SRC_BLK),),
            in_specs=[pl.BlockSpec((PROJ_TN, D_MODEL), lambda j, src: (src[j], 0)),
                      pl.BlockSpec((GLA_LOWRANK, D_MODEL), lambda j, src: ((src[j] + 1) * per_blk, 0))],
            out_specs=[pl.BlockSpec((PROJ_TN, D_MODEL), lambda j, src: (j, 0)),
                       pl.BlockSpec((LANES, D_MODEL), lambda j, src: (0, 0))]),
        out_shape=[jax.ShapeDtypeStruct((P_COLS, D_MODEL), BF16),
                   jax.ShapeDtypeStruct((LANES, D_MODEL), BF16)],
        compiler_params=pltpu.CompilerParams(dimension_semantics=("arbitrary",),
                                             vmem_limit_bytes=VMEM_LIMIT),
        name="prep_w",
    )(src, wt, wt)


def kernel(x_prompt, x_sample, state_gla, state_ret, c_prompt, c_sample, w_ada, b_ada, w_in, w_lr2, b_lr2,
           gla_norm_g, ret_norm_g, w_branch_gla, w_branch_ret, w_out, ln_g, ln_b):
    assert w_ada.shape[0] == 1, "one layer"
    bp, tp, _ = x_prompt.shape
    bs, ts, _ = x_sample.shape

    mod = _ada(jnp.concatenate([c_prompt, c_sample], axis=0), w_ada[0], b_ada)
    shift, scale1, gate = mod[:, :D_MODEL], mod[:, D_MODEL:2 * D_MODEL], mod[:, 2 * D_MODEL:]
    per_group = lambda a: a[:bp][:, None, :]
    per_token = lambda a: jnp.repeat(a[bp:], ts, axis=0)[None]

    w_main, w_lr = _prep_w_in(w_in[0].T)
    w_lr2p = jnp.pad(w_lr2[0], ((0, LANES - GLA_LOWRANK), (0, 0))).astype(BF16)
    wbg = (gla_norm_g[0][:, None] * w_branch_gla[0]).astype(BF16)
    wbr = (ret_norm_g[0][:, None] * w_branch_ret[0]).astype(BF16)
    wo = w_out[0].astype(BF16)

    xs = x_sample.reshape(bs * ts, D_MODEL)
    cos_s, sin_s = _rope_tables(PAST_LEN + jnp.arange(bs * ts, dtype=jnp.int32) % ts)
    p_s, la_s = _proj(xs, per_token(scale1), per_token(shift), w_main, w_lr, w_lr2p, b_lr2, cos_s, sin_s,
                      tm=bs * ts, rows_per_mod=ts, p_dtype=F32)

    xp = x_prompt.reshape(bp * tp, D_MODEL)
    cos_p, sin_p = _rope_tables(jnp.arange(tp, dtype=jnp.int32))
    p_p, la_p, og_s, or_s, sg_s, sr_s = _proj(
        xp, per_group(scale1), per_group(shift), w_main, w_lr, w_lr2p, b_lr2, cos_p, sin_p,
        tm=1024, rows_per_mod=tp, p_dtype=BF16, cum_chunk=CHUNK,
        rider=_decode_rider(p_s, la_s, state_gla[0], state_ret[0], bs, ts))

    og_p, or_p, sg_p, sr_p = _mixer_prompt(p_p, la_p, bp, tp)
    y_p = _out_stage(og_p, or_p, p_p, xp, per_group(gate), wbg, wbr, wo, ln_g, ln_b, tm=1024, rows_per_mod=tp)
    y_s = _out_stage(og_s, or_s, p_s, xs, per_token(gate), wbg, wbr, wo, ln_g, ln_b,
                     tm=bs * ts, rows_per_mod=ts)

    return (y_p.reshape(bp, tp, D_MODEL), y_s.reshape(bs, ts, D_MODEL),
            sg_p[None], sr_p[None], sg_s[None], sr_s[None])
```

```python
import functools

import numpy as np
import jax
import jax.numpy as jnp
from jax import lax
from jax.experimental import pallas as pl
from jax.experimental.pallas import tpu as pltpu

F32 = jnp.float32
BF16 = jnp.bfloat16

D_MODEL = 1024
PAST_LEN = 16384
GLA_HEADS, GLA_DK, GLA_DV = 4, 128, 256
RET_HEADS, RET_DK, RET_DV = 4, 256, 512
GK, GV = GLA_HEADS * GLA_DK, GLA_HEADS * GLA_DV
RK, RV = RET_HEADS * RET_DK, RET_HEADS * RET_DV
GLA_LOWRANK = 16
GLA_TAU = 16.0
GLA_SUB = 16
ROPE_BASE = 10000.0
DEEPNORM_ALPHA = 2.0 ** 0.25
LN_EPS = 1e-5
HEAD_NORM_EPS = 1e-5

CHUNK = 128
SAMPLE_ROWS = 16
LANES = 128
LR_PAD = 256

PROJ_TN = 1024
PROJ_SUB = 1024
OUT_SUB = 256
P_COLS = 11 * PROJ_TN
BLK_VR, BLK_ZR = 0, 3
BLK_QR, BLK_KR, BLK_VG, BLK_QKG, BLK_ZG, BLK_MG, BLK_MR = 2, 3, 4, 5, 8, 9, 10
FIRST_SILU_BLK, FIRST_SIGMOID_BLK = 6, 9
_O_QG, _O_KG, _O_VG, _O_ZG, _O_LR = 0, GK, 2 * GK, 2 * GK + GV, 2 * GK + 2 * GV
_O_QR = _O_LR + GLA_LOWRANK
_O_KR = _O_QR + RK
_O_VR = _O_KR + RK
_O_ZR = _O_VR + RV
_O_MG = _O_ZR + RV
_O_MR = _O_MG + D_MODEL
_O_END = _O_MR + D_MODEL

VMEM_LIMIT = 56 * 1024 * 1024


def _dot(a, b):
    return jnp.dot(a, b, preferred_element_type=F32)


def _dot_nt(a, b):
    return lax.dot_general(a, b, (((1,), (1,)), ((), ())), preferred_element_type=F32)


def _log_sigmoid(x):
    return -(jnp.maximum(-x, 0.0) + jnp.log(1.0 + jnp.exp(-jnp.abs(x))))


def _sigmoid(x):
    return 0.5 * jnp.tanh(0.5 * x) + 0.5


def _silu(x):
    return x * _sigmoid(x)


def _ada_kernel(c_ref, w_ref, b_ref, o_ref):
    one = jnp.where(pl.program_id(0) == 1, 1.0, 0.0)
    o_ref[...] = _dot(c_ref[...].astype(BF16), w_ref[...].astype(BF16)) + (b_ref[...] + one)


def _ada(c, w_ada, b_ada):
    n = c.shape[0]
    return pl.pallas_call(
        _ada_kernel,
        grid=(3,),
        in_specs=[pl.BlockSpec((n, D_MODEL), lambda j: (0, 0)),
                  pl.BlockSpec((D_MODEL, D_MODEL), lambda j: (0, j)),
                  pl.BlockSpec((1, D_MODEL), lambda j: (0, j))],
        out_specs=pl.BlockSpec((n, D_MODEL), lambda j: (0, j)),
        out_shape=jax.ShapeDtypeStruct((n, 3 * D_MODEL), F32),
        compiler_params=pltpu.CompilerParams(dimension_semantics=("arbitrary",)),
        name="ada",
    )(c, w_ada, b_ada)


def _proj_kernel(cum_chunk, x_ref, scale1_ref, shift_ref, w_ref, wlr_ref, wlr2_ref, blr2_ref, cos_ref, sin_ref,
                 p_ref, la_ref, h_ref):
    j = pl.program_id(1)

    @pl.when(j == 0)
    def _():
        h = (x_ref[...] * scale1_ref[...] + shift_ref[...]).astype(BF16)
        h_ref[...] = h
        sub = min(PROJ_SUB, h.shape[0])
        lr = jnp.concatenate([_dot_nt(h[r:r + sub, :], wlr_ref[...]) for r in range(0, h.shape[0], sub)], axis=0)
        pre = _dot(lr.astype(BF16), wlr2_ref[...]) + blr2_ref[...]
        la = _log_sigmoid(pre) * (1.0 / GLA_TAU)
        if cum_chunk:
            for r in range(0, la.shape[0], cum_chunk):
                la_ref[r:r + cum_chunk, :] = _cumsum_rows_mxu(la[r:r + cum_chunk, :])
        else:
            la_ref[...] = la

    def emit(epilogue):
        sub = min(PROJ_SUB, p_ref.shape[0])
        for r in range(0, p_ref.shape[0], sub):
            rs = slice(r, r + sub)
            p_ref[rs, :] = epilogue(_dot_nt(h_ref[rs, :], w_ref[...]), rs).astype(p_ref.dtype)

    def rotate(scale):
        return lambda a, rs: _rotary_heads(a, cos_ref[rs, :], sin_ref[rs, :], RET_HEADS, scale)

    plain = (j < FIRST_SILU_BLK) & (j != BLK_QR) & (j != BLK_KR)
    pl.when(plain)(lambda: emit(lambda a, rs: a))
    pl.when(j == BLK_QR)(lambda: emit(rotate(1.0)))
    pl.when(j == BLK_KR)(lambda: emit(rotate(RET_DK ** -0.5)))
    pl.when((j >= FIRST_SILU_BLK) & (j < FIRST_SIGMOID_BLK))(lambda: emit(lambda a, rs: _silu(a)))
    pl.when(j >= FIRST_SIGMOID_BLK)(lambda: emit(lambda a, rs: _sigmoid(a)))


def _proj(x2d, scale1_3, shift3, w_main, w_lr, w_lr2, b_lr2, cos, sin, *, tm, rows_per_mod, p_dtype, cum_chunk=0):
    assert cum_chunk == 0 or (tm % cum_chunk == 0 and rows_per_mod % cum_chunk == 0)
    n = x2d.shape[0]
    rope_tiles = cos.shape[0] // tm
    rope_spec = pl.BlockSpec((tm, RET_DK // 2), lambda i, j: (i % rope_tiles, 0))
    per_row = shift3.shape[1] != 1
    if per_row:
        mod_spec = pl.BlockSpec((None, tm, D_MODEL), lambda i, j: (0, i, 0))
    else:
        tiles_per_mod = rows_per_mod // tm
        mod_spec = pl.BlockSpec((None, 1, D_MODEL), lambda i, j: (i // tiles_per_mod, 0, 0))
    tn = PROJ_TN
    return pl.pallas_call(
        functools.partial(_proj_kernel, cum_chunk),
        grid=(n // tm, P_COLS // tn),
        in_specs=[pl.BlockSpec((tm, D_MODEL), lambda i, j: (i, 0)),
                  mod_spec, mod_spec,
                  pl.BlockSpec((tn, D_MODEL), lambda i, j: (j, 0)),
                  pl.BlockSpec((LR_PAD, D_MODEL), lambda i, j: (0, 0)),
                  pl.BlockSpec((LR_PAD, GK), lambda i, j: (0, 0)),
                  pl.BlockSpec((1, GK), lambda i, j: (0, 0)),
                  rope_spec, rope_spec],
        out_specs=[pl.BlockSpec((tm, tn), lambda i, j: (i, j)),
                   pl.BlockSpec((tm, GK), lambda i, j: (i, 0))],
        out_shape=[jax.ShapeDtypeStruct((n, P_COLS), p_dtype),
                   jax.ShapeDtypeStruct((n, GK), F32)],
        scratch_shapes=[pltpu.VMEM((tm, D_MODEL), BF16)],
        compiler_params=pltpu.CompilerParams(dimension_semantics=("parallel", "arbitrary"),
                                             vmem_limit_bytes=VMEM_LIMIT),
        name="proj",
    )(x2d, scale1_3, shift3, w_main, w_lr, w_lr2, b_lr2, cos, sin)


def _rms_gate(o, sz):
    o = o * lax.rsqrt(jnp.mean(o * o, axis=-1, keepdims=True) + HEAD_NORM_EPS)
    return o.astype(sz.dtype) * sz


def _group_gate(o, sz):
    mu = jnp.mean(o, axis=-1, keepdims=True)
    c = o - mu
    var = jnp.mean(c * c, axis=-1, keepdims=True)
    return (c * lax.rsqrt(var + HEAD_NORM_EPS)).astype(sz.dtype) * sz


def _cumsum_rows_mxu(x):
    n = x.shape[0]
    row = lax.broadcasted_iota(jnp.int32, (n, n), 0)
    col = lax.broadcasted_iota(jnp.int32, (n, n), 1)
    tri = jnp.where(col <= row, 1.0, 0.0).astype(BF16)
    hi = x.astype(BF16)
    rest = x - hi.astype(F32)
    mid = rest.astype(BF16)
    lo = (rest - mid.astype(F32)).astype(BF16)
    return _dot(jnp.concatenate([tri, tri, tri], axis=1), jnp.concatenate([hi, mid, lo], axis=0))


def _block_edge(b, w, prev):
    n, m = b.shape
    pieces = []
    for i in range(n // w):
        r = i * w - 1 if prev else i * w + w - 1
        if r < 0:
            pieces.append(jnp.zeros((w, m), F32))
        else:
            pieces.append(jnp.broadcast_to(b[r:r + 1, :], (w, m)))
    return jnp.concatenate(pieces, axis=0)


def _gla_levels(c):
    w = GLA_SUB
    while w < c:
        yield w
        w *= 2


def _gla_operands(q, k, b):
    c = q.shape[0]
    btot = b[c - 1:c, :]
    rel = b - _block_edge(b, GLA_SUB, True)
    ops = {"q_in": (q * jnp.exp(b)).astype(BF16),
           "k_out": (k * jnp.exp(btot - b)).astype(BF16),
           "qd": (q * jnp.exp(rel)).astype(BF16),
           "kd": (k * jnp.exp(-rel)).astype(BF16)}
    for w in _gla_levels(c):
        if w > GLA_SUB:
            ops["q%d" % w] = (q * jnp.exp(b - _block_edge(b, w, True))).astype(BF16)
        ops["k%d" % w] = (k * jnp.exp(_block_edge(b, w, False) - b)).astype(BF16)
    return ops


def _gla_masks(c):
    row = lax.broadcasted_iota(jnp.int32, (c, c), 0)
    col = lax.broadcasted_iota(jnp.int32, (c, c), 1)
    sh = GLA_SUB.bit_length() - 1
    masks = {"diag": ((row >> sh) == (col >> sh)) & (col <= row)}
    for w in _gla_levels(c):
        sh = w.bit_length() - 1
        rb, cb = row >> sh, col >> sh
        masks[w] = ((rb & 1) == 1) & (cb == rb - 1)
    return masks


def _gla_scores(ops, hs, masks, c):
    a = jnp.where(masks["diag"], _dot_nt(ops["qd"][:, hs], ops["kd"][:, hs]), 0.0)
    for w in _gla_levels(c):
        qw = ops["qd"] if w == GLA_SUB else ops["q%d" % w]
        a = jnp.where(masks[w], _dot_nt(qw[:, hs], ops["k%d" % w][:, hs]), a)
    return a.astype(BF16)


def _rotary_heads(x, cos, sin, heads, scale=1.0):
    half = cos.shape[1]
    if scale != 1.0:
        cos, sin = cos * scale, sin * scale
    cos_t = jnp.concatenate([cos] * (2 * heads), axis=1)
    sin_t = jnp.concatenate([-sin, sin] * heads, axis=1)
    partner = [i + 1 - 2 * (i % 2) for i in range(2 * heads)]
    swapped = jnp.concatenate([x[:, j * half:(j + 1) * half] for j in partner], axis=1)
    return x * cos_t + swapped * sin_t


def _mixer_prompt_kernel(d_chunk, vr_ref, zr_ref, qr_ref, kr_ref, vg_ref, zg_ref, qkg_ref, b_ref,
                         dintra_ref, dread_ref, dwrite_ref,
                         og_ref, or_ref, sg_ref, sr_ref):
    @pl.when(pl.program_id(1) == 0)
    def _():
        sg_ref[...] = jnp.zeros_like(sg_ref)
        sr_ref[...] = jnp.zeros_like(sr_ref)

    c = CHUNK
    g_sl = [slice(h * GLA_DK, (h + 1) * GLA_DK) for h in range(GLA_HEADS)]
    gv_sl = [slice(h * GLA_DV, (h + 1) * GLA_DV) for h in range(GLA_HEADS)]
    r_sl = [slice(h * RET_DK, (h + 1) * RET_DK) for h in range(RET_HEADS)]
    rv_sl = [slice(h * RET_DV, (h + 1) * RET_DV) for h in range(RET_HEADS)]

    b = b_ref[...]
    qb = qr_ref[...]
    kb = kr_ref[...]
    q_read = qb * dread_ref[...].astype(BF16)
    k_write = kb * dwrite_ref[...].astype(BF16)
    g_rows = jnp.exp(b[c - 1:c, :])
    g_cols = [jnp.broadcast_to(g_rows[:, s], (GLA_DK, GLA_DK)).T[:, :1] for s in g_sl]
    gops = _gla_operands(qkg_ref[:, :GK].astype(F32) * (GLA_DK ** -0.5), qkg_ref[:, GK:].astype(F32), b)

    masks = _gla_masks(c)
    a_g = [_gla_scores(gops, s, masks, c) for s in g_sl]
    a_r = [(_dot_nt(qb[:, s], kb[:, s]) * dintra_ref[h]).astype(BF16) for h, s in enumerate(r_sl)]

    for h in range(GLA_HEADS):
        o = _dot(gops["q_in"][:, g_sl[h]], sg_ref[h].astype(BF16)) + _dot(a_g[h], vg_ref[:, gv_sl[h]])
        og_ref[:, gv_sl[h]] = _rms_gate(o, zg_ref[:, gv_sl[h]])
    for h in range(RET_HEADS):
        o = _dot(a_r[h], vr_ref[:, rv_sl[h]]) + _dot(q_read[:, r_sl[h]], sr_ref[h].astype(BF16))
        or_ref[:, rv_sl[h]] = _group_gate(o, zr_ref[:, rv_sl[h]])

    for h in range(GLA_HEADS):
        sg_ref[h] = g_cols[h] * sg_ref[h] + _dot(gops["k_out"][:, g_sl[h]].T, vg_ref[:, gv_sl[h]])
    for h in range(RET_HEADS):
        sr_ref[h] = d_chunk[h] * sr_ref[h] + _dot(k_write[:, r_sl[h]].T, vr_ref[:, rv_sl[h]])


def _ret_consts(length):
    h = np.arange(RET_HEADS, dtype=np.float64)
    log_gamma = np.log1p(-np.exp2(-5.0 - h))
    idx = np.arange(length, dtype=np.float64)
    diff = idx[:, None] - idx[None, :]
    d_intra = np.where(diff[None] >= 0, np.exp(np.maximum(diff, 0.0)[None] * log_gamma[:, None, None]), 0.0)
    d_read = np.exp((idx + 1.0)[None, :] * log_gamma[:, None])
    d_write = np.exp((length - 1.0 - idx)[None, :] * log_gamma[:, None])
    d_chunk = np.exp(length * log_gamma)
    return (d_intra.astype(np.float32), d_read.astype(np.float32)[..., None],
            d_write.astype(np.float32)[..., None], tuple(float(np.float32(x)) for x in d_chunk))


def _rope_tables(pos):
    half = RET_DK // 2
    inv_freq = ROPE_BASE ** (-jnp.arange(half, dtype=F32) / half)
    ang = pos.astype(F32)[:, None] * inv_freq[None, :]
    return jnp.cos(ang), jnp.sin(ang)


def _mixer_prompt(p, la, batch, seq):
    assert p.dtype == BF16
    nchunk = seq // CHUNK
    d_intra, d_read, d_write, d_chunk = _ret_consts(CHUNK)
    d_read = np.repeat(d_read[..., 0].T, RET_DK, axis=1)
    d_write = np.repeat(d_write[..., 0].T, RET_DK, axis=1)

    def pcol(width, blk):
        return pl.BlockSpec((CHUNK, width), lambda b, t: (b * nchunk + t, blk))

    const3 = lambda shape: pl.BlockSpec(shape, lambda b, t: (0, 0, 0))
    tok = lambda width: pl.BlockSpec((CHUNK, width), lambda b, t: (b * nchunk + t, 0))
    return pl.pallas_call(
        functools.partial(_mixer_prompt_kernel, d_chunk),
        grid=(batch, nchunk),
        in_specs=[pcol(RV, BLK_VR), pcol(RV, BLK_ZR), pcol(RK, BLK_QR), pcol(RK, BLK_KR),
                  pcol(GV, BLK_VG), pcol(GV, BLK_ZG), pcol(2 * GK, BLK_QKG), tok(GK),
                  const3((RET_HEADS, CHUNK, CHUNK)),
                  pl.BlockSpec((CHUNK, RK), lambda b, t: (0, 0)),
                  pl.BlockSpec((CHUNK, RK), lambda b, t: (0, 0))],
        out_specs=[tok(GV), tok(RV),
                   pl.BlockSpec((None, GLA_HEADS, GLA_DK, GLA_DV), lambda b, t: (b, 0, 0, 0)),
                   pl.BlockSpec((None, RET_HEADS, RET_DK, RET_DV), lambda b, t: (b, 0, 0, 0))],
        out_shape=[jax.ShapeDtypeStruct((batch * seq, GV), BF16),
                   jax.ShapeDtypeStruct((batch * seq, RV), BF16),
                   jax.ShapeDtypeStruct((batch, GLA_HEADS, GLA_DK, GLA_DV), F32),
                   jax.ShapeDtypeStruct((batch, RET_HEADS, RET_DK, RET_DV), F32)],
        compiler_params=pltpu.CompilerParams(dimension_semantics=("parallel", "arbitrary"),
                                             vmem_limit_bytes=VMEM_LIMIT),
        name="mixer_prompt",
    )(p, p, p, p, p, p, p, la, jnp.asarray(d_intra), jnp.asarray(d_read), jnp.asarray(d_write))


def _pad_rows(x, rows):
    return jnp.concatenate([x, jnp.zeros((rows - x.shape[0], x.shape[1]), x.dtype)], axis=0)


def _mixer_sample_kernel(seq, gammas, d_chunk, vr_ref, zr_ref, qr_ref, kr_ref, vg_ref, zg_ref, qkg_ref,
                         la_ref, dread_ref, dwrite_ref,
                         sg_in, sr_in, og_ref, or_ref, sg_ref, sr_ref):
    rows = SAMPLE_ROWS
    n_el = rows // seq
    row1 = lax.broadcasted_iota(jnp.int32, (rows, 1), 0)
    pos1 = row1 % seq
    tcol = lax.broadcasted_iota(jnp.int32, (LANES, LANES), 1)

    def seg_cumsum(x):
        s = 1
        while s < seq:
            x = x + jnp.where(pos1 >= s, pltpu.roll(x, s, 0), 0.0)
            s *= 2
        return x

    def intra(q, k, v, gamma):
        out = jnp.zeros_like(v)
        for d in range(seq):
            kd = k if d == 0 else pltpu.roll(k, d, 0)
            vd = v if d == 0 else pltpu.roll(v, d, 0)
            a = jnp.sum(q * kd, axis=1, keepdims=True) * (gamma ** d)
            out = out + jnp.where(pos1 >= d, a, 0.0) * vd
        return out

    def inter(q, states):
        qb = q.astype(BF16)
        out = None
        for e in range(n_el):
            oe = _dot(qb, states[e].astype(BF16))
            out = oe if out is None else jnp.where(row1 >= e * seq, oe, out)
        return out

    for h in range(GLA_HEADS):
        kq = slice(h * GLA_DK, (h + 1) * GLA_DK)
        kk = slice(GK + h * GLA_DK, GK + (h + 1) * GLA_DK)
        vv = slice(h * GLA_DV, (h + 1) * GLA_DV)
        q = qkg_ref[:, kq] * (GLA_DK ** -0.5)
        k = qkg_ref[:, kk]
        v = vg_ref[:, vv]
        la = la_ref[:, kq]
        b = seg_cumsum(la)
        q_dec = q * jnp.exp(b)
        k_dec = k * jnp.exp(-b)
        states = [sg_in[e, h] for e in range(n_el)]
        o = inter(q_dec, states) + intra(q_dec, k_dec, v, 1.0)
        og_ref[:, vv] = _rms_gate(o, zg_ref[:, vv])
        b_last = jnp.zeros_like(b)
        for e in range(n_el):
            r = e * seq + seq - 1
            b_last = jnp.where((row1 >= e * seq) & (row1 < (e + 1) * seq),
                               jnp.broadcast_to(b[r:r + 1, :], b.shape), b_last)
        k_out = k * jnp.exp(b_last - b)
        k_outT = _pad_rows(k_out, LANES).T
        laT = _pad_rows(la, LANES).T
        vb = _pad_rows(v, LANES).astype(BF16)
        tl = tcol[:GLA_DK]
        for e in range(n_el):
            sel = (tl >= e * seq) & (tl < (e + 1) * seq)
            g_col = jnp.exp(jnp.sum(jnp.where(sel, laT, 0.0), axis=1, keepdims=True))
            sg_ref[e, h] = g_col * states[e] + _dot(jnp.where(sel, k_outT, 0.0).astype(BF16), vb)

    for h in range(RET_HEADS):
        kq = slice(h * RET_DK, (h + 1) * RET_DK)
        vv = slice(h * RET_DV, (h + 1) * RET_DV)
        q = qr_ref[:, kq]
        k = kr_ref[:, kq]
        v = vr_ref[:, vv]
        states = [sr_in[e, h] for e in range(n_el)]
        o = inter(q, states) * dread_ref[h] + intra(q, k, v, gammas[h])
        or_ref[:, vv] = _group_gate(o, zr_ref[:, vv])
        kwT = _pad_rows(k * dwrite_ref[h], LANES).T
        vb = _pad_rows(v, LANES).astype(BF16)
        tl = jnp.concatenate([tcol, tcol], axis=0)
        for e in range(n_el):
            sel = (tl >= e * seq) & (tl < (e + 1) * seq)
            sr_ref[e, h] = d_chunk[h] * states[e] + _dot(jnp.where(sel, kwT, 0.0).astype(BF16), vb)


def _mixer_sample(p, la, state_gla, state_ret, batch, seq):
    rows = SAMPLE_ROWS
    n_el = rows // seq
    _, d_read, d_write, d_chunk = _ret_consts(seq)
    gammas = tuple(float(1.0 - 2.0 ** (-5.0 - h)) for h in range(RET_HEADS))
    d_read = np.tile(d_read, (1, n_el, 1))
    d_write = np.tile(d_write, (1, n_el, 1))

    pcol = lambda width, blk: pl.BlockSpec((rows, width), lambda i: (i, blk))
    tok = lambda width: pl.BlockSpec((rows, width), lambda i: (i, 0))
    full3 = lambda shape: pl.BlockSpec(shape, lambda i: (0, 0, 0))
    sg_spec = pl.BlockSpec((n_el, GLA_HEADS, GLA_DK, GLA_DV), lambda i: (i, 0, 0, 0))
    sr_spec = pl.BlockSpec((n_el, RET_HEADS, RET_DK, RET_DV), lambda i: (i, 0, 0, 0))
    return pl.pallas_call(
        functools.partial(_mixer_sample_kernel, seq, gammas, d_chunk),
        grid=(batch * seq // rows,),
        in_specs=[pcol(RV, BLK_VR), pcol(RV, BLK_ZR), pcol(RK, BLK_QR), pcol(RK, BLK_KR),
                  pcol(GV, BLK_VG), pcol(GV, BLK_ZG), pcol(2 * GK, BLK_QKG), tok(GK),
                  full3((RET_HEADS, rows, 1)), full3((RET_HEADS, rows, 1)),
                  sg_spec, sr_spec],
        out_specs=[tok(GV), tok(RV), sg_spec, sr_spec],
        out_shape=[jax.ShapeDtypeStruct((batch * seq, GV), F32),
                   jax.ShapeDtypeStruct((batch * seq, RV), F32),
                   jax.ShapeDtypeStruct(state_gla.shape, F32),
                   jax.ShapeDtypeStruct(state_ret.shape, F32)],
        compiler_params=pltpu.CompilerParams(dimension_semantics=("parallel",),
                                             vmem_limit_bytes=VMEM_LIMIT),
        name="mixer_sample",
    )(p, p, p, p, p, p, p, la, jnp.asarray(d_read), jnp.asarray(d_write), state_gla, state_ret)


def _out_kernel(og_ref, or_ref, mg_ref, mr_ref, x_ref, gate_ref, wbg_ref, wbr_ref, wo_ref, lng_ref, lnb_ref,
                y_ref):
    sub = min(OUT_SUB, y_ref.shape[0])
    blocks = [slice(r0, r0 + sub) for r0 in range(0, y_ref.shape[0], sub)]
    merged = []
    for rs in blocks:
        bg = _dot(og_ref[rs, :].astype(BF16), wbg_ref[...])
        br = _dot(or_ref[rs, :].astype(BF16), wbr_ref[...])
        merged.append((mg_ref[rs, :].astype(F32) * bg + mr_ref[rs, :].astype(F32) * br).astype(BF16))
    for rs, m in zip(blocks, merged):
        out = _dot(m, wo_ref[...])
        gate = gate_ref[...] if gate_ref.shape[0] == 1 else gate_ref[rs, :]
        r = DEEPNORM_ALPHA * x_ref[rs, :] + gate * out
        mu = jnp.mean(r, axis=-1, keepdims=True)
        c = r - mu
        var = jnp.mean(c * c, axis=-1, keepdims=True)
        y_ref[rs, :] = c * lax.rsqrt(var + LN_EPS) * lng_ref[...] + lnb_ref[...]


def _out_stage(og, orr, p, x2d, gate3, wbg, wbr, wo, ln_g, ln_b, *, tm, rows_per_mod):
    n = x2d.shape[0]
    if gate3.shape[1] != 1:
        gate_spec = pl.BlockSpec((None, tm, D_MODEL), lambda i: (0, i, 0))
    else:
        tiles_per_mod = rows_per_mod // tm
        gate_spec = pl.BlockSpec((None, 1, D_MODEL), lambda i: (i // tiles_per_mod, 0, 0))
    full = lambda shape: pl.BlockSpec(shape, lambda i: (0, 0), pipeline_mode=pl.Buffered(1))
    return pl.pallas_call(
        _out_kernel,
        grid=(n // tm,),
        in_specs=[pl.BlockSpec((tm, GV), lambda i: (i, 0)),
                  pl.BlockSpec((tm, RV), lambda i: (i, 0)),
                  pl.BlockSpec((tm, D_MODEL), lambda i: (i, BLK_MG)),
                  pl.BlockSpec((tm, D_MODEL), lambda i: (i, BLK_MR)),
                  pl.BlockSpec((tm, D_MODEL), lambda i: (i, 0)),
                  gate_spec,
                  full((GV, D_MODEL)), full((RV, D_MODEL)), full((D_MODEL, D_MODEL)),
                  full((1, D_MODEL)), full((1, D_MODEL))],
        out_specs=pl.BlockSpec((tm, D_MODEL), lambda i: (i, 0)),
        out_shape=jax.ShapeDtypeStruct((n, D_MODEL), F32),
        compiler_params=pltpu.CompilerParams(dimension_semantics=("parallel",),
                                             vmem_limit_bytes=VMEM_LIMIT),
        name="out_stage",
    )(og, orr, p, p, x2d, gate3, wbg, wbr, wo, ln_g, ln_b)


_W_SRC_BLK = (_O_VR // PROJ_TN, _O_VR // PROJ_TN + 1, _O_QR // PROJ_TN, _O_KR // PROJ_TN, _O_VG // PROJ_TN,
              _O_QG // PROJ_TN, _O_ZR // PROJ_TN, _O_ZR // PROJ_TN + 1, _O_ZG // PROJ_TN, _O_MG // PROJ_TN,
              _O_MR // PROJ_TN)
_LR_BLK = _O_LR // PROJ_TN


def _prep_w_kernel(src_ref, a_ref, b_ref, o_ref, lr_ref):
    blk = src_ref[pl.program_id(0)]
    keep = PROJ_TN - GLA_LOWRANK

    @pl.when(blk >= _LR_BLK)
    def _():
        o_ref[:keep, :] = a_ref[GLA_LOWRANK:, :].astype(BF16)
        o_ref[keep:, :] = b_ref[...].astype(BF16)

    @pl.when(blk < _LR_BLK)
    def _():
        o_ref[...] = a_ref[...].astype(BF16)

    @pl.when(blk == _LR_BLK)
    def _():
        lr_ref[:GLA_LOWRANK, :] = a_ref[:GLA_LOWRANK, :].astype(BF16)
        lr_ref[GLA_LOWRANK:, :] = jnp.zeros((LR_PAD - GLA_LOWRANK, D_MODEL), BF16)


def _prep_w_in(wt):
    assert all(o % PROJ_TN in (0, GLA_LOWRANK) for o in (_O_VR, _O_QR, _O_KR, _O_VG, _O_QG, _O_ZR, _O_ZG,
                                                         _O_MG, _O_MR)) and _O_LR % PROJ_TN == 0
    src = jnp.asarray(_W_SRC_BLK, jnp.int32)
    per_blk = PROJ_TN // GLA_LOWRANK
    return pl.pallas_call(
        _prep_w_kernel,
        grid_spec=pltpu.PrefetchScalarGridSpec(
            num_scalar_prefetch=1,
            grid=(len(_W_SRC_BLK),),
            in_specs=[pl.BlockSpec((PROJ_TN, D_MODEL), lambda j, src: (src[j], 0)),
                      pl.BlockSpec((GLA_LOWRANK, D_MODEL), lambda j, src: ((src[j] + 1) * per_blk, 0))],
            out_specs=[pl.BlockSpec((PROJ_TN, D_MODEL), lambda j, src: (j, 0)),
                       pl.BlockSpec((LR_PAD, D_MODEL), lambda j, src: (0, 0))]),
        out_shape=[jax.ShapeDtypeStruct((P_COLS, D_MODEL), BF16),
                   jax.ShapeDtypeStruct((LR_PAD, D_MODEL), BF16)],
        compiler_params=pltpu.CompilerParams(dimension_semantics=("arbitrary",),
                                             vmem_limit_bytes=VMEM_LIMIT),
        name="prep_w",
    )(src, wt, wt)


def kernel(x_prompt, x_sample, state_gla, state_ret, c_prompt, c_sample, w_ada, b_ada, w_in, w_lr2, b_lr2,
           gla_norm_g, ret_norm_g, w_branch_gla, w_branch_ret, w_out, ln_g, ln_b):
    assert w_ada.shape[0] == 1, "one layer"
    bp, tp, _ = x_prompt.shape
    bs, ts, _ = x_sample.shape

    mod = _ada(jnp.concatenate([c_prompt, c_sample], axis=0), w_ada[0], b_ada)
    shift, scale1, gate = mod[:, :D_MODEL], mod[:, D_MODEL:2 * D_MODEL], mod[:, 2 * D_MODEL:]
    per_group = lambda a: a[:bp][:, None, :]
    per_token = lambda a: jnp.repeat(a[bp:], ts, axis=0)[None]

    w_main, w_lr = _prep_w_in(w_in[0].T)
    w_lr2p = jnp.pad(w_lr2[0], ((0, LR_PAD - GLA_LOWRANK), (0, 0))).astype(BF16)
    wbg = (gla_norm_g[0][:, None] * w_branch_gla[0]).astype(BF16)
    wbr = (ret_norm_g[0][:, None] * w_branch_ret[0]).astype(BF16)
    wo = w_out[0].astype(BF16)

    xp = x_prompt.reshape(bp * tp, D_MODEL)
    cos_p, sin_p = _rope_tables(jnp.arange(tp, dtype=jnp.int32))
    p_p, la_p = _proj(xp, per_group(scale1), per_group(shift), w_main, w_lr, w_lr2p, b_lr2, cos_p, sin_p,
                      tm=2048, rows_per_mod=tp, p_dtype=BF16, cum_chunk=CHUNK)
    og_p, or_p, sg_p, sr_p = _mixer_prompt(p_p, la_p, bp, tp)
    y_p = _out_stage(og_p, or_p, p_p, xp, per_group(gate), wbg, wbr, wo, ln_g, ln_b, tm=1024, rows_per_mod=tp)

    xs = x_sample.reshape(bs * ts, D_MODEL)
    cos_s, sin_s = _rope_tables(PAST_LEN + jnp.arange(bs * ts, dtype=jnp.int32) % ts)
    p_s, la_s = _proj(xs, per_token(scale1), per_token(shift), w_main, w_lr, w_lr2p, b_lr2, cos_s, sin_s,
                      tm=bs * ts, rows_per_mod=ts, p_dtype=F32)
    og_s, or_s, sg_s, sr_s = _mixer_sample(p_s, la_s, state_gla[0], state_ret[0], bs, ts)
    y_s = _out_stage(og_s, or_s, p_s, xs, per_token(gate), wbg, wbr, wo, ln_g, ln_b,
                     tm=bs * ts, rows_per_mod=ts)

    return (y_p.reshape(bp, tp, D_MODEL), y_s.reshape(bs, ts, D_MODEL),
            sg_p[None], sr_p[None], sg_s[None], sr_s[None])
```

```python
import functools

import numpy as np
import jax
import jax.numpy as jnp
from jax import lax
from jax.experimental import pallas as pl
from jax.experimental.pallas import tpu as pltpu

F32 = jnp.float32
BF16 = jnp.bfloat16

D_MODEL = 1024
PAST_LEN = 16384
GLA_HEADS, GLA_DK, GLA_DV = 4, 128, 256
RET_HEADS, RET_DK, RET_DV = 4, 256, 512
GK, GV = GLA_HEADS * GLA_DK, GLA_HEADS * GLA_DV
RK, RV = RET_HEADS * RET_DK, RET_HEADS * RET_DV
GLA_LOWRANK = 16
GLA_TAU = 16.0
GLA_SUB = 16
ROPE_BASE = 10000.0
DEEPNORM_ALPHA = 2.0 ** 0.25
LN_EPS = 1e-5
HEAD_NORM_EPS = 1e-5

CHUNK = 128
MIXER_CHUNKS = 4
SAMPLE_ROWS = 16
LANES = 128
LR_PAD = 256

PROJ_TN = 1024
PROJ_SUB = 1024
OUT_SUB = 256
P_COLS = 11 * PROJ_TN
BLK_VR, BLK_ZR = 0, 3
BLK_QR, BLK_KR, BLK_VG, BLK_QKG, BLK_ZG, BLK_MG, BLK_MR = 2, 3, 4, 5, 8, 9, 10
FIRST_SILU_BLK, FIRST_SIGMOID_BLK = 6, 9
_O_QG, _O_KG, _O_VG, _O_ZG, _O_LR = 0, GK, 2 * GK, 2 * GK + GV, 2 * GK + 2 * GV
_O_QR = _O_LR + GLA_LOWRANK
_O_KR = _O_QR + RK
_O_VR = _O_KR + RK
_O_ZR = _O_VR + RV
_O_MG = _O_ZR + RV
_O_MR = _O_MG + D_MODEL
_O_END = _O_MR + D_MODEL

VMEM_LIMIT = 56 * 1024 * 1024


def _dot(a, b):
    return jnp.dot(a, b, preferred_element_type=F32)


def _dot_nt(a, b):
    return lax.dot_general(a, b, (((1,), (1,)), ((), ())), preferred_element_type=F32)


def _log_sigmoid(x):
    return -(jnp.maximum(-x, 0.0) + jnp.log(1.0 + jnp.exp(-jnp.abs(x))))


def _sigmoid(x):
    return 0.5 * jnp.tanh(0.5 * x) + 0.5


def _silu(x):
    return x * _sigmoid(x)


def _ada_kernel(c_ref, w_ref, b_ref, o_ref):
    one = jnp.where(pl.program_id(0) == 1, 1.0, 0.0)
    o_ref[...] = _dot(c_ref[...].astype(BF16), w_ref[...].astype(BF16)) + (b_ref[...] + one)


def _ada(c, w_ada, b_ada):
    n = c.shape[0]
    return pl.pallas_call(
        _ada_kernel,
        grid=(3,),
        in_specs=[pl.BlockSpec((n, D_MODEL), lambda j: (0, 0)),
                  pl.BlockSpec((D_MODEL, D_MODEL), lambda j: (0, j)),
                  pl.BlockSpec((1, D_MODEL), lambda j: (0, j))],
        out_specs=pl.BlockSpec((n, D_MODEL), lambda j: (0, j)),
        out_shape=jax.ShapeDtypeStruct((n, 3 * D_MODEL), F32),
        compiler_params=pltpu.CompilerParams(dimension_semantics=("arbitrary",)),
        name="ada",
    )(c, w_ada, b_ada)


def _proj_kernel(cum_chunk, x_ref, scale1_ref, shift_ref, w_ref, wlr_ref, wlr2_ref, blr2_ref, cos_ref, sin_ref,
                 p_ref, la_ref, h_ref):
    j = pl.program_id(1)

    @pl.when(j == 0)
    def _():
        h = (x_ref[...] * scale1_ref[...] + shift_ref[...]).astype(BF16)
        h_ref[...] = h
        sub = min(PROJ_SUB, h.shape[0])
        lr = jnp.concatenate([_dot_nt(h[r:r + sub, :], wlr_ref[...]) for r in range(0, h.shape[0], sub)], axis=0)
        pre = _dot(lr.astype(BF16), wlr2_ref[...]) + blr2_ref[...]
        la = _log_sigmoid(pre) * (1.0 / GLA_TAU)
        if cum_chunk:
            for r in range(0, la.shape[0], cum_chunk):
                la_ref[r:r + cum_chunk, :] = _cumsum_rows_mxu(la[r:r + cum_chunk, :])
        else:
            la_ref[...] = la

    def emit(epilogue):
        sub = min(PROJ_SUB, p_ref.shape[0])
        for r in range(0, p_ref.shape[0], sub):
            rs = slice(r, r + sub)
            p_ref[rs, :] = epilogue(_dot_nt(h_ref[rs, :], w_ref[...]), rs).astype(p_ref.dtype)

    def rotate(scale):
        return lambda a, rs: _rotary_heads(a, cos_ref[rs, :], sin_ref[rs, :], RET_HEADS, scale)

    plain = (j < FIRST_SILU_BLK) & (j != BLK_QR) & (j != BLK_KR)
    pl.when(plain)(lambda: emit(lambda a, rs: a))
    pl.when(j == BLK_QR)(lambda: emit(rotate(1.0)))
    pl.when(j == BLK_KR)(lambda: emit(rotate(RET_DK ** -0.5)))
    pl.when((j >= FIRST_SILU_BLK) & (j < FIRST_SIGMOID_BLK))(lambda: emit(lambda a, rs: _silu(a)))
    pl.when(j >= FIRST_SIGMOID_BLK)(lambda: emit(lambda a, rs: _sigmoid(a)))


def _proj(x2d, scale1_3, shift3, w_main, w_lr, w_lr2, b_lr2, cos, sin, *, tm, rows_per_mod, p_dtype, cum_chunk=0):
    assert cum_chunk == 0 or (tm % cum_chunk == 0 and rows_per_mod % cum_chunk == 0)
    n = x2d.shape[0]
    rope_tiles = cos.shape[0] // tm
    rope_spec = pl.BlockSpec((tm, RET_DK // 2), lambda i, j: (i % rope_tiles, 0))
    per_row = shift3.shape[1] != 1
    if per_row:
        mod_spec = pl.BlockSpec((None, tm, D_MODEL), lambda i, j: (0, i, 0))
    else:
        tiles_per_mod = rows_per_mod // tm
        mod_spec = pl.BlockSpec((None, 1, D_MODEL), lambda i, j: (i // tiles_per_mod, 0, 0))
    tn = PROJ_TN
    return pl.pallas_call(
        functools.partial(_proj_kernel, cum_chunk),
        grid=(n // tm, P_COLS // tn),
        in_specs=[pl.BlockSpec((tm, D_MODEL), lambda i, j: (i, 0)),
                  mod_spec, mod_spec,
                  pl.BlockSpec((tn, D_MODEL), lambda i, j: (j, 0)),
                  pl.BlockSpec((LR_PAD, D_MODEL), lambda i, j: (0, 0)),
                  pl.BlockSpec((LR_PAD, GK), lambda i, j: (0, 0)),
                  pl.BlockSpec((1, GK), lambda i, j: (0, 0)),
                  rope_spec, rope_spec],
        out_specs=[pl.BlockSpec((tm, tn), lambda i, j: (i, j)),
                   pl.BlockSpec((tm, GK), lambda i, j: (i, 0))],
        out_shape=[jax.ShapeDtypeStruct((n, P_COLS), p_dtype),
                   jax.ShapeDtypeStruct((n, GK), F32)],
        scratch_shapes=[pltpu.VMEM((tm, D_MODEL), BF16)],
        compiler_params=pltpu.CompilerParams(dimension_semantics=("parallel", "arbitrary"),
                                             vmem_limit_bytes=VMEM_LIMIT),
        name="proj",
    )(x2d, scale1_3, shift3, w_main, w_lr, w_lr2, b_lr2, cos, sin)


def _rms_gate(o, sz):
    o = o * lax.rsqrt(jnp.mean(o * o, axis=-1, keepdims=True) + HEAD_NORM_EPS)
    return o.astype(sz.dtype) * sz


def _group_gate(o, sz):
    mu = jnp.mean(o, axis=-1, keepdims=True)
    c = o - mu
    var = jnp.mean(c * c, axis=-1, keepdims=True)
    return (c * lax.rsqrt(var + HEAD_NORM_EPS)).astype(sz.dtype) * sz


def _cumsum_rows_mxu(x):
    n = x.shape[0]
    row = lax.broadcasted_iota(jnp.int32, (n, n), 0)
    col = lax.broadcasted_iota(jnp.int32, (n, n), 1)
    tri = jnp.where(col <= row, 1.0, 0.0).astype(BF16)
    hi = x.astype(BF16)
    rest = x - hi.astype(F32)
    mid = rest.astype(BF16)
    lo = (rest - mid.astype(F32)).astype(BF16)
    return _dot(jnp.concatenate([tri, tri, tri], axis=1), jnp.concatenate([hi, mid, lo], axis=0))


def _block_edge(b, w, prev):
    n, m = b.shape
    pieces = []
    for i in range(n // w):
        r = i * w - 1 if prev else i * w + w - 1
        if r < 0:
            pieces.append(jnp.zeros((w, m), F32))
        else:
            pieces.append(jnp.broadcast_to(b[r:r + 1, :], (w, m)))
    return jnp.concatenate(pieces, axis=0)


def _gla_levels(c):
    w = GLA_SUB
    while w < c:
        yield w
        w *= 2


def _gla_operands(q, k, b):
    c = q.shape[0]
    btot = b[c - 1:c, :]
    rel = b - _block_edge(b, GLA_SUB, True)
    ops = {"q_in": (q * jnp.exp(b)).astype(BF16),
           "k_out": (k * jnp.exp(btot - b)).astype(BF16),
           "qd": (q * jnp.exp(rel)).astype(BF16),
           "kd": (k * jnp.exp(-rel)).astype(BF16)}
    for w in _gla_levels(c):
        if w > GLA_SUB:
            ops["q%d" % w] = (q * jnp.exp(b - _block_edge(b, w, True))).astype(BF16)
        ops["k%d" % w] = (k * jnp.exp(_block_edge(b, w, False) - b)).astype(BF16)
    return ops


def _gla_masks(c):
    row = lax.broadcasted_iota(jnp.int32, (c, c), 0)
    col = lax.broadcasted_iota(jnp.int32, (c, c), 1)
    sh = GLA_SUB.bit_length() - 1
    masks = {"diag": ((row >> sh) == (col >> sh)) & (col <= row)}
    for w in _gla_levels(c):
        sh = w.bit_length() - 1
        rb, cb = row >> sh, col >> sh
        masks[w] = ((rb & 1) == 1) & (cb == rb - 1)
    return masks


def _gla_scores(ops, hs, masks, c):
    a = jnp.where(masks["diag"], _dot_nt(ops["qd"][:, hs], ops["kd"][:, hs]), 0.0)
    for w in _gla_levels(c):
        qw = ops["qd"] if w == GLA_SUB else ops["q%d" % w]
        a = jnp.where(masks[w], _dot_nt(qw[:, hs], ops["k%d" % w][:, hs]), a)
    return a.astype(BF16)


def _rotary_heads(x, cos, sin, heads, scale=1.0):
    half = cos.shape[1]
    if scale != 1.0:
        cos, sin = cos * scale, sin * scale
    cos_t = jnp.concatenate([cos] * (2 * heads), axis=1)
    sin_t = jnp.concatenate([-sin, sin] * heads, axis=1)
    partner = [i + 1 - 2 * (i % 2) for i in range(2 * heads)]
    swapped = jnp.concatenate([x[:, j * half:(j + 1) * half] for j in partner], axis=1)
    return x * cos_t + swapped * sin_t


def _mixer_prompt_kernel(d_chunk, vr_ref, zr_ref, qr_ref, kr_ref, vg_ref, zg_ref, qkg_ref, b_ref,
                         dintra_ref, dread_ref, dwrite_ref,
                         og_ref, or_ref, sg_ref, sr_ref):
    @pl.when(pl.program_id(1) == 0)
    def _():
        sg_ref[...] = jnp.zeros_like(sg_ref)
        sr_ref[...] = jnp.zeros_like(sr_ref)

    c = CHUNK
    g_sl = [slice(h * GLA_DK, (h + 1) * GLA_DK) for h in range(GLA_HEADS)]
    gv_sl = [slice(h * GLA_DV, (h + 1) * GLA_DV) for h in range(GLA_HEADS)]
    r_sl = [slice(h * RET_DK, (h + 1) * RET_DK) for h in range(RET_HEADS)]
    rv_sl = [slice(h * RET_DV, (h + 1) * RET_DV) for h in range(RET_HEADS)]

    def chunk(i, carry):
        rows = pl.ds(pl.multiple_of(i * c, c), c)
        b = b_ref[rows, :]
        qb = qr_ref[rows, :]
        kb = kr_ref[rows, :]
        q_read = qb * dread_ref[...].astype(BF16)
        k_write = kb * dwrite_ref[...].astype(BF16)
        g_rows = jnp.exp(b[c - 1:c, :])
        g_cols = [jnp.broadcast_to(g_rows[:, s], (GLA_DK, GLA_DK)).T[:, :1] for s in g_sl]
        gops = _gla_operands(qkg_ref[rows, :GK].astype(F32) * (GLA_DK ** -0.5),
                             qkg_ref[rows, GK:].astype(F32), b)

        masks = _gla_masks(c)
        a_g = [_gla_scores(gops, s, masks, c) for s in g_sl]
        a_r = [(_dot_nt(qb[:, s], kb[:, s]) * dintra_ref[h]).astype(BF16) for h, s in enumerate(r_sl)]

        for h in range(GLA_HEADS):
            o = _dot(gops["q_in"][:, g_sl[h]], sg_ref[h].astype(BF16)) + _dot(a_g[h], vg_ref[rows, gv_sl[h]])
            og_ref[rows, gv_sl[h]] = _rms_gate(o, zg_ref[rows, gv_sl[h]])
        for h in range(RET_HEADS):
            o = _dot(a_r[h], vr_ref[rows, rv_sl[h]]) + _dot(q_read[:, r_sl[h]], sr_ref[h].astype(BF16))
            or_ref[rows, rv_sl[h]] = _group_gate(o, zr_ref[rows, rv_sl[h]])

        for h in range(GLA_HEADS):
            sg_ref[h] = g_cols[h] * sg_ref[h] + _dot(gops["k_out"][:, g_sl[h]].T, vg_ref[rows, gv_sl[h]])
        for h in range(RET_HEADS):
            sr_ref[h] = d_chunk[h] * sr_ref[h] + _dot(k_write[:, r_sl[h]].T, vr_ref[rows, rv_sl[h]])
        return carry

    lax.fori_loop(0, MIXER_CHUNKS, chunk, 0)


def _ret_consts(length):
    h = np.arange(RET_HEADS, dtype=np.float64)
    log_gamma = np.log1p(-np.exp2(-5.0 - h))
    idx = np.arange(length, dtype=np.float64)
    diff = idx[:, None] - idx[None, :]
    d_intra = np.where(diff[None] >= 0, np.exp(np.maximum(diff, 0.0)[None] * log_gamma[:, None, None]), 0.0)
    d_read = np.exp((idx + 1.0)[None, :] * log_gamma[:, None])
    d_write = np.exp((length - 1.0 - idx)[None, :] * log_gamma[:, None])
    d_chunk = np.exp(length * log_gamma)
    return (d_intra.astype(np.float32), d_read.astype(np.float32)[..., None],
            d_write.astype(np.float32)[..., None], tuple(float(np.float32(x)) for x in d_chunk))


def _rope_tables(pos):
    half = RET_DK // 2
    inv_freq = ROPE_BASE ** (-jnp.arange(half, dtype=F32) / half)
    ang = pos.astype(F32)[:, None] * inv_freq[None, :]
    return jnp.cos(ang), jnp.sin(ang)


def _mixer_prompt(p, la, batch, seq):
    assert p.dtype == BF16 and seq % (CHUNK * MIXER_CHUNKS) == 0
    d_intra, d_read, d_write, d_chunk = _ret_consts(CHUNK)
    d_read = np.repeat(d_read[..., 0].T, RET_DK, axis=1)
    d_write = np.repeat(d_write[..., 0].T, RET_DK, axis=1)

    rows = CHUNK * MIXER_CHUNKS
    nstep = seq // rows

    def pcol(width, blk):
        return pl.BlockSpec((rows, width), lambda b, t: (b * nstep + t, blk))

    const3 = lambda shape: pl.BlockSpec(shape, lambda b, t: (0, 0, 0))
    tok = lambda width: pl.BlockSpec((rows, width), lambda b, t: (b * nstep + t, 0))
    return pl.pallas_call(
        functools.partial(_mixer_prompt_kernel, d_chunk),
        grid=(batch, nstep),
        in_specs=[pcol(RV, BLK_VR), pcol(RV, BLK_ZR), pcol(RK, BLK_QR), pcol(RK, BLK_KR),
                  pcol(GV, BLK_VG), pcol(GV, BLK_ZG), pcol(2 * GK, BLK_QKG), tok(GK),
                  const3((RET_HEADS, CHUNK, CHUNK)),
                  pl.BlockSpec((CHUNK, RK), lambda b, t: (0, 0)),
                  pl.BlockSpec((CHUNK, RK), lambda b, t: (0, 0))],
        out_specs=[tok(GV), tok(RV),
                   pl.BlockSpec((None, GLA_HEADS, GLA_DK, GLA_DV), lambda b, t: (b, 0, 0, 0)),
                   pl.BlockSpec((None, RET_HEADS, RET_DK, RET_DV), lambda b, t: (b, 0, 0, 0))],
        out_shape=[jax.ShapeDtypeStruct((batch * seq, GV), BF16),
                   jax.ShapeDtypeStruct((batch * seq, RV), BF16),
                   jax.ShapeDtypeStruct((batch, GLA_HEADS, GLA_DK, GLA_DV), F32),
                   jax.ShapeDtypeStruct((batch, RET_HEADS, RET_DK, RET_DV), F32)],
        compiler_params=pltpu.CompilerParams(dimension_semantics=("parallel", "arbitrary"),
                                             vmem_limit_bytes=VMEM_LIMIT),
        name="mixer_prompt",
    )(p, p, p, p, p, p, p, la, jnp.asarray(d_intra), jnp.asarray(d_read), jnp.asarray(d_write))


def _pad_rows(x, rows):
    return jnp.concatenate([x, jnp.zeros((rows - x.shape[0], x.shape[1]), x.dtype)], axis=0)


def _mixer_sample_kernel(seq, gammas, d_chunk, vr_ref, zr_ref, qr_ref, kr_ref, vg_ref, zg_ref, qkg_ref,
                         la_ref, dread_ref, dwrite_ref,
                         sg_in, sr_in, og_ref, or_ref, sg_ref, sr_ref):
    rows = SAMPLE_ROWS
    n_el = rows // seq
    row1 = lax.broadcasted_iota(jnp.int32, (rows, 1), 0)
    pos1 = row1 % seq
    tcol = lax.broadcasted_iota(jnp.int32, (LANES, LANES), 1)

    def seg_cumsum(x):
        s = 1
        while s < seq:
            x = x + jnp.where(pos1 >= s, pltpu.roll(x, s, 0), 0.0)
            s *= 2
        return x

    def intra(q, k, v, gamma):
        out = jnp.zeros_like(v)
        for d in range(seq):
            kd = k if d == 0 else pltpu.roll(k, d, 0)
            vd = v if d == 0 else pltpu.roll(v, d, 0)
            a = jnp.sum(q * kd, axis=1, keepdims=True) * (gamma ** d)
            out = out + jnp.where(pos1 >= d, a, 0.0) * vd
        return out

    def inter(q, states):
        qb = q.astype(BF16)
        out = None
        for e in range(n_el):
            oe = _dot(qb, states[e].astype(BF16))
            out = oe if out is None else jnp.where(row1 >= e * seq, oe, out)
        return out

    for h in range(GLA_HEADS):
        kq = slice(h * GLA_DK, (h + 1) * GLA_DK)
        kk = slice(GK + h * GLA_DK, GK + (h + 1) * GLA_DK)
        vv = slice(h * GLA_DV, (h + 1) * GLA_DV)
        q = qkg_ref[:, kq] * (GLA_DK ** -0.5)
        k = qkg_ref[:, kk]
        v = vg_ref[:, vv]
        la = la_ref[:, kq]
        b = seg_cumsum(la)
        q_dec = q * jnp.exp(b)
        k_dec = k * jnp.exp(-b)
        states = [sg_in[e, h] for e in range(n_el)]
        o = inter(q_dec, states) + intra(q_dec, k_dec, v, 1.0)
        og_ref[:, vv] = _rms_gate(o, zg_ref[:, vv])
        b_last = jnp.zeros_like(b)
        for e in range(n_el):
            r = e * seq + seq - 1
            b_last = jnp.where((row1 >= e * seq) & (row1 < (e + 1) * seq),
                               jnp.broadcast_to(b[r:r + 1, :], b.shape), b_last)
        k_out = k * jnp.exp(b_last - b)
        k_outT = _pad_rows(k_out, LANES).T
        laT = _pad_rows(la, LANES).T
        vb = _pad_rows(v, LANES).astype(BF16)
        tl = tcol[:GLA_DK]
        for e in range(n_el):
            sel = (tl >= e * seq) & (tl < (e + 1) * seq)
            g_col = jnp.exp(jnp.sum(jnp.where(sel, laT, 0.0), axis=1, keepdims=True))
            sg_ref[e, h] = g_col * states[e] + _dot(jnp.where(sel, k_outT, 0.0).astype(BF16), vb)

    for h in range(RET_HEADS):
        kq = slice(h * RET_DK, (h + 1) * RET_DK)
        vv = slice(h * RET_DV, (h + 1) * RET_DV)
        q = qr_ref[:, kq]
        k = kr_ref[:, kq]
        v = vr_ref[:, vv]
        states = [sr_in[e, h] for e in range(n_el)]
        o = inter(q, states) * dread_ref[h] + intra(q, k, v, gammas[h])
        or_ref[:, vv] = _group_gate(o, zr_ref[:, vv])
        kwT = _pad_rows(k * dwrite_ref[h], LANES).T
        vb = _pad_rows(v, LANES).astype(BF16)
        tl = jnp.concatenate([tcol, tcol], axis=0)
        for e in range(n_el):
            sel = (tl >= e * seq) & (tl < (e + 1) * seq)
            sr_ref[e, h] = d_chunk[h] * states[e] + _dot(jnp.where(sel, kwT, 0.0).astype(BF16), vb)


def _mixer_sample(p, la, state_gla, state_ret, batch, seq):
    rows = SAMPLE_ROWS
    n_el = rows // seq
    _, d_read, d_write, d_chunk = _ret_consts(seq)
    gammas = tuple(float(1.0 - 2.0 ** (-5.0 - h)) for h in range(RET_HEADS))
    d_read = np.tile(d_read, (1, n_el, 1))
    d_write = np.tile(d_write, (1, n_el, 1))

    pcol = lambda width, blk: pl.BlockSpec((rows, width), lambda i: (i, blk))
    tok = lambda width: pl.BlockSpec((rows, width), lambda i: (i, 0))
    full3 = lambda shape: pl.BlockSpec(shape, lambda i: (0, 0, 0))
    sg_spec = pl.BlockSpec((n_el, GLA_HEADS, GLA_DK, GLA_DV), lambda i: (i, 0, 0, 0))
    sr_spec = pl.BlockSpec((n_el, RET_HEADS, RET_DK, RET_DV), lambda i: (i, 0, 0, 0))
    return pl.pallas_call(
        functools.partial(_mixer_sample_kernel, seq, gammas, d_chunk),
        grid=(batch * seq // rows,),
        in_specs=[pcol(RV, BLK_VR), pcol(RV, BLK_ZR), pcol(RK, BLK_QR), pcol(RK, BLK_KR),
                  pcol(GV, BLK_VG), pcol(GV, BLK_ZG), pcol(2 * GK, BLK_QKG), tok(GK),
                  full3((RET_HEADS, rows, 1)), full3((RET_HEADS, rows, 1)),
                  sg_spec, sr_spec],
        out_specs=[tok(GV), tok(RV), sg_spec, sr_spec],
        out_shape=[jax.ShapeDtypeStruct((batch * seq, GV), F32),
                   jax.ShapeDtypeStruct((batch * seq, RV), F32),
                   jax.ShapeDtypeStruct(state_gla.shape, F32),
                   jax.ShapeDtypeStruct(state_ret.shape, F32)],
        compiler_params=pltpu.CompilerParams(dimension_semantics=("parallel",),
                                             vmem_limit_bytes=VMEM_LIMIT),
        name="mixer_sample",
    )(p, p, p, p, p, p, p, la, jnp.asarray(d_read), jnp.asarray(d_write), state_gla, state_ret)


def _out_kernel(og_ref, or_ref, mg_ref, mr_ref, x_ref, gate_ref, wbg_ref, wbr_ref, wo_ref, lng_ref, lnb_ref,
                y_ref):
    sub = min(OUT_SUB, y_ref.shape[0])
    blocks = [slice(r0, r0 + sub) for r0 in range(0, y_ref.shape[0], sub)]
    merged = []
    for rs in blocks:
        bg = _dot(og_ref[rs, :].astype(BF16), wbg_ref[...])
        br = _dot(or_ref[rs, :].astype(BF16), wbr_ref[...])
        merged.append((mg_ref[rs, :].astype(F32) * bg + mr_ref[rs, :].astype(F32) * br).astype(BF16))
    for rs, m in zip(blocks, merged):
        out = _dot(m, wo_ref[...])
        gate = gate_ref[...] if gate_ref.shape[0] == 1 else gate_ref[rs, :]
        r = DEEPNORM_ALPHA * x_ref[rs, :] + gate * out
        mu = jnp.mean(r, axis=-1, keepdims=True)
        c = r - mu
        var = jnp.mean(c * c, axis=-1, keepdims=True)
        y_ref[rs, :] = c * lax.rsqrt(var + LN_EPS) * lng_ref[...] + lnb_ref[...]


def _out_stage(og, orr, p, x2d, gate3, wbg, wbr, wo, ln_g, ln_b, *, tm, rows_per_mod):
    n = x2d.shape[0]
    if gate3.shape[1] != 1:
        gate_spec = pl.BlockSpec((None, tm, D_MODEL), lambda i: (0, i, 0))
    else:
        tiles_per_mod = rows_per_mod // tm
        gate_spec = pl.BlockSpec((None, 1, D_MODEL), lambda i: (i // tiles_per_mod, 0, 0))
    full = lambda shape: pl.BlockSpec(shape, lambda i: (0, 0), pipeline_mode=pl.Buffered(1))
    return pl.pallas_call(
        _out_kernel,
        grid=(n // tm,),
        in_specs=[pl.BlockSpec((tm, GV), lambda i: (i, 0)),
                  pl.BlockSpec((tm, RV), lambda i: (i, 0)),
                  pl.BlockSpec((tm, D_MODEL), lambda i: (i, BLK_MG)),
                  pl.BlockSpec((tm, D_MODEL), lambda i: (i, BLK_MR)),
                  pl.BlockSpec((tm, D_MODEL), lambda i: (i, 0)),
                  gate_spec,
                  full((GV, D_MODEL)), full((RV, D_MODEL)), full((D_MODEL, D_MODEL)),
                  full((1, D_MODEL)), full((1, D_MODEL))],
        out_specs=pl.BlockSpec((tm, D_MODEL), lambda i: (i, 0)),
        out_shape=jax.ShapeDtypeStruct((n, D_MODEL), F32),
        compiler_params=pltpu.CompilerParams(dimension_semantics=("parallel",),
                                             vmem_limit_bytes=VMEM_LIMIT),
        name="out_stage",
    )(og, orr, p, p, x2d, gate3, wbg, wbr, wo, ln_g, ln_b)


_W_SRC_BLK = (_O_VR // PROJ_TN, _O_VR // PROJ_TN + 1, _O_QR // PROJ_TN, _O_KR // PROJ_TN, _O_VG // PROJ_TN,
              _O_QG // PROJ_TN, _O_ZR // PROJ_TN, _O_ZR // PROJ_TN + 1, _O_ZG // PROJ_TN, _O_MG // PROJ_TN,
              _O_MR // PROJ_TN)
_LR_BLK = _O_LR // PROJ_TN


def _prep_w_kernel(src_ref, a_ref, b_ref, o_ref, lr_ref):
    blk = src_ref[pl.program_id(0)]
    keep = PROJ_TN - GLA_LOWRANK

    @pl.when(blk >= _LR_BLK)
    def _():
        o_ref[:keep, :] = a_ref[GLA_LOWRANK:, :].astype(BF16)
        o_ref[keep:, :] = b_ref[...].astype(BF16)

    @pl.when(blk < _LR_BLK)
    def _():
        o_ref[...] = a_ref[...].astype(BF16)

    @pl.when(blk == _LR_BLK)
    def _():
        lr_ref[:GLA_LOWRANK, :] = a_ref[:GLA_LOWRANK, :].astype(BF16)
        lr_ref[GLA_LOWRANK:, :] = jnp.zeros((LR_PAD - GLA_LOWRANK, D_MODEL), BF16)


def _prep_w_in(wt):
    assert all(o % PROJ_TN in (0, GLA_LOWRANK) for o in (_O_VR, _O_QR, _O_KR, _O_VG, _O_QG, _O_ZR, _O_ZG,
                                                         _O_MG, _O_MR)) and _O_LR % PROJ_TN == 0
    src = jnp.asarray(_W_SRC_BLK, jnp.int32)
    per_blk = PROJ_TN // GLA_LOWRANK
    return pl.pallas_call(
        _prep_w_kernel,
        grid_spec=pltpu.PrefetchScalarGridSpec(
            num_scalar_prefetch=1,
            grid=(len(_W_SRC_BLK),),
            in_specs=[pl.BlockSpec((PROJ_TN, D_MODEL), lambda j, src: (src[j], 0)),
                      pl.BlockSpec((GLA_LOWRANK, D_MODEL), lambda j, src: ((src[j] + 1) * per_blk, 0))],
            out_specs=[pl.BlockSpec((PROJ_TN, D_MODEL), lambda j, src: (j, 0)),
                       pl.BlockSpec((LR_PAD, D_MODEL), lambda j, src: (0, 0))]),
        out_shape=[jax.ShapeDtypeStruct((P_COLS, D_MODEL), BF16),
                   jax.ShapeDtypeStruct((LR_PAD, D_MODEL), BF16)],
        compiler_params=pltpu.CompilerParams(dimension_semantics=("arbitrary",),
                                             vmem_limit_bytes=VMEM_LIMIT),
        name="prep_w",
    )(src, wt, wt)


def kernel(x_prompt, x_sample, state_gla, state_ret, c_prompt, c_sample, w_ada, b_ada, w_in, w_lr2, b_lr2,
           gla_norm_g, ret_norm_g, w_branch_gla, w_branch_ret, w_out, ln_g, ln_b):
    assert w_ada.shape[0] == 1, "one layer"
    bp, tp, _ = x_prompt.shape
    bs, ts, _ = x_sample.shape

    mod = _ada(jnp.concatenate([c_prompt, c_sample], axis=0), w_ada[0], b_ada)
    shift, scale1, gate = mod[:, :D_MODEL], mod[:, D_MODEL:2 * D_MODEL], mod[:, 2 * D_MODEL:]
    per_group = lambda a: a[:bp][:, None, :]
    per_token = lambda a: jnp.repeat(a[bp:], ts, axis=0)[None]

    w_main, w_lr = _prep_w_in(w_in[0].T)
    w_lr2p = jnp.pad(w_lr2[0], ((0, LR_PAD - GLA_LOWRANK), (0, 0))).astype(BF16)
    wbg = (gla_norm_g[0][:, None] * w_branch_gla[0]).astype(BF16)
    wbr = (ret_norm_g[0][:, None] * w_branch_ret[0]).astype(BF16)
    wo = w_out[0].astype(BF16)

    xp = x_prompt.reshape(bp * tp, D_MODEL)
    cos_p, sin_p = _rope_tables(jnp.arange(tp, dtype=jnp.int32))
    p_p, la_p = _proj(xp, per_group(scale1), per_group(shift), w_main, w_lr, w_lr2p, b_lr2, cos_p, sin_p,
                      tm=2048, rows_per_mod=tp, p_dtype=BF16, cum_chunk=CHUNK)
    og_p, or_p, sg_p, sr_p = _mixer_prompt(p_p, la_p, bp, tp)
    y_p = _out_stage(og_p, or_p, p_p, xp, per_group(gate), wbg, wbr, wo, ln_g, ln_b, tm=1024, rows_per_mod=tp)

    xs = x_sample.reshape(bs * ts, D_MODEL)
    cos_s, sin_s = _rope_tables(PAST_LEN + jnp.arange(bs * ts, dtype=jnp.int32) % ts)
    p_s, la_s = _proj(xs, per_token(scale1), per_token(shift), w_main, w_lr, w_lr2p, b_lr2, cos_s, sin_s,
                      tm=bs * ts, rows_per_mod=ts, p_dtype=F32)
    og_s, or_s, sg_s, sr_s = _mixer_sample(p_s, la_s, state_gla[0], state_ret[0], bs, ts)
    y_s = _out_stage(og_s, or_s, p_s, xs, per_token(gate), wbg, wbr, wo, ln_g, ln_b,
                     tm=bs * ts, rows_per_mod=ts)

    return (y_p.reshape(bp, tp, D_MODEL), y_s.reshape(bs, ts, D_MODEL),
            sg_p[None], sr_p[None], sg_s[None], sr_s[None])
```

```python
import functools

import numpy as np
import jax
import jax.numpy as jnp
from jax import lax
from jax.experimental import pallas as pl
from jax.experimental.pallas import tpu as pltpu

F32 = jnp.float32
BF16 = jnp.bfloat16

D_MODEL = 1024
PAST_LEN = 16384
GLA_HEADS, GLA_DK, GLA_DV = 4, 128, 256
RET_HEADS, RET_DK, RET_DV = 4, 256, 512
GK, GV = GLA_HEADS * GLA_DK, GLA_HEADS * GLA_DV
RK, RV = RET_HEADS * RET_DK, RET_HEADS * RET_DV
GLA_LOWRANK = 16
GLA_TAU = 16.0
GLA_SUB = 16
ROPE_BASE = 10000.0
DEEPNORM_ALPHA = 2.0 ** 0.25
LN_EPS = 1e-5
HEAD_NORM_EPS = 1e-5

CHUNK = 128
MIXER_CHUNKS = 4
SAMPLE_ROWS = 16
LANES = 128
LR_PAD = 256

PROJ_TN = 1024
PROJ_SUB = 1024
OUT_SUB = 256
P_COLS = 11 * PROJ_TN
BLK_VR, BLK_ZR = 0, 3
BLK_QR, BLK_KR, BLK_VG, BLK_QKG, BLK_ZG, BLK_MG, BLK_MR = 2, 3, 4, 5, 8, 9, 10
FIRST_SILU_BLK, FIRST_SIGMOID_BLK = 6, 9
_O_QG, _O_KG, _O_VG, _O_ZG, _O_LR = 0, GK, 2 * GK, 2 * GK + GV, 2 * GK + 2 * GV
_O_QR = _O_LR + GLA_LOWRANK
_O_KR = _O_QR + RK
_O_VR = _O_KR + RK
_O_ZR = _O_VR + RV
_O_MG = _O_ZR + RV
_O_MR = _O_MG + D_MODEL
_O_END = _O_MR + D_MODEL

VMEM_LIMIT = 56 * 1024 * 1024


def _dot(a, b):
    return jnp.dot(a, b, preferred_element_type=F32)


def _dot_nt(a, b):
    return lax.dot_general(a, b, (((1,), (1,)), ((), ())), preferred_element_type=F32)


def _log_sigmoid(x):
    return jnp.minimum(x, 0.0) - jnp.log(1.0 + jnp.exp(-jnp.abs(x)))


def _sigmoid(x):
    return 0.5 * jnp.tanh(0.5 * x) + 0.5


def _silu(x):
    return x * _sigmoid(x)


def _ada_kernel(c_ref, w_ref, b_ref, o_ref):
    one = jnp.where(pl.program_id(0) == 1, 1.0, 0.0)
    o_ref[...] = _dot(c_ref[...].astype(BF16), w_ref[...].astype(BF16)) + (b_ref[...] + one)


def _ada(c, w_ada, b_ada):
    n = c.shape[0]
    return pl.pallas_call(
        _ada_kernel,
        grid=(3,),
        in_specs=[pl.BlockSpec((n, D_MODEL), lambda j: (0, 0)),
                  pl.BlockSpec((D_MODEL, D_MODEL), lambda j: (0, j)),
                  pl.BlockSpec((1, D_MODEL), lambda j: (0, j))],
        out_specs=pl.BlockSpec((n, D_MODEL), lambda j: (0, j)),
        out_shape=jax.ShapeDtypeStruct((n, 3 * D_MODEL), F32),
        compiler_params=pltpu.CompilerParams(dimension_semantics=("arbitrary",)),
        name="ada",
    )(c, w_ada, b_ada)


def _proj_kernel(cum_chunk, x_ref, scale1_ref, shift_ref, w_ref, wlr_ref, wlr2_ref, blr2_ref, cos_ref, sin_ref,
                 p_ref, la_ref, h_ref):
    j = pl.program_id(1)

    @pl.when(j == 0)
    def _():
        h = (x_ref[...] * scale1_ref[...] + shift_ref[...]).astype(BF16)
        h_ref[...] = h
        sub = min(PROJ_SUB, h.shape[0])
        lr = jnp.concatenate([_dot_nt(h[r:r + sub, :], wlr_ref[...]) for r in range(0, h.shape[0], sub)], axis=0)
        pre = _dot(lr.astype(BF16), wlr2_ref[...]) + blr2_ref[...]
        la = _log_sigmoid(pre) * (1.0 / GLA_TAU)
        if cum_chunk:
            for r in range(0, la.shape[0], cum_chunk):
                la_ref[r:r + cum_chunk, :] = _cumsum_rows_mxu(la[r:r + cum_chunk, :])
        else:
            la_ref[...] = la

    def emit(epilogue):
        sub = min(PROJ_SUB, p_ref.shape[0])
        for r in range(0, p_ref.shape[0], sub):
            rs = slice(r, r + sub)
            p_ref[rs, :] = epilogue(_dot_nt(h_ref[rs, :], w_ref[...]), rs).astype(p_ref.dtype)

    def rotate(scale):
        return lambda a, rs: _rotary_heads(a, cos_ref[rs, :], sin_ref[rs, :], RET_HEADS, scale)

    plain = (j < FIRST_SILU_BLK) & (j != BLK_QR) & (j != BLK_KR)
    pl.when(plain)(lambda: emit(lambda a, rs: a))
    pl.when(j == BLK_QR)(lambda: emit(rotate(1.0)))
    pl.when(j == BLK_KR)(lambda: emit(rotate(RET_DK ** -0.5)))
    pl.when((j >= FIRST_SILU_BLK) & (j < FIRST_SIGMOID_BLK))(lambda: emit(lambda a, rs: _silu(a)))
    pl.when(j >= FIRST_SIGMOID_BLK)(lambda: emit(lambda a, rs: _sigmoid(a)))


def _proj(x2d, scale1_3, shift3, w_main, w_lr, w_lr2, b_lr2, cos, sin, *, tm, rows_per_mod, p_dtype, cum_chunk=0):
    assert cum_chunk == 0 or (tm % cum_chunk == 0 and rows_per_mod % cum_chunk == 0)
    n = x2d.shape[0]
    rope_tiles = cos.shape[0] // tm
    rope_spec = pl.BlockSpec((tm, RET_DK // 2), lambda i, j: (i % rope_tiles, 0))
    per_row = shift3.shape[1] != 1
    if per_row:
        mod_spec = pl.BlockSpec((None, tm, D_MODEL), lambda i, j: (0, i, 0))
    else:
        tiles_per_mod = rows_per_mod // tm
        mod_spec = pl.BlockSpec((None, 1, D_MODEL), lambda i, j: (i // tiles_per_mod, 0, 0))
    tn = PROJ_TN
    return pl.pallas_call(
        functools.partial(_proj_kernel, cum_chunk),
        grid=(n // tm, P_COLS // tn),
        in_specs=[pl.BlockSpec((tm, D_MODEL), lambda i, j: (i, 0)),
                  mod_spec, mod_spec,
                  pl.BlockSpec((tn, D_MODEL), lambda i, j: (j, 0)),
                  pl.BlockSpec((LR_PAD, D_MODEL), lambda i, j: (0, 0)),
                  pl.BlockSpec((LR_PAD, GK), lambda i, j: (0, 0)),
                  pl.BlockSpec((1, GK), lambda i, j: (0, 0)),
                  rope_spec, rope_spec],
        out_specs=[pl.BlockSpec((tm, tn), lambda i, j: (i, j)),
                   pl.BlockSpec((tm, GK), lambda i, j: (i, 0))],
        out_shape=[jax.ShapeDtypeStruct((n, P_COLS), p_dtype),
                   jax.ShapeDtypeStruct((n, GK), F32)],
        scratch_shapes=[pltpu.VMEM((tm, D_MODEL), BF16)],
        compiler_params=pltpu.CompilerParams(dimension_semantics=("parallel", "arbitrary"),
                                             vmem_limit_bytes=VMEM_LIMIT),
        name="proj",
    )(x2d, scale1_3, shift3, w_main, w_lr, w_lr2, b_lr2, cos, sin)


def _rms_gate(o, sz):
    o = o * lax.rsqrt(jnp.mean(o * o, axis=-1, keepdims=True) + HEAD_NORM_EPS)
    return o.astype(sz.dtype) * sz


def _group_gate(o, sz):
    mu = jnp.mean(o, axis=-1, keepdims=True)
    c = o - mu
    var = jnp.mean(c * c, axis=-1, keepdims=True)
    return (c * lax.rsqrt(var + HEAD_NORM_EPS)).astype(sz.dtype) * sz


def _cumsum_rows_mxu(x):
    n = x.shape[0]
    row = lax.broadcasted_iota(jnp.int32, (n, n), 0)
    col = lax.broadcasted_iota(jnp.int32, (n, n), 1)
    tri = jnp.where(col <= row, 1.0, 0.0).astype(BF16)
    hi = x.astype(BF16)
    rest = x - hi.astype(F32)
    mid = rest.astype(BF16)
    lo = (rest - mid.astype(F32)).astype(BF16)
    return _dot(jnp.concatenate([tri, tri, tri], axis=1), jnp.concatenate([hi, mid, lo], axis=0))


def _block_edge(b, w, prev):
    n, m = b.shape
    pieces = []
    for i in range(n // w):
        r = i * w - 1 if prev else i * w + w - 1
        if r < 0:
            pieces.append(jnp.zeros((w, m), F32))
        else:
            pieces.append(jnp.broadcast_to(b[r:r + 1, :], (w, m)))
    return jnp.concatenate(pieces, axis=0)


def _gla_levels(c):
    w = GLA_SUB
    while w < c:
        yield w
        w *= 2


def _gla_operands(q, k, b):
    c = q.shape[0]
    btot = b[c - 1:c, :]
    rel = b - _block_edge(b, GLA_SUB, True)
    ops = {"q_in": (q * jnp.exp(b)).astype(BF16),
           "k_out": (k * jnp.exp(btot - b)).astype(BF16),
           "qd": (q * jnp.exp(rel)).astype(BF16),
           "kd": (k * jnp.exp(-rel)).astype(BF16)}
    for w in _gla_levels(c):
        if w > GLA_SUB:
            ops["q%d" % w] = (q * jnp.exp(b - _block_edge(b, w, True))).astype(BF16)
        ops["k%d" % w] = (k * jnp.exp(_block_edge(b, w, False) - b)).astype(BF16)
    return ops


def _gla_level_map(c):
    row = lax.broadcasted_iota(jnp.int32, (c, c), 0)
    col = lax.broadcasted_iota(jnp.int32, (c, c), 1)
    sh = GLA_SUB.bit_length() - 1
    level = jnp.where(((row >> sh) == (col >> sh)) & (col <= row), GLA_SUB - 1, 0)
    for w in _gla_levels(c):
        sh = w.bit_length() - 1
        rb, cb = row >> sh, col >> sh
        level = jnp.where(((rb & 1) == 1) & (cb == rb - 1), w, level)
    return level


def _gla_scores(ops, hs, level, c):
    a = jnp.where(level == GLA_SUB - 1, _dot_nt(ops["qd"][:, hs], ops["kd"][:, hs]), 0.0)
    for w in _gla_levels(c):
        qw = ops["qd"] if w == GLA_SUB else ops["q%d" % w]
        a = jnp.where(level == w, _dot_nt(qw[:, hs], ops["k%d" % w][:, hs]), a)
    return a.astype(BF16)


def _rotary_heads(x, cos, sin, heads, scale=1.0):
    half = cos.shape[1]
    if scale != 1.0:
        cos, sin = cos * scale, sin * scale
    cos_t = jnp.concatenate([cos] * (2 * heads), axis=1)
    sin_t = jnp.concatenate([-sin, sin] * heads, axis=1)
    partner = [i + 1 - 2 * (i % 2) for i in range(2 * heads)]
    swapped = jnp.concatenate([x[:, j * half:(j + 1) * half] for j in partner], axis=1)
    return x * cos_t + swapped * sin_t


def _mixer_prompt_kernel(d_chunk, vr_ref, zr_ref, qr_ref, kr_ref, vg_ref, zg_ref, qkg_ref, b_ref,
                         dintra_ref, dread_ref, dwrite_ref,
                         og_ref, or_ref, sg_ref, sr_ref):
    @pl.when(pl.program_id(1) == 0)
    def _():
        sg_ref[...] = jnp.zeros_like(sg_ref)
        sr_ref[...] = jnp.zeros_like(sr_ref)

    c = CHUNK
    g_sl = [slice(h * GLA_DK, (h + 1) * GLA_DK) for h in range(GLA_HEADS)]
    gv_sl = [slice(h * GLA_DV, (h + 1) * GLA_DV) for h in range(GLA_HEADS)]
    r_sl = [slice(h * RET_DK, (h + 1) * RET_DK) for h in range(RET_HEADS)]
    rv_sl = [slice(h * RET_DV, (h + 1) * RET_DV) for h in range(RET_HEADS)]

    level = _gla_level_map(c)
    d_read = dread_ref[...].astype(BF16)
    d_write = dwrite_ref[...].astype(BF16)

    def chunk(i, carry):
        rows = pl.ds(pl.multiple_of(i * c, c), c)
        b = b_ref[rows, :]
        qb = qr_ref[rows, :]
        kb = kr_ref[rows, :]
        q_read = qb * d_read
        k_write = kb * d_write
        g_rows = jnp.exp(b[c - 1:c, :])
        g_cols = [jnp.broadcast_to(g_rows[:, s], (GLA_DK, GLA_DK)).T[:, :1] for s in g_sl]
        gops = _gla_operands(qkg_ref[rows, :GK].astype(F32) * (GLA_DK ** -0.5),
                             qkg_ref[rows, GK:].astype(F32), b)

        a_g = [_gla_scores(gops, s, level, c) for s in g_sl]
        a_r = [(_dot_nt(qb[:, s], kb[:, s]) * dintra_ref[h]).astype(BF16) for h, s in enumerate(r_sl)]

        for h in range(GLA_HEADS):
            o = _dot(gops["q_in"][:, g_sl[h]], sg_ref[h].astype(BF16)) + _dot(a_g[h], vg_ref[rows, gv_sl[h]])
            og_ref[rows, gv_sl[h]] = _rms_gate(o, zg_ref[rows, gv_sl[h]])
        for h in range(RET_HEADS):
            o = _dot(a_r[h], vr_ref[rows, rv_sl[h]]) + _dot(q_read[:, r_sl[h]], sr_ref[h].astype(BF16))
            or_ref[rows, rv_sl[h]] = _group_gate(o, zr_ref[rows, rv_sl[h]])

        for h in range(GLA_HEADS):
            sg_ref[h] = g_cols[h] * sg_ref[h] + _dot(gops["k_out"][:, g_sl[h]].T, vg_ref[rows, gv_sl[h]])
        for h in range(RET_HEADS):
            sr_ref[h] = d_chunk[h] * sr_ref[h] + _dot(k_write[:, r_sl[h]].T, vr_ref[rows, rv_sl[h]])
        return carry

    lax.fori_loop(0, MIXER_CHUNKS, chunk, 0)


def _ret_consts(length):
    h = np.arange(RET_HEADS, dtype=np.float64)
    log_gamma = np.log1p(-np.exp2(-5.0 - h))
    idx = np.arange(length, dtype=np.float64)
    diff = idx[:, None] - idx[None, :]
    d_intra = np.where(diff[None] >= 0, np.exp(np.maximum(diff, 0.0)[None] * log_gamma[:, None, None]), 0.0)
    d_read = np.exp((idx + 1.0)[None, :] * log_gamma[:, None])
    d_write = np.exp((length - 1.0 - idx)[None, :] * log_gamma[:, None])
    d_chunk = np.exp(length * log_gamma)
    return (d_intra.astype(np.float32), d_read.astype(np.float32)[..., None],
            d_write.astype(np.float32)[..., None], tuple(float(np.float32(x)) for x in d_chunk))


def _rope_tables(pos):
    half = RET_DK // 2
    inv_freq = ROPE_BASE ** (-jnp.arange(half, dtype=F32) / half)
    ang = pos.astype(F32)[:, None] * inv_freq[None, :]
    return jnp.cos(ang), jnp.sin(ang)


def _mixer_prompt(p, la, batch, seq):
    assert p.dtype == BF16 and seq % (CHUNK * MIXER_CHUNKS) == 0
    d_intra, d_read, d_write, d_chunk = _ret_consts(CHUNK)
    d_read = np.repeat(d_read[..., 0].T, RET_DK, axis=1)
    d_write = np.repeat(d_write[..., 0].T, RET_DK, axis=1)

    rows = CHUNK * MIXER_CHUNKS
    nstep = seq // rows

    def pcol(width, blk):
        return pl.BlockSpec((rows, width), lambda b, t: (b * nstep + t, blk))

    const3 = lambda shape: pl.BlockSpec(shape, lambda b, t: (0, 0, 0))
    tok = lambda width: pl.BlockSpec((rows, width), lambda b, t: (b * nstep + t, 0))
    return pl.pallas_call(
        functools.partial(_mixer_prompt_kernel, d_chunk),
        grid=(batch, nstep),
        in_specs=[pcol(RV, BLK_VR), pcol(RV, BLK_ZR), pcol(RK, BLK_QR), pcol(RK, BLK_KR),
                  pcol(GV, BLK_VG), pcol(GV, BLK_ZG), pcol(2 * GK, BLK_QKG), tok(GK),
                  const3((RET_HEADS, CHUNK, CHUNK)),
                  pl.BlockSpec((CHUNK, RK), lambda b, t: (0, 0)),
                  pl.BlockSpec((CHUNK, RK), lambda b, t: (0, 0))],
        out_specs=[tok(GV), tok(RV),
                   pl.BlockSpec((None, GLA_HEADS, GLA_DK, GLA_DV), lambda b, t: (b, 0, 0, 0)),
                   pl.BlockSpec((None, RET_HEADS, RET_DK, RET_DV), lambda b, t: (b, 0, 0, 0))],
        out_shape=[jax.ShapeDtypeStruct((batch * seq, GV), BF16),
                   jax.ShapeDtypeStruct((batch * seq, RV), BF16),
                   jax.ShapeDtypeStruct((batch, GLA_HEADS, GLA_DK, GLA_DV), F32),
                   jax.ShapeDtypeStruct((batch, RET_HEADS, RET_DK, RET_DV), F32)],
        compiler_params=pltpu.CompilerParams(dimension_semantics=("parallel", "arbitrary"),
                                             vmem_limit_bytes=VMEM_LIMIT),
        name="mixer_prompt",
    )(p, p, p, p, p, p, p, la, jnp.asarray(d_intra), jnp.asarray(d_read), jnp.asarray(d_write))


def _pad_rows(x, rows):
    return jnp.concatenate([x, jnp.zeros((rows - x.shape[0], x.shape[1]), x.dtype)], axis=0)


def _mixer_sample_kernel(seq, gammas, d_chunk, vr_ref, zr_ref, qr_ref, kr_ref, vg_ref, zg_ref, qkg_ref,
                         la_ref, dread_ref, dwrite_ref,
                         sg_in, sr_in, og_ref, or_ref, sg_ref, sr_ref):
    rows = SAMPLE_ROWS
    n_el = rows // seq
    row1 = lax.broadcasted_iota(jnp.int32, (rows, 1), 0)
    pos1 = row1 % seq
    tcol = lax.broadcasted_iota(jnp.int32, (LANES, LANES), 1)

    def seg_cumsum(x):
        s = 1
        while s < seq:
            x = x + jnp.where(pos1 >= s, pltpu.roll(x, s, 0), 0.0)
            s *= 2
        return x

    def intra(q, k, v, gamma):
        out = jnp.zeros_like(v)
        for d in range(seq):
            kd = k if d == 0 else pltpu.roll(k, d, 0)
            vd = v if d == 0 else pltpu.roll(v, d, 0)
            a = jnp.sum(q * kd, axis=1, keepdims=True) * (gamma ** d)
            out = out + jnp.where(pos1 >= d, a, 0.0) * vd
        return out

    def inter(q, states):
        qb = q.astype(BF16)
        out = None
        for e in range(n_el):
            oe = _dot(qb, states[e].astype(BF16))
            out = oe if out is None else jnp.where(row1 >= e * seq, oe, out)
        return out

    for h in range(GLA_HEADS):
        kq = slice(h * GLA_DK, (h + 1) * GLA_DK)
        kk = slice(GK + h * GLA_DK, GK + (h + 1) * GLA_DK)
        vv = slice(h * GLA_DV, (h + 1) * GLA_DV)
        q = qkg_ref[:, kq] * (GLA_DK ** -0.5)
        k = qkg_ref[:, kk]
        v = vg_ref[:, vv]
        la = la_ref[:, kq]
        b = seg_cumsum(la)
        q_dec = q * jnp.exp(b)
        k_dec = k * jnp.exp(-b)
        states = [sg_in[e, h] for e in range(n_el)]
        o = inter(q_dec, states) + intra(q_dec, k_dec, v, 1.0)
        og_ref[:, vv] = _rms_gate(o, zg_ref[:, vv])
        b_last = jnp.zeros_like(b)
        for e in range(n_el):
            r = e * seq + seq - 1
            b_last = jnp.where((row1 >= e * seq) & (row1 < (e + 1) * seq),
                               jnp.broadcast_to(b[r:r + 1, :], b.shape), b_last)
        k_out = k * jnp.exp(b_last - b)
        k_outT = _pad_rows(k_out, LANES).T
        laT = _pad_rows(la, LANES).T
        vb = _pad_rows(v, LANES).astype(BF16)
        tl = tcol[:GLA_DK]
        for e in range(n_el):
            sel = (tl >= e * seq) & (tl < (e + 1) * seq)
            g_col = jnp.exp(jnp.sum(jnp.where(sel, laT, 0.0), axis=1, keepdims=True))
            sg_ref[e, h] = g_col * states[e] + _dot(jnp.where(sel, k_outT, 0.0).astype(BF16), vb)

    for h in range(RET_HEADS):
        kq = slice(h * RET_DK, (h + 1) * RET_DK)
        vv = slice(h * RET_DV, (h + 1) * RET_DV)
        q = qr_ref[:, kq]
        k = kr_ref[:, kq]
        v = vr_ref[:, vv]
        states = [sr_in[e, h] for e in range(n_el)]
        o = inter(q, states) * dread_ref[h] + intra(q, k, v, gammas[h])
        or_ref[:, vv] = _group_gate(o, zr_ref[:, vv])
        kwT = _pad_rows(k * dwrite_ref[h], LANES).T
        vb = _pad_rows(v, LANES).astype(BF16)
        tl = jnp.concatenate([tcol, tcol], axis=0)
        for e in range(n_el):
            sel = (tl >= e * seq) & (tl < (e + 1) * seq)
            sr_ref[e, h] = d_chunk[h] * states[e] + _dot(jnp.where(sel, kwT, 0.0).astype(BF16), vb)


def _mixer_sample(p, la, state_gla, state_ret, batch, seq):
    rows = SAMPLE_ROWS
    n_el = rows // seq
    _, d_read, d_write, d_chunk = _ret_consts(seq)
    gammas = tuple(float(1.0 - 2.0 ** (-5.0 - h)) for h in range(RET_HEADS))
    d_read = np.tile(d_read, (1, n_el, 1))
    d_write = np.tile(d_write, (1, n_el, 1))

    pcol = lambda width, blk: pl.BlockSpec((rows, width), lambda i: (i, blk))
    tok = lambda width: pl.BlockSpec((rows, width), lambda i: (i, 0))
    full3 = lambda shape: pl.BlockSpec(shape, lambda i: (0, 0, 0))
    sg_spec = pl.BlockSpec((n_el, GLA_HEADS, GLA_DK, GLA_DV), lambda i: (i, 0, 0, 0))
    sr_spec = pl.BlockSpec((n_el, RET_HEADS, RET_DK, RET_DV), lambda i: (i, 0, 0, 0))
    return pl.pallas_call(
        functools.partial(_mixer_sample_kernel, seq, gammas, d_chunk),
        grid=(batch * seq // rows,),
        in_specs=[pcol(RV, BLK_VR), pcol(RV, BLK_ZR), pcol(RK, BLK_QR), pcol(RK, BLK_KR),
                  pcol(GV, BLK_VG), pcol(GV, BLK_ZG), pcol(2 * GK, BLK_QKG), tok(GK),
                  full3((RET_HEADS, rows, 1)), full3((RET_HEADS, rows, 1)),
                  sg_spec, sr_spec],
        out_specs=[tok(GV), tok(RV), sg_spec, sr_spec],
        out_shape=[jax.ShapeDtypeStruct((batch * seq, GV), F32),
                   jax.ShapeDtypeStruct((batch * seq, RV), F32),
                   jax.ShapeDtypeStruct(state_gla.shape, F32),
                   jax.ShapeDtypeStruct(state_ret.shape, F32)],
        compiler_params=pltpu.CompilerParams(dimension_semantics=("parallel",),
                                             vmem_limit_bytes=VMEM_LIMIT),
        name="mixer_sample",
    )(p, p, p, p, p, p, p, la, jnp.asarray(d_read), jnp.asarray(d_write), state_gla, state_ret)


def _out_kernel(og_ref, or_ref, mg_ref, mr_ref, x_ref, gate_ref, wbg_ref, wbr_ref, wo_ref, lng_ref, lnb_ref,
                y_ref):
    sub = min(OUT_SUB, y_ref.shape[0])
    blocks = [slice(r0, r0 + sub) for r0 in range(0, y_ref.shape[0], sub)]
    merged = []
    for rs in blocks:
        bg = _dot(og_ref[rs, :].astype(BF16), wbg_ref[...])
        br = _dot(or_ref[rs, :].astype(BF16), wbr_ref[...])
        merged.append((mg_ref[rs, :].astype(F32) * bg + mr_ref[rs, :].astype(F32) * br).astype(BF16))
    for rs, m in zip(blocks, merged):
        out = _dot(m, wo_ref[...])
        gate = gate_ref[...] if gate_ref.shape[0] == 1 else gate_ref[rs, :]
        r = DEEPNORM_ALPHA * x_ref[rs, :] + gate * out
        mu = jnp.mean(r, axis=-1, keepdims=True)
        c = r - mu
        var = jnp.mean(c * c, axis=-1, keepdims=True)
        y_ref[rs, :] = c * lax.rsqrt(var + LN_EPS) * lng_ref[...] + lnb_ref[...]


def _out_stage(og, orr, p, x2d, gate3, wbg, wbr, wo, ln_g, ln_b, *, tm, rows_per_mod):
    n = x2d.shape[0]
    if gate3.shape[1] != 1:
        gate_spec = pl.BlockSpec((None, tm, D_MODEL), lambda i: (0, i, 0))
    else:
        tiles_per_mod = rows_per_mod // tm
        gate_spec = pl.BlockSpec((None, 1, D_MODEL), lambda i: (i // tiles_per_mod, 0, 0))
    full = lambda shape: pl.BlockSpec(shape, lambda i: (0, 0), pipeline_mode=pl.Buffered(1))
    return pl.pallas_call(
        _out_kernel,
        grid=(n // tm,),
        in_specs=[pl.BlockSpec((tm, GV), lambda i: (i, 0)),
                  pl.BlockSpec((tm, RV), lambda i: (i, 0)),
                  pl.BlockSpec((tm, D_MODEL), lambda i: (i, BLK_MG)),
                  pl.BlockSpec((tm, D_MODEL), lambda i: (i, BLK_MR)),
                  pl.BlockSpec((tm, D_MODEL), lambda i: (i, 0)),
                  gate_spec,
                  full((GV, D_MODEL)), full((RV, D_MODEL)), full((D_MODEL, D_MODEL)),
                  full((1, D_MODEL)), full((1, D_MODEL))],
        out_specs=pl.BlockSpec((tm, D_MODEL), lambda i: (i, 0)),
        out_shape=jax.ShapeDtypeStruct((n, D_MODEL), F32),
        compiler_params=pltpu.CompilerParams(dimension_semantics=("parallel",),
                                             vmem_limit_bytes=VMEM_LIMIT),
        name="out_stage",
    )(og, orr, p, p, x2d, gate3, wbg, wbr, wo, ln_g, ln_b)


_W_SRC_BLK = (_O_VR // PROJ_TN, _O_VR // PROJ_TN + 1, _O_QR // PROJ_TN, _O_KR // PROJ_TN, _O_VG // PROJ_TN,
              _O_QG // PROJ_TN, _O_ZR // PROJ_TN, _O_ZR // PROJ_TN + 1, _O_ZG // PROJ_TN, _O_MG // PROJ_TN,
              _O_MR // PROJ_TN)
_LR_BLK = _O_LR // PROJ_TN


def _prep_w_kernel(src_ref, a_ref, b_ref, o_ref, lr_ref):
    blk = src_ref[pl.program_id(0)]
    keep = PROJ_TN - GLA_LOWRANK

    @pl.when(blk >= _LR_BLK)
    def _():
        o_ref[:keep, :] = a_ref[GLA_LOWRANK:, :].astype(BF16)
        o_ref[keep:, :] = b_ref[...].astype(BF16)

    @pl.when(blk < _LR_BLK)
    def _():
        o_ref[...] = a_ref[...].astype(BF16)

    @pl.when(blk == _LR_BLK)
    def _():
        lr_ref[:GLA_LOWRANK, :] = a_ref[:GLA_LOWRANK, :].astype(BF16)
        lr_ref[GLA_LOWRANK:, :] = jnp.zeros((LR_PAD - GLA_LOWRANK, D_MODEL), BF16)


def _prep_w_in(wt):
    assert all(o % PROJ_TN in (0, GLA_LOWRANK) for o in (_O_VR, _O_QR, _O_KR, _O_VG, _O_QG, _O_ZR, _O_ZG,
                                                         _O_MG, _O_MR)) and _O_LR % PROJ_TN == 0
    src = jnp.asarray(_W_SRC_BLK, jnp.int32)
    per_blk = PROJ_TN // GLA_LOWRANK
    return pl.pallas_call(
        _prep_w_kernel,
        grid_spec=pltpu.PrefetchScalarGridSpec(
            num_scalar_prefetch=1,
            grid=(len(_W_SRC_BLK),),
            in_specs=[pl.BlockSpec((PROJ_TN, D_MODEL), lambda j, src: (src[j], 0)),
                      pl.BlockSpec((GLA_LOWRANK, D_MODEL), lambda j, src: ((src[j] + 1) * per_blk, 0))],
            out_specs=[pl.BlockSpec((PROJ_TN, D_MODEL), lambda j, src: (j, 0)),
                       pl.BlockSpec((LR_PAD, D_MODEL), lambda j, src: (0, 0))]),
        out_shape=[jax.ShapeDtypeStruct((P_COLS, D_MODEL), BF16),
                   jax.ShapeDtypeStruct((LR_PAD, D_MODEL), BF16)],
        compiler_params=pltpu.CompilerParams(dimension_semantics=("arbitrary",),
                                             vmem_limit_bytes=VMEM_LIMIT),
        name="prep_w",
    )(src, wt, wt)


def kernel(x_prompt, x_sample, state_gla, state_ret, c_prompt, c_sample, w_ada, b_ada, w_in, w_lr2, b_lr2,
           gla_norm_g, ret_norm_g, w_branch_gla, w_branch_ret, w_out, ln_g, ln_b):
    assert w_ada.shape[0] == 1, "one layer"
    bp, tp, _ = x_prompt.shape
    bs, ts, _ = x_sample.shape

    mod = _ada(jnp.concatenate([c_prompt, c_sample], axis=0), w_ada[0], b_ada)
    shift, scale1, gate = mod[:, :D_MODEL], mod[:, D_MODEL:2 * D_MODEL], mod[:, 2 * D_MODEL:]
    per_group = lambda a: a[:bp][:, None, :]
    per_token = lambda a: jnp.repeat(a[bp:], ts, axis=0)[None]

    w_main, w_lr = _prep_w_in(w_in[0].T)
    w_lr2p = jnp.pad(w_lr2[0], ((0, LR_PAD - GLA_LOWRANK), (0, 0))).astype(BF16)
    wbg = (gla_norm_g[0][:, None] * w_branch_gla[0]).astype(BF16)
    wbr = (ret_norm_g[0][:, None] * w_branch_ret[0]).astype(BF16)
    wo = w_out[0].astype(BF16)

    xp = x_prompt.reshape(bp * tp, D_MODEL)
    cos_p, sin_p = _rope_tables(jnp.arange(tp, dtype=jnp.int32))
    p_p, la_p = _proj(xp, per_group(scale1), per_group(shift), w_main, w_lr, w_lr2p, b_lr2, cos_p, sin_p,
                      tm=2048, rows_per_mod=tp, p_dtype=BF16, cum_chunk=CHUNK)
    og_p, or_p, sg_p, sr_p = _mixer_prompt(p_p, la_p, bp, tp)
    y_p = _out_stage(og_p, or_p, p_p, xp, per_group(gate), wbg, wbr, wo, ln_g, ln_b, tm=1024, rows_per_mod=tp)

    xs = x_sample.reshape(bs * ts, D_MODEL)
    cos_s, sin_s = _rope_tables(PAST_LEN + jnp.arange(bs * ts, dtype=jnp.int32) % ts)
    p_s, la_s = _proj(xs, per_token(scale1), per_token(shift), w_main, w_lr, w_lr2p, b_lr2, cos_s, sin_s,
                      tm=bs * ts, rows_per_mod=ts, p_dtype=F32)
    og_s, or_s, sg_s, sr_s = _mixer_sample(p_s, la_s, state_gla[0], state_ret[0], bs, ts)
    y_s = _out_stage(og_s, or_s, p_s, xs, per_token(gate), wbg, wbr, wo, ln_g, ln_b,
                     tm=bs * ts, rows_per_mod=ts)

    return (y_p.reshape(bp, tp, D_MODEL), y_s.reshape(bs, ts, D_MODEL),
            sg_p[None], sr_p[None], sg_s[None], sr_s[None])
```

```python
import functools

import numpy as np
import jax
import jax.numpy as jnp
from jax import lax
from jax.experimental import pallas as pl
from jax.experimental.pallas import tpu as pltpu

F32 = jnp.float32
BF16 = jnp.bfloat16

D_MODEL = 1024
PAST_LEN = 16384
GLA_HEADS, GLA_DK, GLA_DV = 4, 128, 256
RET_HEADS, RET_DK, RET_DV = 4, 256, 512
GK, GV = GLA_HEADS * GLA_DK, GLA_HEADS * GLA_DV
RK, RV = RET_HEADS * RET_DK, RET_HEADS * RET_DV
GLA_LOWRANK = 16
GLA_TAU = 16.0
GLA_SUB = 16
ROPE_BASE = 10000.0
DEEPNORM_ALPHA = 2.0 ** 0.25
LN_EPS = 1e-5
HEAD_NORM_EPS = 1e-5

CHUNK = 128
MIXER_CHUNKS = 4
SAMPLE_ROWS = 16
LANES = 128
LR_PAD = 256

PROJ_TN = 1024
PROJ_SUB = 1024
OUT_SUB = 256
P_COLS = 11 * PROJ_TN
BLK_VR, BLK_ZR = 0, 3
BLK_QR, BLK_KR, BLK_VG, BLK_QKG, BLK_ZG, BLK_MG, BLK_MR = 2, 3, 4, 5, 8, 9, 10
FIRST_SILU_BLK, FIRST_SIGMOID_BLK = 6, 9
_O_QG, _O_KG, _O_VG, _O_ZG, _O_LR = 0, GK, 2 * GK, 2 * GK + GV, 2 * GK + 2 * GV
_O_QR = _O_LR + GLA_LOWRANK
_O_KR = _O_QR + RK
_O_VR = _O_KR + RK
_O_ZR = _O_VR + RV
_O_MG = _O_ZR + RV
_O_MR = _O_MG + D_MODEL
_O_END = _O_MR + D_MODEL

VMEM_LIMIT = 56 * 1024 * 1024


def _dot(a, b):
    return jnp.dot(a, b, preferred_element_type=F32)


def _dot_nt(a, b):
    return lax.dot_general(a, b, (((1,), (1,)), ((), ())), preferred_element_type=F32)


def _log_sigmoid(x):
    return jnp.minimum(x, 0.0) - jnp.log(1.0 + jnp.exp(-jnp.abs(x)))


def _sigmoid(x):
    return 0.5 * jnp.tanh(0.5 * x) + 0.5


def _silu(x):
    return x * _sigmoid(x)


def _ada_kernel(c_ref, w_ref, b_ref, o_ref):
    one = jnp.where(pl.program_id(0) == 1, 1.0, 0.0)
    o_ref[...] = _dot(c_ref[...].astype(BF16), w_ref[...].astype(BF16)) + (b_ref[...] + one)


def _ada(c, w_ada, b_ada):
    n = c.shape[0]
    return pl.pallas_call(
        _ada_kernel,
        grid=(3,),
        in_specs=[pl.BlockSpec((n, D_MODEL), lambda j: (0, 0)),
                  pl.BlockSpec((D_MODEL, D_MODEL), lambda j: (0, j)),
                  pl.BlockSpec((1, D_MODEL), lambda j: (0, j))],
        out_specs=pl.BlockSpec((n, D_MODEL), lambda j: (0, j)),
        out_shape=jax.ShapeDtypeStruct((n, 3 * D_MODEL), F32),
        compiler_params=pltpu.CompilerParams(dimension_semantics=("arbitrary",)),
        name="ada",
    )(c, w_ada, b_ada)


def _proj_kernel(cum_chunk, x_ref, scale1_ref, shift_ref, w_ref, wlr_ref, wlr2_ref, blr2_ref, cos_ref, sin_ref,
                 p_ref, la_ref, h_ref):
    j = pl.program_id(1)

    @pl.when(j == 0)
    def _():
        h = (x_ref[...] * scale1_ref[...] + shift_ref[...]).astype(BF16)
        h_ref[...] = h
        sub = min(PROJ_SUB, h.shape[0])
        lr = jnp.concatenate([_dot_nt(h[r:r + sub, :], wlr_ref[...]) for r in range(0, h.shape[0], sub)], axis=0)
        pre = _dot(lr.astype(BF16), wlr2_ref[...]) + blr2_ref[...]
        la = _log_sigmoid(pre) * (1.0 / GLA_TAU)
        if cum_chunk:
            for r in range(0, la.shape[0], cum_chunk):
                la_ref[r:r + cum_chunk, :] = _cumsum_rows_mxu(la[r:r + cum_chunk, :])
        else:
            la_ref[...] = la

    def emit(epilogue):
        sub = min(PROJ_SUB, p_ref.shape[0])
        for r in range(0, p_ref.shape[0], sub):
            rs = slice(r, r + sub)
            p_ref[rs, :] = epilogue(_dot(h_ref[rs, :], w_ref[...]), rs).astype(p_ref.dtype)

    def rotate(scale):
        return lambda a, rs: _rotary_heads(a, cos_ref[rs, :], sin_ref[rs, :], RET_HEADS, scale)

    plain = (j < FIRST_SILU_BLK) & (j != BLK_QR) & (j != BLK_KR)
    pl.when(plain)(lambda: emit(lambda a, rs: a))
    pl.when(j == BLK_QR)(lambda: emit(rotate(1.0)))
    pl.when(j == BLK_KR)(lambda: emit(rotate(RET_DK ** -0.5)))
    pl.when((j >= FIRST_SILU_BLK) & (j < FIRST_SIGMOID_BLK))(lambda: emit(lambda a, rs: _silu(a)))
    pl.when(j >= FIRST_SIGMOID_BLK)(lambda: emit(lambda a, rs: _sigmoid(a)))


def _proj(x2d, scale1_3, shift3, w_main, w_lr, w_lr2, b_lr2, cos, sin, *, tm, rows_per_mod, p_dtype, cum_chunk=0):
    assert cum_chunk == 0 or (tm % cum_chunk == 0 and rows_per_mod % cum_chunk == 0)
    n = x2d.shape[0]
    rope_tiles = cos.shape[0] // tm
    rope_spec = pl.BlockSpec((tm, RET_DK // 2), lambda i, j: (i % rope_tiles, 0))
    per_row = shift3.shape[1] != 1
    if per_row:
        mod_spec = pl.BlockSpec((None, tm, D_MODEL), lambda i, j: (0, i, 0))
    else:
        tiles_per_mod = rows_per_mod // tm
        mod_spec = pl.BlockSpec((None, 1, D_MODEL), lambda i, j: (i // tiles_per_mod, 0, 0))
    tn = PROJ_TN
    return pl.pallas_call(
        functools.partial(_proj_kernel, cum_chunk),
        grid=(n // tm, P_COLS // tn),
        in_specs=[pl.BlockSpec((tm, D_MODEL), lambda i, j: (i, 0)),
                  mod_spec, mod_spec,
                  pl.BlockSpec((D_MODEL, tn), lambda i, j: (0, j)),
                  pl.BlockSpec((LR_PAD, D_MODEL), lambda i, j: (0, 0)),
                  pl.BlockSpec((LR_PAD, GK), lambda i, j: (0, 0)),
                  pl.BlockSpec((1, GK), lambda i, j: (0, 0)),
                  rope_spec, rope_spec],
        out_specs=[pl.BlockSpec((tm, tn), lambda i, j: (i, j)),
                   pl.BlockSpec((tm, GK), lambda i, j: (i, 0))],
        out_shape=[jax.ShapeDtypeStruct((n, P_COLS), p_dtype),
                   jax.ShapeDtypeStruct((n, GK), F32)],
        scratch_shapes=[pltpu.VMEM((tm, D_MODEL), BF16)],
        compiler_params=pltpu.CompilerParams(dimension_semantics=("parallel", "arbitrary"),
                                             vmem_limit_bytes=VMEM_LIMIT),
        name="proj",
    )(x2d, scale1_3, shift3, w_main, w_lr, w_lr2, b_lr2, cos, sin)


def _rms_gate(o, sz):
    o = o * lax.rsqrt(jnp.mean(o * o, axis=-1, keepdims=True) + HEAD_NORM_EPS)
    return o.astype(sz.dtype) * sz


def _group_gate(o, sz):
    mu = jnp.mean(o, axis=-1, keepdims=True)
    c = o - mu
    var = jnp.mean(c * c, axis=-1, keepdims=True)
    return (c * lax.rsqrt(var + HEAD_NORM_EPS)).astype(sz.dtype) * sz


def _cumsum_rows_mxu(x):
    n = x.shape[0]
    row = lax.broadcasted_iota(jnp.int32, (n, n), 0)
    col = lax.broadcasted_iota(jnp.int32, (n, n), 1)
    tri = jnp.where(col <= row, 1.0, 0.0).astype(BF16)
    hi = x.astype(BF16)
    rest = x - hi.astype(F32)
    mid = rest.astype(BF16)
    lo = (rest - mid.astype(F32)).astype(BF16)
    return _dot(jnp.concatenate([tri, tri, tri], axis=1), jnp.concatenate([hi, mid, lo], axis=0))


def _block_edge(b, w, prev):
    n, m = b.shape
    pieces = []
    for i in range(n // w):
        r = i * w - 1 if prev else i * w + w - 1
        if r < 0:
            pieces.append(jnp.zeros((w, m), F32))
        else:
            pieces.append(jnp.broadcast_to(b[r:r + 1, :], (w, m)))
    return jnp.concatenate(pieces, axis=0)


def _gla_levels(c):
    w = GLA_SUB
    while w < c:
        yield w
        w *= 2


def _gla_operands(q, k, b):
    c = q.shape[0]
    btot = b[c - 1:c, :]
    rel = b - _block_edge(b, GLA_SUB, True)
    ops = {"q_in": (q * jnp.exp(b)).astype(BF16),
           "k_out": (k * jnp.exp(btot - b)).astype(BF16),
           "qd": (q * jnp.exp(rel)).astype(BF16),
           "kd": (k * jnp.exp(-rel)).astype(BF16)}
    for w in _gla_levels(c):
        if w > GLA_SUB:
            ops["q%d" % w] = (q * jnp.exp(b - _block_edge(b, w, True))).astype(BF16)
        ops["k%d" % w] = (k * jnp.exp(_block_edge(b, w, False) - b)).astype(BF16)
    return ops


def _gla_level_map(c):
    row = lax.broadcasted_iota(jnp.int32, (c, c), 0)
    col = lax.broadcasted_iota(jnp.int32, (c, c), 1)
    sh = GLA_SUB.bit_length() - 1
    level = jnp.where(((row >> sh) == (col >> sh)) & (col <= row), GLA_SUB - 1, 0)
    for w in _gla_levels(c):
        sh = w.bit_length() - 1
        rb, cb = row >> sh, col >> sh
        level = jnp.where(((rb & 1) == 1) & (cb == rb - 1), w, level)
    return level


def _gla_scores(ops, hs, level, c):
    a = jnp.where(level == GLA_SUB - 1, _dot_nt(ops["qd"][:, hs], ops["kd"][:, hs]), 0.0)
    for w in _gla_levels(c):
        qw = ops["qd"] if w == GLA_SUB else ops["q%d" % w]
        a = jnp.where(level == w, _dot_nt(qw[:, hs], ops["k%d" % w][:, hs]), a)
    return a.astype(BF16)


def _rotary_heads(x, cos, sin, heads, scale=1.0):
    half = cos.shape[1]
    if scale != 1.0:
        cos, sin = cos * scale, sin * scale
    cos_t = jnp.concatenate([cos] * (2 * heads), axis=1)
    sin_t = jnp.concatenate([-sin, sin] * heads, axis=1)
    partner = [i + 1 - 2 * (i % 2) for i in range(2 * heads)]
    swapped = jnp.concatenate([x[:, j * half:(j + 1) * half] for j in partner], axis=1)
    return x * cos_t + swapped * sin_t


def _mixer_prompt_kernel(d_chunk, vr_ref, zr_ref, qr_ref, kr_ref, vg_ref, zg_ref, qkg_ref, b_ref,
                         dintra_ref, dread_ref, dwrite_ref,
                         og_ref, or_ref, sg_ref, sr_ref):
    @pl.when(pl.program_id(1) == 0)
    def _():
        sg_ref[...] = jnp.zeros_like(sg_ref)
        sr_ref[...] = jnp.zeros_like(sr_ref)

    c = CHUNK
    g_sl = [slice(h * GLA_DK, (h + 1) * GLA_DK) for h in range(GLA_HEADS)]
    gv_sl = [slice(h * GLA_DV, (h + 1) * GLA_DV) for h in range(GLA_HEADS)]
    r_sl = [slice(h * RET_DK, (h + 1) * RET_DK) for h in range(RET_HEADS)]
    rv_sl = [slice(h * RET_DV, (h + 1) * RET_DV) for h in range(RET_HEADS)]

    level = _gla_level_map(c)
    d_read = dread_ref[...].astype(BF16)
    d_write = dwrite_ref[...].astype(BF16)

    def chunk(i, carry):
        rows = pl.ds(pl.multiple_of(i * c, c), c)
        b = b_ref[rows, :]
        qb = qr_ref[rows, :]
        kb = kr_ref[rows, :]
        q_read = qb * d_read
        k_write = kb * d_write
        g_rows = jnp.exp(b[c - 1:c, :])
        g_cols = [jnp.broadcast_to(g_rows[:, s], (GLA_DK, GLA_DK)).T[:, :1] for s in g_sl]
        gops = _gla_operands(qkg_ref[rows, :GK].astype(F32) * (GLA_DK ** -0.5),
                             qkg_ref[rows, GK:].astype(F32), b)

        a_g = [_gla_scores(gops, s, level, c) for s in g_sl]
        a_r = [(_dot_nt(qb[:, s], kb[:, s]) * dintra_ref[h]).astype(BF16) for h, s in enumerate(r_sl)]

        for h in range(GLA_HEADS):
            o = _dot(gops["q_in"][:, g_sl[h]], sg_ref[h].astype(BF16)) + _dot(a_g[h], vg_ref[rows, gv_sl[h]])
            og_ref[rows, gv_sl[h]] = _rms_gate(o, zg_ref[rows, gv_sl[h]])
        for h in range(RET_HEADS):
            o = _dot(a_r[h], vr_ref[rows, rv_sl[h]]) + _dot(q_read[:, r_sl[h]], sr_ref[h].astype(BF16))
            or_ref[rows, rv_sl[h]] = _group_gate(o, zr_ref[rows, rv_sl[h]])

        for h in range(GLA_HEADS):
            sg_ref[h] = g_cols[h] * sg_ref[h] + _dot(gops["k_out"][:, g_sl[h]].T, vg_ref[rows, gv_sl[h]])
        for h in range(RET_HEADS):
            sr_ref[h] = d_chunk[h] * sr_ref[h] + _dot(k_write[:, r_sl[h]].T, vr_ref[rows, rv_sl[h]])
        return carry

    lax.fori_loop(0, MIXER_CHUNKS, chunk, 0)


def _ret_consts(length):
    h = np.arange(RET_HEADS, dtype=np.float64)
    log_gamma = np.log1p(-np.exp2(-5.0 - h))
    idx = np.arange(length, dtype=np.float64)
    diff = idx[:, None] - idx[None, :]
    d_intra = np.where(diff[None] >= 0, np.exp(np.maximum(diff, 0.0)[None] * log_gamma[:, None, None]), 0.0)
    d_read = np.exp((idx + 1.0)[None, :] * log_gamma[:, None])
    d_write = np.exp((length - 1.0 - idx)[None, :] * log_gamma[:, None])
    d_chunk = np.exp(length * log_gamma)
    return (d_intra.astype(np.float32), d_read.astype(np.float32)[..., None],
            d_write.astype(np.float32)[..., None], tuple(float(np.float32(x)) for x in d_chunk))


def _rope_tables(pos):
    half = RET_DK // 2
    inv_freq = ROPE_BASE ** (-jnp.arange(half, dtype=F32) / half)
    ang = pos.astype(F32)[:, None] * inv_freq[None, :]
    return jnp.cos(ang), jnp.sin(ang)


def _mixer_prompt(p, la, batch, seq):
    assert p.dtype == BF16 and seq % (CHUNK * MIXER_CHUNKS) == 0
    d_intra, d_read, d_write, d_chunk = _ret_consts(CHUNK)
    d_read = np.repeat(d_read[..., 0].T, RET_DK, axis=1)
    d_write = np.repeat(d_write[..., 0].T, RET_DK, axis=1)

    rows = CHUNK * MIXER_CHUNKS
    nstep = seq // rows

    def pcol(width, blk):
        return pl.BlockSpec((rows, width), lambda b, t: (b * nstep + t, blk))

    const3 = lambda shape: pl.BlockSpec(shape, lambda b, t: (0, 0, 0))
    tok = lambda width: pl.BlockSpec((rows, width), lambda b, t: (b * nstep + t, 0))
    return pl.pallas_call(
        functools.partial(_mixer_prompt_kernel, d_chunk),
        grid=(batch, nstep),
        in_specs=[pcol(RV, BLK_VR), pcol(RV, BLK_ZR), pcol(RK, BLK_QR), pcol(RK, BLK_KR),
                  pcol(GV, BLK_VG), pcol(GV, BLK_ZG), pcol(2 * GK, BLK_QKG), tok(GK),
                  const3((RET_HEADS, CHUNK, CHUNK)),
                  pl.BlockSpec((CHUNK, RK), lambda b, t: (0, 0)),
                  pl.BlockSpec((CHUNK, RK), lambda b, t: (0, 0))],
        out_specs=[tok(GV), tok(RV),
                   pl.BlockSpec((None, GLA_HEADS, GLA_DK, GLA_DV), lambda b, t: (b, 0, 0, 0)),
                   pl.BlockSpec((None, RET_HEADS, RET_DK, RET_DV), lambda b, t: (b, 0, 0, 0))],
        out_shape=[jax.ShapeDtypeStruct((batch * seq, GV), BF16),
                   jax.ShapeDtypeStruct((batch * seq, RV), BF16),
                   jax.ShapeDtypeStruct((batch, GLA_HEADS, GLA_DK, GLA_DV), F32),
                   jax.ShapeDtypeStruct((batch, RET_HEADS, RET_DK, RET_DV), F32)],
        compiler_params=pltpu.CompilerParams(dimension_semantics=("parallel", "arbitrary"),
                                             vmem_limit_bytes=VMEM_LIMIT),
        name="mixer_prompt",
    )(p, p, p, p, p, p, p, la, jnp.asarray(d_intra), jnp.asarray(d_read), jnp.asarray(d_write))


def _pad_rows(x, rows):
    return jnp.concatenate([x, jnp.zeros((rows - x.shape[0], x.shape[1]), x.dtype)], axis=0)


def _mixer_sample_kernel(seq, gammas, d_chunk, vr_ref, zr_ref, qr_ref, kr_ref, vg_ref, zg_ref, qkg_ref,
                         la_ref, dread_ref, dwrite_ref,
                         sg_in, sr_in, og_ref, or_ref, sg_ref, sr_ref):
    rows = SAMPLE_ROWS
    n_el = rows // seq
    row1 = lax.broadcasted_iota(jnp.int32, (rows, 1), 0)
    pos1 = row1 % seq
    tcol = lax.broadcasted_iota(jnp.int32, (LANES, LANES), 1)

    def seg_cumsum(x):
        s = 1
        while s < seq:
            x = x + jnp.where(pos1 >= s, pltpu.roll(x, s, 0), 0.0)
            s *= 2
        return x

    def intra(q, k, v, gamma):
        out = jnp.zeros_like(v)
        for d in range(seq):
            kd = k if d == 0 else pltpu.roll(k, d, 0)
            vd = v if d == 0 else pltpu.roll(v, d, 0)
            a = jnp.sum(q * kd, axis=1, keepdims=True) * (gamma ** d)
            out = out + jnp.where(pos1 >= d, a, 0.0) * vd
        return out

    def inter(q, states):
        qb = q.astype(BF16)
        out = None
        for e in range(n_el):
            oe = _dot(qb, states[e].astype(BF16))
            out = oe if out is None else jnp.where(row1 >= e * seq, oe, out)
        return out

    for h in range(GLA_HEADS):
        kq = slice(h * GLA_DK, (h + 1) * GLA_DK)
        kk = slice(GK + h * GLA_DK, GK + (h + 1) * GLA_DK)
        vv = slice(h * GLA_DV, (h + 1) * GLA_DV)
        q = qkg_ref[:, kq] * (GLA_DK ** -0.5)
        k = qkg_ref[:, kk]
        v = vg_ref[:, vv]
        la = la_ref[:, kq]
        b = seg_cumsum(la)
        q_dec = q * jnp.exp(b)
        k_dec = k * jnp.exp(-b)
        states = [sg_in[e, h] for e in range(n_el)]
        o = inter(q_dec, states) + intra(q_dec, k_dec, v, 1.0)
        og_ref[:, vv] = _rms_gate(o, zg_ref[:, vv])
        b_last = jnp.zeros_like(b)
        for e in range(n_el):
            r = e * seq + seq - 1
            b_last = jnp.where((row1 >= e * seq) & (row1 < (e + 1) * seq),
                               jnp.broadcast_to(b[r:r + 1, :], b.shape), b_last)
        k_out = k * jnp.exp(b_last - b)
        k_outT = _pad_rows(k_out, LANES).T
        laT = _pad_rows(la, LANES).T
        vb = _pad_rows(v, LANES).astype(BF16)
        tl = tcol[:GLA_DK]
        for e in range(n_el):
            sel = (tl >= e * seq) & (tl < (e + 1) * seq)
            g_col = jnp.exp(jnp.sum(jnp.where(sel, laT, 0.0), axis=1, keepdims=True))
            sg_ref[e, h] = g_col * states[e] + _dot(jnp.where(sel, k_outT, 0.0).astype(BF16), vb)

    for h in range(RET_HEADS):
        kq = slice(h * RET_DK, (h + 1) * RET_DK)
        vv = slice(h * RET_DV, (h + 1) * RET_DV)
        q = qr_ref[:, kq]
        k = kr_ref[:, kq]
        v = vr_ref[:, vv]
        states = [sr_in[e, h] for e in range(n_el)]
        o = inter(q, states) * dread_ref[h] + intra(q, k, v, gammas[h])
        or_ref[:, vv] = _group_gate(o, zr_ref[:, vv])
        kwT = _pad_rows(k * dwrite_ref[h], LANES).T
        vb = _pad_rows(v, LANES).astype(BF16)
        tl = jnp.concatenate([tcol, tcol], axis=0)
        for e in range(n_el):
            sel = (tl >= e * seq) & (tl < (e + 1) * seq)
            sr_ref[e, h] = d_chunk[h] * states[e] + _dot(jnp.where(sel, kwT, 0.0).astype(BF16), vb)


def _mixer_sample(p, la, state_gla, state_ret, batch, seq):
    rows = SAMPLE_ROWS
    n_el = rows // seq
    _, d_read, d_write, d_chunk = _ret_consts(seq)
    gammas = tuple(float(1.0 - 2.0 ** (-5.0 - h)) for h in range(RET_HEADS))
    d_read = np.tile(d_read, (1, n_el, 1))
    d_write = np.tile(d_write, (1, n_el, 1))

    pcol = lambda width, blk: pl.BlockSpec((rows, width), lambda i: (i, blk))
    tok = lambda width: pl.BlockSpec((rows, width), lambda i: (i, 0))
    full3 = lambda shape: pl.BlockSpec(shape, lambda i: (0, 0, 0))
    sg_spec = pl.BlockSpec((n_el, GLA_HEADS, GLA_DK, GLA_DV), lambda i: (i, 0, 0, 0))
    sr_spec = pl.BlockSpec((n_el, RET_HEADS, RET_DK, RET_DV), lambda i: (i, 0, 0, 0))
    return pl.pallas_call(
        functools.partial(_mixer_sample_kernel, seq, gammas, d_chunk),
        grid=(batch * seq // rows,),
        in_specs=[pcol(RV, BLK_VR), pcol(RV, BLK_ZR), pcol(RK, BLK_QR), pcol(RK, BLK_KR),
                  pcol(GV, BLK_VG), pcol(GV, BLK_ZG), pcol(2 * GK, BLK_QKG), tok(GK),
                  full3((RET_HEADS, rows, 1)), full3((RET_HEADS, rows, 1)),
                  sg_spec, sr_spec],
        out_specs=[tok(GV), tok(RV), sg_spec, sr_spec],
        out_shape=[jax.ShapeDtypeStruct((batch * seq, GV), F32),
                   jax.ShapeDtypeStruct((batch * seq, RV), F32),
                   jax.ShapeDtypeStruct(state_gla.shape, F32),
                   jax.ShapeDtypeStruct(state_ret.shape, F32)],
        compiler_params=pltpu.CompilerParams(dimension_semantics=("parallel",),
                                             vmem_limit_bytes=VMEM_LIMIT),
        name="mixer_sample",
    )(p, p, p, p, p, p, p, la, jnp.asarray(d_read), jnp.asarray(d_write), state_gla, state_ret)


def _out_kernel(og_ref, or_ref, mg_ref, mr_ref, x_ref, gate_ref, wbg_ref, wbr_ref, wo_ref, lng_ref, lnb_ref,
                y_ref):
    sub = min(OUT_SUB, y_ref.shape[0])
    blocks = [slice(r0, r0 + sub) for r0 in range(0, y_ref.shape[0], sub)]
    merged = []
    for rs in blocks:
        bg = _dot(og_ref[rs, :].astype(BF16), wbg_ref[...])
        br = _dot(or_ref[rs, :].astype(BF16), wbr_ref[...])
        merged.append((mg_ref[rs, :].astype(F32) * bg + mr_ref[rs, :].astype(F32) * br).astype(BF16))
    for rs, m in zip(blocks, merged):
        out = _dot(m, wo_ref[...])
        gate = gate_ref[...] if gate_ref.shape[0] == 1 else gate_ref[rs, :]
        r = DEEPNORM_ALPHA * x_ref[rs, :] + gate * out
        mu = jnp.mean(r, axis=-1, keepdims=True)
        c = r - mu
        var = jnp.mean(c * c, axis=-1, keepdims=True)
        y_ref[rs, :] = c * lax.rsqrt(var + LN_EPS) * lng_ref[...] + lnb_ref[...]


def _out_stage(og, orr, p, x2d, gate3, wbg, wbr, wo, ln_g, ln_b, *, tm, rows_per_mod):
    n = x2d.shape[0]
    if gate3.shape[1] != 1:
        gate_spec = pl.BlockSpec((None, tm, D_MODEL), lambda i: (0, i, 0))
    else:
        tiles_per_mod = rows_per_mod // tm
        gate_spec = pl.BlockSpec((None, 1, D_MODEL), lambda i: (i // tiles_per_mod, 0, 0))
    full = lambda shape: pl.BlockSpec(shape, lambda i: (0, 0), pipeline_mode=pl.Buffered(1))
    return pl.pallas_call(
        _out_kernel,
        grid=(n // tm,),
        in_specs=[pl.BlockSpec((tm, GV), lambda i: (i, 0)),
                  pl.BlockSpec((tm, RV), lambda i: (i, 0)),
                  pl.BlockSpec((tm, D_MODEL), lambda i: (i, BLK_MG)),
                  pl.BlockSpec((tm, D_MODEL), lambda i: (i, BLK_MR)),
                  pl.BlockSpec((tm, D_MODEL), lambda i: (i, 0)),
                  gate_spec,
                  full((GV, D_MODEL)), full((RV, D_MODEL)), full((D_MODEL, D_MODEL)),
                  full((1, D_MODEL)), full((1, D_MODEL))],
        out_specs=pl.BlockSpec((tm, D_MODEL), lambda i: (i, 0)),
        out_shape=jax.ShapeDtypeStruct((n, D_MODEL), F32),
        compiler_params=pltpu.CompilerParams(dimension_semantics=("parallel",),
                                             vmem_limit_bytes=VMEM_LIMIT),
        name="out_stage",
    )(og, orr, p, p, x2d, gate3, wbg, wbr, wo, ln_g, ln_b)


_W_SRC_BLK = (_O_VR // PROJ_TN, _O_VR // PROJ_TN + 1, _O_QR // PROJ_TN, _O_KR // PROJ_TN, _O_VG // PROJ_TN,
              _O_QG // PROJ_TN, _O_ZR // PROJ_TN, _O_ZR // PROJ_TN + 1, _O_ZG // PROJ_TN, _O_MG // PROJ_TN,
              _O_MR // PROJ_TN)
_LR_BLK = _O_LR // PROJ_TN


def _prep_w_kernel(src_ref, a_ref, b_ref, o_ref, lr_ref):
    blk = src_ref[pl.program_id(0)]

    @pl.when(blk >= _LR_BLK)
    def _():
        o_ref[...] = jnp.concatenate([a_ref[GLA_LOWRANK:, :], b_ref[...]], axis=0).T.astype(BF16)

    @pl.when(blk < _LR_BLK)
    def _():
        o_ref[...] = a_ref[...].T.astype(BF16)

    @pl.when(blk == _LR_BLK)
    def _():
        lr_ref[:GLA_LOWRANK, :] = a_ref[:GLA_LOWRANK, :].astype(BF16)
        lr_ref[GLA_LOWRANK:, :] = jnp.zeros((LR_PAD - GLA_LOWRANK, D_MODEL), BF16)


def _prep_w_in(wt):
    assert all(o % PROJ_TN in (0, GLA_LOWRANK) for o in (_O_VR, _O_QR, _O_KR, _O_VG, _O_QG, _O_ZR, _O_ZG,
                                                         _O_MG, _O_MR)) and _O_LR % PROJ_TN == 0
    src = jnp.asarray(_W_SRC_BLK, jnp.int32)
    per_blk = PROJ_TN // GLA_LOWRANK
    return pl.pallas_call(
        _prep_w_kernel,
        grid_spec=pltpu.PrefetchScalarGridSpec(
            num_scalar_prefetch=1,
            grid=(len(_W_SRC_BLK),),
            in_specs=[pl.BlockSpec((PROJ_TN, D_MODEL), lambda j, src: (src[j], 0)),
                      pl.BlockSpec((GLA_LOWRANK, D_MODEL), lambda j, src: ((src[j] + 1) * per_blk, 0))],
            out_specs=[pl.BlockSpec((D_MODEL, PROJ_TN), lambda j, src: (0, j)),
                       pl.BlockSpec((LR_PAD, D_MODEL), lambda j, src: (0, 0))]),
        out_shape=[jax.ShapeDtypeStruct((D_MODEL, P_COLS), BF16),
                   jax.ShapeDtypeStruct((LR_PAD, D_MODEL), BF16)],
        compiler_params=pltpu.CompilerParams(dimension_semantics=("arbitrary",),
                                             vmem_limit_bytes=VMEM_LIMIT),
        name="prep_w",
    )(src, wt, wt)


def kernel(x_prompt, x_sample, state_gla, state_ret, c_prompt, c_sample, w_ada, b_ada, w_in, w_lr2, b_lr2,
           gla_norm_g, ret_norm_g, w_branch_gla, w_branch_ret, w_out, ln_g, ln_b):
    assert w_ada.shape[0] == 1, "one layer"
    bp, tp, _ = x_prompt.shape
    bs, ts, _ = x_sample.shape

    mod = _ada(jnp.concatenate([c_prompt, c_sample], axis=0), w_ada[0], b_ada)
    shift, scale1, gate = mod[:, :D_MODEL], mod[:, D_MODEL:2 * D_MODEL], mod[:, 2 * D_MODEL:]
    per_group = lambda a: a[:bp][:, None, :]
    per_token = lambda a: jnp.repeat(a[bp:], ts, axis=0)[None]

    w_main, w_lr = _prep_w_in(w_in[0].T)
    w_lr2p = jnp.pad(w_lr2[0], ((0, LR_PAD - GLA_LOWRANK), (0, 0))).astype(BF16)
    wbg = (gla_norm_g[0][:, None] * w_branch_gla[0]).astype(BF16)
    wbr = (ret_norm_g[0][:, None] * w_branch_ret[0]).astype(BF16)
    wo = w_out[0].astype(BF16)

    xp = x_prompt.reshape(bp * tp, D_MODEL)
    cos_p, sin_p = _rope_tables(jnp.arange(tp, dtype=jnp.int32))
    p_p, la_p = _proj(xp, per_group(scale1), per_group(shift), w_main, w_lr, w_lr2p, b_lr2, cos_p, sin_p,
                      tm=2048, rows_per_mod=tp, p_dtype=BF16, cum_chunk=CHUNK)
    og_p, or_p, sg_p, sr_p = _mixer_prompt(p_p, la_p, bp, tp)
    y_p = _out_stage(og_p, or_p, p_p, xp, per_group(gate), wbg, wbr, wo, ln_g, ln_b, tm=1024, rows_per_mod=tp)

    xs = x_sample.reshape(bs * ts, D_MODEL)
    cos_s, sin_s = _rope_tables(PAST_LEN + jnp.arange(bs * ts, dtype=jnp.int32) % ts)
    p_s, la_s = _proj(xs, per_token(scale1), per_token(shift), w_main, w_lr, w_lr2p, b_lr2, cos_s, sin_s,
                      tm=bs * ts, rows_per_mod=ts, p_dtype=F32)
    og_s, or_s, sg_s, sr_s = _mixer_sample(p_s, la_s, state_gla[0], state_ret[0], bs, ts)
    y_s = _out_stage(og_s, or_s, p_s, xs, per_token(gate), wbg, wbr, wo, ln_g, ln_b,
                     tm=bs * ts, rows_per_mod=ts)

    return (y_p.reshape(bp, tp, D_MODEL), y_s.reshape(bs, ts, D_MODEL),
            sg_p[None], sr_p[None], sg_s[None], sr_s[None])
```

```python
import functools

import numpy as np
import jax
import jax.numpy as jnp
from jax import lax
from jax.experimental import pallas as pl
from jax.experimental.pallas import tpu as pltpu

F32 = jnp.float32
BF16 = jnp.bfloat16

D_MODEL = 1024
PAST_LEN = 16384
GLA_HEADS, GLA_DK, GLA_DV = 4, 128, 256
RET_HEADS, RET_DK, RET_DV = 4, 256, 512
GK, GV = GLA_HEADS * GLA_DK, GLA_HEADS * GLA_DV
RK, RV = RET_HEADS * RET_DK, RET_HEADS * RET_DV
GLA_LOWRANK = 16
GLA_TAU = 16.0
GLA_SUB = 16
ROPE_BASE = 10000.0
DEEPNORM_ALPHA = 2.0 ** 0.25
LN_EPS = 1e-5
HEAD_NORM_EPS = 1e-5

CHUNK = 128
MIXER_CHUNKS = 4
SAMPLE_ROWS = 16
LANES = 128
LR_PAD = 256

PROJ_TN = 1024
PROJ_SUB = 1024
OUT_SUB = 256
P_COLS = 11 * PROJ_TN
BLK_VR, BLK_ZR = 0, 3
BLK_QR, BLK_KR, BLK_VG, BLK_QKG, BLK_ZG, BLK_MG, BLK_MR = 2, 3, 4, 5, 8, 9, 10
FIRST_SILU_BLK, FIRST_SIGMOID_BLK = 6, 9
_O_QG, _O_KG, _O_VG, _O_ZG, _O_LR = 0, GK, 2 * GK, 2 * GK + GV, 2 * GK + 2 * GV
_O_QR = _O_LR + GLA_LOWRANK
_O_KR = _O_QR + RK
_O_VR = _O_KR + RK
_O_ZR = _O_VR + RV
_O_MG = _O_ZR + RV
_O_MR = _O_MG + D_MODEL
_O_END = _O_MR + D_MODEL

VMEM_LIMIT = 56 * 1024 * 1024


def _dot(a, b):
    return jnp.dot(a, b, preferred_element_type=F32)


def _dot_nt(a, b):
    return lax.dot_general(a, b, (((1,), (1,)), ((), ())), preferred_element_type=F32)


def _log_sigmoid(x):
    return jnp.minimum(x, 0.0) - jnp.log(1.0 + jnp.exp(-jnp.abs(x)))


def _sigmoid(x):
    return 0.5 * jnp.tanh(0.5 * x) + 0.5


def _silu(x):
    return x * _sigmoid(x)


def _ada_kernel(c_ref, w_ref, b_ref, o_ref):
    one = jnp.where(pl.program_id(0) == 1, 1.0, 0.0)
    o_ref[...] = _dot(c_ref[...].astype(BF16), w_ref[...].astype(BF16)) + (b_ref[...] + one)


def _ada(c, w_ada, b_ada):
    n = c.shape[0]
    return pl.pallas_call(
        _ada_kernel,
        grid=(3,),
        in_specs=[pl.BlockSpec((n, D_MODEL), lambda j: (0, 0)),
                  pl.BlockSpec((D_MODEL, D_MODEL), lambda j: (0, j)),
                  pl.BlockSpec((1, D_MODEL), lambda j: (0, j))],
        out_specs=pl.BlockSpec((n, D_MODEL), lambda j: (0, j)),
        out_shape=jax.ShapeDtypeStruct((n, 3 * D_MODEL), F32),
        compiler_params=pltpu.CompilerParams(dimension_semantics=("arbitrary",)),
        name="ada",
    )(c, w_ada, b_ada)


def _proj_kernel(cum_chunk, x_ref, scale1_ref, shift_ref, w_ref, wlr_ref, wlr2_ref, blr2_ref, cos_ref, sin_ref,
                 p_ref, la_ref, h_ref):
    j = pl.program_id(1)

    @pl.when(j == 0)
    def _():
        h = (x_ref[...] * scale1_ref[...] + shift_ref[...]).astype(BF16)
        h_ref[...] = h
        sub = min(PROJ_SUB, h.shape[0])
        lr = jnp.concatenate([_dot_nt(h[r:r + sub, :], wlr_ref[...]) for r in range(0, h.shape[0], sub)], axis=0)
        pre = _dot(lr.astype(BF16), wlr2_ref[...]) + blr2_ref[...]
        la = _log_sigmoid(pre) * (1.0 / GLA_TAU)
        if cum_chunk:
            for r in range(0, la.shape[0], cum_chunk):
                la_ref[r:r + cum_chunk, :] = _cumsum_rows_mxu(la[r:r + cum_chunk, :])
        else:
            la_ref[...] = la

    def emit(epilogue):
        sub = min(PROJ_SUB, p_ref.shape[0])
        for r in range(0, p_ref.shape[0], sub):
            rs = slice(r, r + sub)
            p_ref[rs, :] = epilogue(_dot_nt(h_ref[rs, :], w_ref[...]), rs).astype(p_ref.dtype)

    def rotate(scale):
        return lambda a, rs: _rotary_heads(a, cos_ref[rs, :], sin_ref[rs, :], RET_HEADS, scale)

    plain = (j < FIRST_SILU_BLK) & (j != BLK_QR) & (j != BLK_KR)
    pl.when(plain)(lambda: emit(lambda a, rs: a))
    pl.when(j == BLK_QR)(lambda: emit(rotate(1.0)))
    pl.when(j == BLK_KR)(lambda: emit(rotate(RET_DK ** -0.5)))
    pl.when((j >= FIRST_SILU_BLK) & (j < FIRST_SIGMOID_BLK))(lambda: emit(lambda a, rs: _silu(a)))
    pl.when(j >= FIRST_SIGMOID_BLK)(lambda: emit(lambda a, rs: _sigmoid(a)))


def _proj(x2d, scale1_3, shift3, w_main, w_lr, w_lr2, b_lr2, cos, sin, *, tm, rows_per_mod, p_dtype, cum_chunk=0):
    assert cum_chunk == 0 or (tm % cum_chunk == 0 and rows_per_mod % cum_chunk == 0)
    n = x2d.shape[0]
    rope_tiles = cos.shape[0] // tm
    rope_spec = pl.BlockSpec((tm, RET_DK // 2), lambda i, j: (i % rope_tiles, 0))
    per_row = shift3.shape[1] != 1
    if per_row:
        mod_spec = pl.BlockSpec((None, tm, D_MODEL), lambda i, j: (0, i, 0))
    else:
        tiles_per_mod = rows_per_mod // tm
        mod_spec = pl.BlockSpec((None, 1, D_MODEL), lambda i, j: (i // tiles_per_mod, 0, 0))
    tn = PROJ_TN
    return pl.pallas_call(
        functools.partial(_proj_kernel, cum_chunk),
        grid=(n // tm, P_COLS // tn),
        in_specs=[pl.BlockSpec((tm, D_MODEL), lambda i, j: (i, 0)),
                  mod_spec, mod_spec,
                  pl.BlockSpec((tn, D_MODEL), lambda i, j: (j, 0)),
                  pl.BlockSpec((LR_PAD, D_MODEL), lambda i, j: (0, 0)),
                  pl.BlockSpec((LR_PAD, GK), lambda i, j: (0, 0)),
                  pl.BlockSpec((1, GK), lambda i, j: (0, 0)),
                  rope_spec, rope_spec],
        out_specs=[pl.BlockSpec((None, tm, tn), lambda i, j: (j, i, 0)),
                   pl.BlockSpec((tm, GK), lambda i, j: (i, 0))],
        out_shape=[jax.ShapeDtypeStruct((P_COLS // tn, n, tn), p_dtype),
                   jax.ShapeDtypeStruct((n, GK), F32)],
        scratch_shapes=[pltpu.VMEM((tm, D_MODEL), BF16)],
        compiler_params=pltpu.CompilerParams(dimension_semantics=("parallel", "arbitrary"),
                                             vmem_limit_bytes=VMEM_LIMIT),
        name="proj",
    )(x2d, scale1_3, shift3, w_main, w_lr, w_lr2, b_lr2, cos, sin)


def _rms_gate(o, sz):
    o = o * lax.rsqrt(jnp.mean(o * o, axis=-1, keepdims=True) + HEAD_NORM_EPS)
    return o.astype(sz.dtype) * sz


def _group_gate(o, sz):
    mu = jnp.mean(o, axis=-1, keepdims=True)
    c = o - mu
    var = jnp.mean(c * c, axis=-1, keepdims=True)
    return (c * lax.rsqrt(var + HEAD_NORM_EPS)).astype(sz.dtype) * sz


def _p_spec(rows, width, blk, row_index):
    n = width // PROJ_TN
    if n == 1:
        return pl.BlockSpec((None, rows, PROJ_TN), lambda *g: (blk, row_index(*g), 0))
    return pl.BlockSpec((n, rows, PROJ_TN), lambda *g: (blk, row_index(*g), 0))


def _wide(ref, rows, cols):
    return ref[cols.start // PROJ_TN, rows, cols.start % PROJ_TN:(cols.stop - 1) % PROJ_TN + 1]


def _cumsum_rows_mxu(x):
    n = x.shape[0]
    row = lax.broadcasted_iota(jnp.int32, (n, n), 0)
    col = lax.broadcasted_iota(jnp.int32, (n, n), 1)
    tri = jnp.where(col <= row, 1.0, 0.0).astype(BF16)
    hi = x.astype(BF16)
    rest = x - hi.astype(F32)
    mid = rest.astype(BF16)
    lo = (rest - mid.astype(F32)).astype(BF16)
    return _dot(jnp.concatenate([tri, tri, tri], axis=1), jnp.concatenate([hi, mid, lo], axis=0))


def _block_edge(b, w, prev):
    n, m = b.shape
    pieces = []
    for i in range(n // w):
        r = i * w - 1 if prev else i * w + w - 1
        if r < 0:
            pieces.append(jnp.zeros((w, m), F32))
        else:
            pieces.append(jnp.broadcast_to(b[r:r + 1, :], (w, m)))
    return jnp.concatenate(pieces, axis=0)


def _gla_levels(c):
    w = GLA_SUB
    while w < c:
        yield w
        w *= 2


def _gla_operands(q, k, b):
    c = q.shape[0]
    btot = b[c - 1:c, :]
    rel = b - _block_edge(b, GLA_SUB, True)
    ops = {"q_in": (q * jnp.exp(b)).astype(BF16),
           "k_out": (k * jnp.exp(btot - b)).astype(BF16),
           "qd": (q * jnp.exp(rel)).astype(BF16),
           "kd": (k * jnp.exp(-rel)).astype(BF16)}
    for w in _gla_levels(c):
        if w > GLA_SUB:
            ops["q%d" % w] = (q * jnp.exp(b - _block_edge(b, w, True))).astype(BF16)
        ops["k%d" % w] = (k * jnp.exp(_block_edge(b, w, False) - b)).astype(BF16)
    return ops


def _gla_level_map(c):
    row = lax.broadcasted_iota(jnp.int32, (c, c), 0)
    col = lax.broadcasted_iota(jnp.int32, (c, c), 1)
    sh = GLA_SUB.bit_length() - 1
    level = jnp.where(((row >> sh) == (col >> sh)) & (col <= row), GLA_SUB - 1, 0)
    for w in _gla_levels(c):
        sh = w.bit_length() - 1
        rb, cb = row >> sh, col >> sh
        level = jnp.where(((rb & 1) == 1) & (cb == rb - 1), w, level)
    return level


def _gla_scores(ops, hs, level, c):
    a = jnp.where(level == GLA_SUB - 1, _dot_nt(ops["qd"][:, hs], ops["kd"][:, hs]), 0.0)
    for w in _gla_levels(c):
        qw = ops["qd"] if w == GLA_SUB else ops["q%d" % w]
        a = jnp.where(level == w, _dot_nt(qw[:, hs], ops["k%d" % w][:, hs]), a)
    return a.astype(BF16)


def _rotary_heads(x, cos, sin, heads, scale=1.0):
    half = cos.shape[1]
    if scale != 1.0:
        cos, sin = cos * scale, sin * scale
    cos_t = jnp.concatenate([cos] * (2 * heads), axis=1)
    sin_t = jnp.concatenate([-sin, sin] * heads, axis=1)
    partner = [i + 1 - 2 * (i % 2) for i in range(2 * heads)]
    swapped = jnp.concatenate([x[:, j * half:(j + 1) * half] for j in partner], axis=1)
    return x * cos_t + swapped * sin_t


def _mixer_prompt_kernel(d_chunk, vr_ref, zr_ref, qr_ref, kr_ref, vg_ref, zg_ref, qkg_ref, b_ref,
                         dintra_ref, dread_ref, dwrite_ref,
                         og_ref, or_ref, sg_ref, sr_ref):
    @pl.when(pl.program_id(1) == 0)
    def _():
        sg_ref[...] = jnp.zeros_like(sg_ref)
        sr_ref[...] = jnp.zeros_like(sr_ref)

    c = CHUNK
    g_sl = [slice(h * GLA_DK, (h + 1) * GLA_DK) for h in range(GLA_HEADS)]
    gv_sl = [slice(h * GLA_DV, (h + 1) * GLA_DV) for h in range(GLA_HEADS)]
    r_sl = [slice(h * RET_DK, (h + 1) * RET_DK) for h in range(RET_HEADS)]
    rv_sl = [slice(h * RET_DV, (h + 1) * RET_DV) for h in range(RET_HEADS)]

    level = _gla_level_map(c)
    d_read = dread_ref[...].astype(BF16)
    d_write = dwrite_ref[...].astype(BF16)

    def chunk(i, carry):
        rows = pl.ds(pl.multiple_of(i * c, c), c)
        b = b_ref[rows, :]
        qb = qr_ref[rows, :]
        kb = kr_ref[rows, :]
        q_read = qb * d_read
        k_write = kb * d_write
        g_rows = jnp.exp(b[c - 1:c, :])
        g_cols = [jnp.broadcast_to(g_rows[:, s], (GLA_DK, GLA_DK)).T[:, :1] for s in g_sl]
        gops = _gla_operands(qkg_ref[rows, :GK].astype(F32) * (GLA_DK ** -0.5),
                             qkg_ref[rows, GK:].astype(F32), b)

        a_g = [_gla_scores(gops, s, level, c) for s in g_sl]
        a_r = [(_dot_nt(qb[:, s], kb[:, s]) * dintra_ref[h]).astype(BF16) for h, s in enumerate(r_sl)]

        for h in range(GLA_HEADS):
            o = _dot(gops["q_in"][:, g_sl[h]], sg_ref[h].astype(BF16)) + _dot(a_g[h], vg_ref[rows, gv_sl[h]])
            og_ref[rows, gv_sl[h]] = _rms_gate(o, zg_ref[rows, gv_sl[h]])
        for h in range(RET_HEADS):
            o = _dot(a_r[h], _wide(vr_ref, rows, rv_sl[h])) + _dot(q_read[:, r_sl[h]], sr_ref[h].astype(BF16))
            or_ref[rows, rv_sl[h]] = _group_gate(o, _wide(zr_ref, rows, rv_sl[h]))

        for h in range(GLA_HEADS):
            sg_ref[h] = g_cols[h] * sg_ref[h] + _dot(gops["k_out"][:, g_sl[h]].T, vg_ref[rows, gv_sl[h]])
        for h in range(RET_HEADS):
            sr_ref[h] = d_chunk[h] * sr_ref[h] + _dot(k_write[:, r_sl[h]].T, _wide(vr_ref, rows, rv_sl[h]))
        return carry

    lax.fori_loop(0, MIXER_CHUNKS, chunk, 0)


def _ret_consts(length):
    h = np.arange(RET_HEADS, dtype=np.float64)
    log_gamma = np.log1p(-np.exp2(-5.0 - h))
    idx = np.arange(length, dtype=np.float64)
    diff = idx[:, None] - idx[None, :]
    d_intra = np.where(diff[None] >= 0, np.exp(np.maximum(diff, 0.0)[None] * log_gamma[:, None, None]), 0.0)
    d_read = np.exp((idx + 1.0)[None, :] * log_gamma[:, None])
    d_write = np.exp((length - 1.0 - idx)[None, :] * log_gamma[:, None])
    d_chunk = np.exp(length * log_gamma)
    return (d_intra.astype(np.float32), d_read.astype(np.float32)[..., None],
            d_write.astype(np.float32)[..., None], tuple(float(np.float32(x)) for x in d_chunk))


def _rope_tables(pos):
    half = RET_DK // 2
    inv_freq = ROPE_BASE ** (-jnp.arange(half, dtype=F32) / half)
    ang = pos.astype(F32)[:, None] * inv_freq[None, :]
    return jnp.cos(ang), jnp.sin(ang)


def _mixer_prompt(p, la, batch, seq):
    assert p.dtype == BF16 and seq % (CHUNK * MIXER_CHUNKS) == 0
    d_intra, d_read, d_write, d_chunk = _ret_consts(CHUNK)
    d_read = np.repeat(d_read[..., 0].T, RET_DK, axis=1)
    d_write = np.repeat(d_write[..., 0].T, RET_DK, axis=1)

    rows = CHUNK * MIXER_CHUNKS
    nstep = seq // rows

    pcol = lambda width, blk: _p_spec(rows, width, blk, lambda b, t: b * nstep + t)
    const3 = lambda shape: pl.BlockSpec(shape, lambda b, t: (0, 0, 0))
    tok = lambda width: pl.BlockSpec((rows, width), lambda b, t: (b * nstep + t, 0))
    return pl.pallas_call(
        functools.partial(_mixer_prompt_kernel, d_chunk),
        grid=(batch, nstep),
        in_specs=[pcol(RV, BLK_VR), pcol(RV, BLK_ZR), pcol(RK, BLK_QR), pcol(RK, BLK_KR),
                  pcol(GV, BLK_VG), pcol(GV, BLK_ZG), pcol(2 * GK, BLK_QKG), tok(GK),
                  const3((RET_HEADS, CHUNK, CHUNK)),
                  pl.BlockSpec((CHUNK, RK), lambda b, t: (0, 0)),
                  pl.BlockSpec((CHUNK, RK), lambda b, t: (0, 0))],
        out_specs=[tok(GV), tok(RV),
                   pl.BlockSpec((None, GLA_HEADS, GLA_DK, GLA_DV), lambda b, t: (b, 0, 0, 0)),
                   pl.BlockSpec((None, RET_HEADS, RET_DK, RET_DV), lambda b, t: (b, 0, 0, 0))],
        out_shape=[jax.ShapeDtypeStruct((batch * seq, GV), BF16),
                   jax.ShapeDtypeStruct((batch * seq, RV), BF16),
                   jax.ShapeDtypeStruct((batch, GLA_HEADS, GLA_DK, GLA_DV), F32),
                   jax.ShapeDtypeStruct((batch, RET_HEADS, RET_DK, RET_DV), F32)],
        compiler_params=pltpu.CompilerParams(dimension_semantics=("parallel", "arbitrary"),
                                             vmem_limit_bytes=VMEM_LIMIT),
        name="mixer_prompt",
    )(p, p, p, p, p, p, p, la, jnp.asarray(d_intra), jnp.asarray(d_read), jnp.asarray(d_write))


def _pad_rows(x, rows):
    return jnp.concatenate([x, jnp.zeros((rows - x.shape[0], x.shape[1]), x.dtype)], axis=0)


def _mixer_sample_kernel(seq, gammas, d_chunk, vr_ref, zr_ref, qr_ref, kr_ref, vg_ref, zg_ref, qkg_ref,
                         la_ref, dread_ref, dwrite_ref,
                         sg_in, sr_in, og_ref, or_ref, sg_ref, sr_ref):
    rows = SAMPLE_ROWS
    n_el = rows // seq
    row1 = lax.broadcasted_iota(jnp.int32, (rows, 1), 0)
    pos1 = row1 % seq
    tcol = lax.broadcasted_iota(jnp.int32, (LANES, LANES), 1)

    def seg_cumsum(x):
        s = 1
        while s < seq:
            x = x + jnp.where(pos1 >= s, pltpu.roll(x, s, 0), 0.0)
            s *= 2
        return x

    def intra(q, k, v, gamma):
        out = jnp.zeros_like(v)
        for d in range(seq):
            kd = k if d == 0 else pltpu.roll(k, d, 0)
            vd = v if d == 0 else pltpu.roll(v, d, 0)
            a = jnp.sum(q * kd, axis=1, keepdims=True) * (gamma ** d)
            out = out + jnp.where(pos1 >= d, a, 0.0) * vd
        return out

    def inter(q, states):
        qb = q.astype(BF16)
        out = None
        for e in range(n_el):
            oe = _dot(qb, states[e].astype(BF16))
            out = oe if out is None else jnp.where(row1 >= e * seq, oe, out)
        return out

    for h in range(GLA_HEADS):
        kq = slice(h * GLA_DK, (h + 1) * GLA_DK)
        kk = slice(GK + h * GLA_DK, GK + (h + 1) * GLA_DK)
        vv = slice(h * GLA_DV, (h + 1) * GLA_DV)
        q = qkg_ref[:, kq] * (GLA_DK ** -0.5)
        k = qkg_ref[:, kk]
        v = vg_ref[:, vv]
        la = la_ref[:, kq]
        b = seg_cumsum(la)
        q_dec = q * jnp.exp(b)
        k_dec = k * jnp.exp(-b)
        states = [sg_in[e, h] for e in range(n_el)]
        o = inter(q_dec, states) + intra(q_dec, k_dec, v, 1.0)
        og_ref[:, vv] = _rms_gate(o, zg_ref[:, vv])
        b_last = jnp.zeros_like(b)
        for e in range(n_el):
            r = e * seq + seq - 1
            b_last = jnp.where((row1 >= e * seq) & (row1 < (e + 1) * seq),
                               jnp.broadcast_to(b[r:r + 1, :], b.shape), b_last)
        k_out = k * jnp.exp(b_last - b)
        k_outT = _pad_rows(k_out, LANES).T
        laT = _pad_rows(la, LANES).T
        vb = _pad_rows(v, LANES).astype(BF16)
        tl = tcol[:GLA_DK]
        for e in range(n_el):
            sel = (tl >= e * seq) & (tl < (e + 1) * seq)
            g_col = jnp.exp(jnp.sum(jnp.where(sel, laT, 0.0), axis=1, keepdims=True))
            sg_ref[e, h] = g_col * states[e] + _dot(jnp.where(sel, k_outT, 0.0).astype(BF16), vb)

    for h in range(RET_HEADS):
        kq = slice(h * RET_DK, (h + 1) * RET_DK)
        vv = slice(h * RET_DV, (h + 1) * RET_DV)
        q = qr_ref[:, kq]
        k = kr_ref[:, kq]
        v = _wide(vr_ref, slice(None), vv)
        states = [sr_in[e, h] for e in range(n_el)]
        o = inter(q, states) * dread_ref[h] + intra(q, k, v, gammas[h])
        or_ref[:, vv] = _group_gate(o, _wide(zr_ref, slice(None), vv))
        kwT = _pad_rows(k * dwrite_ref[h], LANES).T
        vb = _pad_rows(v, LANES).astype(BF16)
        tl = jnp.concatenate([tcol, tcol], axis=0)
        for e in range(n_el):
            sel = (tl >= e * seq) & (tl < (e + 1) * seq)
            sr_ref[e, h] = d_chunk[h] * states[e] + _dot(jnp.where(sel, kwT, 0.0).astype(BF16), vb)


def _mixer_sample(p, la, state_gla, state_ret, batch, seq):
    rows = SAMPLE_ROWS
    n_el = rows // seq
    _, d_read, d_write, d_chunk = _ret_consts(seq)
    gammas = tuple(float(1.0 - 2.0 ** (-5.0 - h)) for h in range(RET_HEADS))
    d_read = np.tile(d_read, (1, n_el, 1))
    d_write = np.tile(d_write, (1, n_el, 1))

    pcol = lambda width, blk: _p_spec(rows, width, blk, lambda i: i)
    tok = lambda width: pl.BlockSpec((rows, width), lambda i: (i, 0))
    full3 = lambda shape: pl.BlockSpec(shape, lambda i: (0, 0, 0))
    sg_spec = pl.BlockSpec((n_el, GLA_HEADS, GLA_DK, GLA_DV), lambda i: (i, 0, 0, 0))
    sr_spec = pl.BlockSpec((n_el, RET_HEADS, RET_DK, RET_DV), lambda i: (i, 0, 0, 0))
    return pl.pallas_call(
        functools.partial(_mixer_sample_kernel, seq, gammas, d_chunk),
        grid=(batch * seq // rows,),
        in_specs=[pcol(RV, BLK_VR), pcol(RV, BLK_ZR), pcol(RK, BLK_QR), pcol(RK, BLK_KR),
                  pcol(GV, BLK_VG), pcol(GV, BLK_ZG), pcol(2 * GK, BLK_QKG), tok(GK),
                  full3((RET_HEADS, rows, 1)), full3((RET_HEADS, rows, 1)),
                  sg_spec, sr_spec],
        out_specs=[tok(GV), tok(RV), sg_spec, sr_spec],
        out_shape=[jax.ShapeDtypeStruct((batch * seq, GV), F32),
                   jax.ShapeDtypeStruct((batch * seq, RV), F32),
                   jax.ShapeDtypeStruct(state_gla.shape, F32),
                   jax.ShapeDtypeStruct(state_ret.shape, F32)],
        compiler_params=pltpu.CompilerParams(dimension_semantics=("parallel",),
                                             vmem_limit_bytes=VMEM_LIMIT),
        name="mixer_sample",
    )(p, p, p, p, p, p, p, la, jnp.asarray(d_read), jnp.asarray(d_write), state_gla, state_ret)


def _out_kernel(og_ref, or_ref, mg_ref, mr_ref, x_ref, gate_ref, wbg_ref, wbr_ref, wo_ref, lng_ref, lnb_ref,
                y_ref):
    sub = min(OUT_SUB, y_ref.shape[0])
    blocks = [slice(r0, r0 + sub) for r0 in range(0, y_ref.shape[0], sub)]
    merged = []
    for rs in blocks:
        bg = _dot(og_ref[rs, :].astype(BF16), wbg_ref[...])
        br = _dot(or_ref[rs, :].astype(BF16), wbr_ref[...])
        merged.append((mg_ref[rs, :].astype(F32) * bg + mr_ref[rs, :].astype(F32) * br).astype(BF16))
    for rs, m in zip(blocks, merged):
        out = _dot(m, wo_ref[...])
        gate = gate_ref[...] if gate_ref.shape[0] == 1 else gate_ref[rs, :]
        r = DEEPNORM_ALPHA * x_ref[rs, :] + gate * out
        mu = jnp.mean(r, axis=-1, keepdims=True)
        c = r - mu
        var = jnp.mean(c * c, axis=-1, keepdims=True)
        y_ref[rs, :] = c * lax.rsqrt(var + LN_EPS) * lng_ref[...] + lnb_ref[...]


def _out_stage(og, orr, p, x2d, gate3, wbg, wbr, wo, ln_g, ln_b, *, tm, rows_per_mod):
    n = x2d.shape[0]
    if gate3.shape[1] != 1:
        gate_spec = pl.BlockSpec((None, tm, D_MODEL), lambda i: (0, i, 0))
    else:
        tiles_per_mod = rows_per_mod // tm
        gate_spec = pl.BlockSpec((None, 1, D_MODEL), lambda i: (i // tiles_per_mod, 0, 0))
    full = lambda shape: pl.BlockSpec(shape, lambda i: (0, 0), pipeline_mode=pl.Buffered(1))
    return pl.pallas_call(
        _out_kernel,
        grid=(n // tm,),
        in_specs=[pl.BlockSpec((tm, GV), lambda i: (i, 0)),
                  pl.BlockSpec((tm, RV), lambda i: (i, 0)),
                  _p_spec(tm, D_MODEL, BLK_MG, lambda i: i),
                  _p_spec(tm, D_MODEL, BLK_MR, lambda i: i),
                  pl.BlockSpec((tm, D_MODEL), lambda i: (i, 0)),
                  gate_spec,
                  full((GV, D_MODEL)), full((RV, D_MODEL)), full((D_MODEL, D_MODEL)),
                  full((1, D_MODEL)), full((1, D_MODEL))],
        out_specs=pl.BlockSpec((tm, D_MODEL), lambda i: (i, 0)),
        out_shape=jax.ShapeDtypeStruct((n, D_MODEL), F32),
        compiler_params=pltpu.CompilerParams(dimension_semantics=("parallel",),
                                             vmem_limit_bytes=VMEM_LIMIT),
        name="out_stage",
    )(og, orr, p, p, x2d, gate3, wbg, wbr, wo, ln_g, ln_b)


_W_SRC_BLK = (_O_VR // PROJ_TN, _O_VR // PROJ_TN + 1, _O_QR // PROJ_TN, _O_KR // PROJ_TN, _O_VG // PROJ_TN,
              _O_QG // PROJ_TN, _O_ZR // PROJ_TN, _O_ZR // PROJ_TN + 1, _O_ZG // PROJ_TN, _O_MG // PROJ_TN,
              _O_MR // PROJ_TN)
_LR_BLK = _O_LR // PROJ_TN


def _prep_w_kernel(src_ref, a_ref, b_ref, o_ref, lr_ref):
    blk = src_ref[pl.program_id(0)]
    keep = PROJ_TN - GLA_LOWRANK

    @pl.when(blk >= _LR_BLK)
    def _():
        o_ref[:keep, :] = a_ref[GLA_LOWRANK:, :].astype(BF16)
        o_ref[keep:, :] = b_ref[...].astype(BF16)

    @pl.when(blk < _LR_BLK)
    def _():
        o_ref[...] = a_ref[...].astype(BF16)

    @pl.when(blk == _LR_BLK)
    def _():
        lr_ref[:GLA_LOWRANK, :] = a_ref[:GLA_LOWRANK, :].astype(BF16)
        lr_ref[GLA_LOWRANK:, :] = jnp.zeros((LR_PAD - GLA_LOWRANK, D_MODEL), BF16)


def _prep_w_in(wt):
    assert all(o % PROJ_TN in (0, GLA_LOWRANK) for o in (_O_VR, _O_QR, _O_KR, _O_VG, _O_QG, _O_ZR, _O_ZG,
                                                         _O_MG, _O_MR)) and _O_LR % PROJ_TN == 0
    src = jnp.asarray(_W_SRC_BLK, jnp.int32)
    per_blk = PROJ_TN // GLA_LOWRANK
    return pl.pallas_call(
        _prep_w_kernel,
        grid_spec=pltpu.PrefetchScalarGridSpec(
            num_scalar_prefetch=1,
            grid=(len(_W_SRC_BLK),),
            in_specs=[pl.BlockSpec((PROJ_TN, D_MODEL), lambda j, src: (src[j], 0)),
                      pl.BlockSpec((GLA_LOWRANK, D_MODEL), lambda j, src: ((src[j] + 1) * per_blk, 0))],
            out_specs=[pl.BlockSpec((PROJ_TN, D_MODEL), lambda j, src: (j, 0)),
                       pl.BlockSpec((LR_PAD, D_MODEL), lambda j, src: (0, 0))]),
        out_shape=[jax.ShapeDtypeStruct((P_COLS, D_MODEL), BF16),
                   jax.ShapeDtypeStruct((LR_PAD, D_MODEL), BF16)],
        compiler_params=pltpu.CompilerParams(dimension_semantics=("arbitrary",),
                                             vmem_limit_bytes=VMEM_LIMIT),
        name="prep_w",
    )(src, wt, wt)


def kernel(x_prompt, x_sample, state_gla, state_ret, c_prompt, c_sample, w_ada, b_ada, w_in, w_lr2, b_lr2,
           gla_norm_g, ret_norm_g, w_branch_gla, w_branch_ret, w_out, ln_g, ln_b):
    assert w_ada.shape[0] == 1, "one layer"
    bp, tp, _ = x_prompt.shape
    bs, ts, _ = x_sample.shape

    mod = _ada(jnp.concatenate([c_prompt, c_sample], axis=0), w_ada[0], b_ada)
    shift, scale1, gate = mod[:, :D_MODEL], mod[:, D_MODEL:2 * D_MODEL], mod[:, 2 * D_MODEL:]
    per_group = lambda a: a[:bp][:, None, :]
    per_token = lambda a: jnp.repeat(a[bp:], ts, axis=0)[None]

    w_main, w_lr = _prep_w_in(w_in[0].T)
    w_lr2p = jnp.pad(w_lr2[0], ((0, LR_PAD - GLA_LOWRANK), (0, 0))).astype(BF16)
    wbg = (gla_norm_g[0][:, None] * w_branch_gla[0]).astype(BF16)
    wbr = (ret_norm_g[0][:, None] * w_branch_ret[0]).astype(BF16)
    wo = w_out[0].astype(BF16)

    xp = x_prompt.reshape(bp * tp, D_MODEL)
    cos_p, sin_p = _rope_tables(jnp.arange(tp, dtype=jnp.int32))
    p_p, la_p = _proj(xp, per_group(scale1), per_group(shift), w_main, w_lr, w_lr2p, b_lr2, cos_p, sin_p,
                      tm=2048, rows_per_mod=tp, p_dtype=BF16, cum_chunk=CHUNK)
    og_p, or_p, sg_p, sr_p = _mixer_prompt(p_p, la_p, bp, tp)
    y_p = _out_stage(og_p, or_p, p_p, xp, per_group(gate), wbg, wbr, wo, ln_g, ln_b, tm=1024, rows_per_mod=tp)

    xs = x_sample.reshape(bs * ts, D_MODEL)
    cos_s, sin_s = _rope_tables(PAST_LEN + jnp.arange(bs * ts, dtype=jnp.int32) % ts)
    p_s, la_s = _proj(xs, per_token(scale1), per_token(shift), w_main, w_lr, w_lr2p, b_lr2, cos_s, sin_s,
                      tm=bs * ts, rows_per_mod=ts, p_dtype=F32)
    og_s, or_s, sg_s, sr_s = _mixer_sample(p_s, la_s, state_gla[0], state_ret[0], bs, ts)
    y_s = _out_stage(og_s, or_s, p_s, xs, per_token(gate), wbg, wbr, wo, ln_g, ln_b,
                     tm=bs * ts, rows_per_mod=ts)

    return (y_p.reshape(bp, tp, D_MODEL), y_s.reshape(bs, ts, D_MODEL),
            sg_p[None], sr_p[None], sg_s[None], sr_s[None])
```

```python
import functools

import numpy as np
import jax
import jax.numpy as jnp
from jax import lax
from jax.experimental import pallas as pl
from jax.experimental.pallas import tpu as pltpu

F32 = jnp.float32
BF16 = jnp.bfloat16

D_MODEL = 1024
PAST_LEN = 16384
GLA_HEADS, GLA_DK, GLA_DV = 4, 128, 256
RET_HEADS, RET_DK, RET_DV = 4, 256, 512
GK, GV = GLA_HEADS * GLA_DK, GLA_HEADS * GLA_DV
RK, RV = RET_HEADS * RET_DK, RET_HEADS * RET_DV
GLA_LOWRANK = 16
GLA_TAU = 16.0
GLA_SUB = 16
ROPE_BASE = 10000.0
DEEPNORM_ALPHA = 2.0 ** 0.25
LN_EPS = 1e-5
HEAD_NORM_EPS = 1e-5

CHUNK = 128
MIXER_CHUNKS = 4
SAMPLE_ROWS = 16
LANES = 128
LR_PAD = 256

P_BLK = 1024
P_COLS = 11 * P_BLK
BLK_VR, BLK_ZR = 0, 3
BLK_QR, BLK_KR, BLK_VG, BLK_QKG, BLK_ZG, BLK_MG, BLK_MR = 2, 3, 4, 5, 8, 9, 10
P_RUNS = ((0, "plain"), (BLK_QR * P_BLK, "rot_q"), (BLK_KR * P_BLK, "rot_k"), (BLK_VG * P_BLK, "plain"),
          (6 * P_BLK, "silu"), (BLK_MG * P_BLK, "sigmoid"), (P_COLS, None))
PROJ_TN = P_COLS // 4
PROJ_SUB = 1024
OUT_SUB = 256
_O_QG, _O_KG, _O_VG, _O_ZG, _O_LR = 0, GK, 2 * GK, 2 * GK + GV, 2 * GK + 2 * GV
_O_QR = _O_LR + GLA_LOWRANK
_O_KR = _O_QR + RK
_O_VR = _O_KR + RK
_O_ZR = _O_VR + RV
_O_MG = _O_ZR + RV
_O_MR = _O_MG + D_MODEL
_O_END = _O_MR + D_MODEL

VMEM_LIMIT = 56 * 1024 * 1024


def _dot(a, b):
    return jnp.dot(a, b, preferred_element_type=F32)


def _dot_nt(a, b):
    return lax.dot_general(a, b, (((1,), (1,)), ((), ())), preferred_element_type=F32)


def _log_sigmoid(x):
    return jnp.minimum(x, 0.0) - jnp.log(1.0 + jnp.exp(-jnp.abs(x)))


def _sigmoid(x):
    return 0.5 * jnp.tanh(0.5 * x) + 0.5


def _silu(x):
    return x * _sigmoid(x)


def _ada_kernel(c_ref, w_ref, b_ref, o_ref):
    one = jnp.where(pl.program_id(0) == 1, 1.0, 0.0)
    o_ref[...] = _dot(c_ref[...].astype(BF16), w_ref[...].astype(BF16)) + (b_ref[...] + one)


def _ada(c, w_ada, b_ada):
    n = c.shape[0]
    return pl.pallas_call(
        _ada_kernel,
        grid=(3,),
        in_specs=[pl.BlockSpec((n, D_MODEL), lambda j: (0, 0)),
                  pl.BlockSpec((D_MODEL, D_MODEL), lambda j: (0, j)),
                  pl.BlockSpec((1, D_MODEL), lambda j: (0, j))],
        out_specs=pl.BlockSpec((n, D_MODEL), lambda j: (0, j)),
        out_shape=jax.ShapeDtypeStruct((n, 3 * D_MODEL), F32),
        compiler_params=pltpu.CompilerParams(dimension_semantics=("arbitrary",)),
        name="ada",
    )(c, w_ada, b_ada)


def _proj_kernel(cum_chunk, x_ref, scale1_ref, shift_ref, w_ref, wlr_ref, wlr2_ref, blr2_ref, cos_ref, sin_ref,
                 p_ref, la_ref, h_ref):
    j = pl.program_id(1)

    @pl.when(j == 0)
    def _():
        h = (x_ref[...] * scale1_ref[...] + shift_ref[...]).astype(BF16)
        h_ref[...] = h
        sub = min(PROJ_SUB, h.shape[0]) // 2
        lr = jnp.concatenate([_dot_nt(h[r:r + sub, :], wlr_ref[...]) for r in range(0, h.shape[0], sub)], axis=0)
        pre = _dot(lr.astype(BF16), wlr2_ref[...]) + blr2_ref[...]
        la = _log_sigmoid(pre) * (1.0 / GLA_TAU)
        if cum_chunk:
            for r in range(0, la.shape[0], cum_chunk):
                la_ref[r:r + cum_chunk, :] = _cumsum_rows_mxu(la[r:r + cum_chunk, :])
        else:
            la_ref[...] = la

    def epilogue(kind, a, rs):
        if kind in ("rot_q", "rot_k"):
            scale = 1.0 if kind == "rot_q" else RET_DK ** -0.5
            return _rotary_heads(a, cos_ref[rs, :], sin_ref[rs, :], a.shape[1] // RET_DK, scale)
        return {"plain": lambda v: v, "silu": _silu, "sigmoid": _sigmoid}[kind](a)

    def emit(step):
        tn = p_ref.shape[1]
        lo, hi = step * tn, (step + 1) * tn
        runs = [(max(a, lo) - lo, min(b, hi) - lo, kind)
                for (a, kind), (b, _) in zip(P_RUNS[:-1], P_RUNS[1:]) if max(a, lo) < min(b, hi)]
        sub = min(PROJ_SUB, p_ref.shape[0])
        for r in range(0, p_ref.shape[0], sub):
            rs = slice(r, r + sub)
            acc = _dot_nt(h_ref[rs, :], w_ref[...])
            for a, b, kind in runs:
                p_ref[rs, a:b] = epilogue(kind, acc[:, a:b], rs).astype(p_ref.dtype)

    for step in range(P_COLS // p_ref.shape[1]):
        pl.when(j == step)(functools.partial(emit, step))


def _proj(x2d, scale1_3, shift3, w_main, w_lr, w_lr2, b_lr2, cos, sin, *, tm, rows_per_mod, p_dtype, cum_chunk=0):
    assert cum_chunk == 0 or (tm % cum_chunk == 0 and rows_per_mod % cum_chunk == 0)
    n = x2d.shape[0]
    rope_tiles = cos.shape[0] // tm
    rope_spec = pl.BlockSpec((tm, RET_DK // 2), lambda i, j: (i % rope_tiles, 0))
    per_row = shift3.shape[1] != 1
    if per_row:
        mod_spec = pl.BlockSpec((None, tm, D_MODEL), lambda i, j: (0, i, 0))
    else:
        tiles_per_mod = rows_per_mod // tm
        mod_spec = pl.BlockSpec((None, 1, D_MODEL), lambda i, j: (i // tiles_per_mod, 0, 0))
    tn = PROJ_TN
    assert all(a % RET_DK == 0 for a, _ in P_RUNS) and tn % RET_DK == 0
    return pl.pallas_call(
        functools.partial(_proj_kernel, cum_chunk),
        grid=(n // tm, P_COLS // tn),
        in_specs=[pl.BlockSpec((tm, D_MODEL), lambda i, j: (i, 0)),
                  mod_spec, mod_spec,
                  pl.BlockSpec((tn, D_MODEL), lambda i, j: (j, 0)),
                  pl.BlockSpec((LR_PAD, D_MODEL), lambda i, j: (0, 0)),
                  pl.BlockSpec((LR_PAD, GK), lambda i, j: (0, 0)),
                  pl.BlockSpec((1, GK), lambda i, j: (0, 0)),
                  rope_spec, rope_spec],
        out_specs=[pl.BlockSpec((tm, tn), lambda i, j: (i, j)),
                   pl.BlockSpec((tm, GK), lambda i, j: (i, 0))],
        out_shape=[jax.ShapeDtypeStruct((n, P_COLS), p_dtype),
                   jax.ShapeDtypeStruct((n, GK), F32)],
        scratch_shapes=[pltpu.VMEM((tm, D_MODEL), BF16)],
        compiler_params=pltpu.CompilerParams(dimension_semantics=("parallel", "arbitrary"),
                                             vmem_limit_bytes=VMEM_LIMIT),
        name="proj",
    )(x2d, scale1_3, shift3, w_main, w_lr, w_lr2, b_lr2, cos, sin)


def _rms_gate(o, sz):
    o = o * lax.rsqrt(jnp.mean(o * o, axis=-1, keepdims=True) + HEAD_NORM_EPS)
    return o.astype(sz.dtype) * sz


def _group_gate(o, sz):
    mu = jnp.mean(o, axis=-1, keepdims=True)
    c = o - mu
    var = jnp.mean(c * c, axis=-1, keepdims=True)
    return (c * lax.rsqrt(var + HEAD_NORM_EPS)).astype(sz.dtype) * sz


def _cumsum_rows_mxu(x):
    n = x.shape[0]
    row = lax.broadcasted_iota(jnp.int32, (n, n), 0)
    col = lax.broadcasted_iota(jnp.int32, (n, n), 1)
    tri = jnp.where(col <= row, 1.0, 0.0).astype(BF16)
    hi = x.astype(BF16)
    rest = x - hi.astype(F32)
    mid = rest.astype(BF16)
    lo = (rest - mid.astype(F32)).astype(BF16)
    return _dot(jnp.concatenate([tri, tri, tri], axis=1), jnp.concatenate([hi, mid, lo], axis=0))


def _block_edge(b, w, prev):
    n, m = b.shape
    pieces = []
    for i in range(n // w):
        r = i * w - 1 if prev else i * w + w - 1
        if r < 0:
            pieces.append(jnp.zeros((w, m), F32))
        else:
            pieces.append(jnp.broadcast_to(b[r:r + 1, :], (w, m)))
    return jnp.concatenate(pieces, axis=0)


def _gla_levels(c):
    w = GLA_SUB
    while w < c:
        yield w
        w *= 2


def _gla_operands(q, k, b):
    c = q.shape[0]
    btot = b[c - 1:c, :]
    rel = b - _block_edge(b, GLA_SUB, True)
    ops = {"q_in": (q * jnp.exp(b)).astype(BF16),
           "k_out": (k * jnp.exp(btot - b)).astype(BF16),
           "qd": (q * jnp.exp(rel)).astype(BF16),
           "kd": (k * jnp.exp(-rel)).astype(BF16)}
    for w in _gla_levels(c):
        if w > GLA_SUB:
            ops["q%d" % w] = (q * jnp.exp(b - _block_edge(b, w, True))).astype(BF16)
        ops["k%d" % w] = (k * jnp.exp(_block_edge(b, w, False) - b)).astype(BF16)
    return ops


def _gla_level_map(c):
    row = lax.broadcasted_iota(jnp.int32, (c, c), 0)
    col = lax.broadcasted_iota(jnp.int32, (c, c), 1)
    sh = GLA_SUB.bit_length() - 1
    level = jnp.where(((row >> sh) == (col >> sh)) & (col <= row), GLA_SUB - 1, 0)
    for w in _gla_levels(c):
        sh = w.bit_length() - 1
        rb, cb = row >> sh, col >> sh
        level = jnp.where(((rb & 1) == 1) & (cb == rb - 1), w, level)
    return level


def _gla_scores(ops, hs, level, c):
    a = jnp.where(level == GLA_SUB - 1, _dot_nt(ops["qd"][:, hs], ops["kd"][:, hs]), 0.0)
    for w in _gla_levels(c):
        qw = ops["qd"] if w == GLA_SUB else ops["q%d" % w]
        a = jnp.where(level == w, _dot_nt(qw[:, hs], ops["k%d" % w][:, hs]), a)
    return a.astype(BF16)


def _rotary_heads(x, cos, sin, heads, scale=1.0):
    half = cos.shape[1]
    if scale != 1.0:
        cos, sin = cos * scale, sin * scale
    cos_t = jnp.concatenate([cos] * (2 * heads), axis=1)
    sin_t = jnp.concatenate([-sin, sin] * heads, axis=1)
    partner = [i + 1 - 2 * (i % 2) for i in range(2 * heads)]
    swapped = jnp.concatenate([x[:, j * half:(j + 1) * half] for j in partner], axis=1)
    return x * cos_t + swapped * sin_t


def _mixer_prompt_kernel(d_chunk, vr_ref, zr_ref, qr_ref, kr_ref, vg_ref, zg_ref, qkg_ref, b_ref,
                         dintra_ref, dread_ref, dwrite_ref,
                         og_ref, or_ref, sg_ref, sr_ref):
    @pl.when(pl.program_id(1) == 0)
    def _():
        sg_ref[...] = jnp.zeros_like(sg_ref)
        sr_ref[...] = jnp.zeros_like(sr_ref)

    c = CHUNK
    g_sl = [slice(h * GLA_DK, (h + 1) * GLA_DK) for h in range(GLA_HEADS)]
    gv_sl = [slice(h * GLA_DV, (h + 1) * GLA_DV) for h in range(GLA_HEADS)]
    r_sl = [slice(h * RET_DK, (h + 1) * RET_DK) for h in range(RET_HEADS)]
    rv_sl = [slice(h * RET_DV, (h + 1) * RET_DV) for h in range(RET_HEADS)]

    level = _gla_level_map(c)
    d_read = dread_ref[...].astype(BF16)
    d_write = dwrite_ref[...].astype(BF16)

    def chunk(i, carry):
        rows = pl.ds(pl.multiple_of(i * c, c), c)
        b = b_ref[rows, :]
        qb = qr_ref[rows, :]
        kb = kr_ref[rows, :]
        q_read = qb * d_read
        k_write = kb * d_write
        g_rows = jnp.exp(b[c - 1:c, :])
        g_cols = [jnp.broadcast_to(g_rows[:, s], (GLA_DK, GLA_DK)).T[:, :1] for s in g_sl]
        gops = _gla_operands(qkg_ref[rows, :GK].astype(F32) * (GLA_DK ** -0.5),
                             qkg_ref[rows, GK:].astype(F32), b)

        a_g = [_gla_scores(gops, s, level, c) for s in g_sl]
        a_r = [(_dot_nt(qb[:, s], kb[:, s]) * dintra_ref[h]).astype(BF16) for h, s in enumerate(r_sl)]

        for h in range(GLA_HEADS):
            o = _dot(gops["q_in"][:, g_sl[h]], sg_ref[h].astype(BF16)) + _dot(a_g[h], vg_ref[rows, gv_sl[h]])
            og_ref[rows, gv_sl[h]] = _rms_gate(o, zg_ref[rows, gv_sl[h]])
        for h in range(RET_HEADS):
            o = _dot(a_r[h], vr_ref[rows, rv_sl[h]]) + _dot(q_read[:, r_sl[h]], sr_ref[h].astype(BF16))
            or_ref[rows, rv_sl[h]] = _group_gate(o, zr_ref[rows, rv_sl[h]])

        for h in range(GLA_HEADS):
            sg_ref[h] = g_cols[h] * sg_ref[h] + _dot(gops["k_out"][:, g_sl[h]].T, vg_ref[rows, gv_sl[h]])
        for h in range(RET_HEADS):
            sr_ref[h] = d_chunk[h] * sr_ref[h] + _dot(k_write[:, r_sl[h]].T, vr_ref[rows, rv_sl[h]])
        return carry

    lax.fori_loop(0, MIXER_CHUNKS, chunk, 0)


def _ret_consts(length):
    h = np.arange(RET_HEADS, dtype=np.float64)
    log_gamma = np.log1p(-np.exp2(-5.0 - h))
    idx = np.arange(length, dtype=np.float64)
    diff = idx[:, None] - idx[None, :]
    d_intra = np.where(diff[None] >= 0, np.exp(np.maximum(diff, 0.0)[None] * log_gamma[:, None, None]), 0.0)
    d_read = np.exp((idx + 1.0)[None, :] * log_gamma[:, None])
    d_write = np.exp((length - 1.0 - idx)[None, :] * log_gamma[:, None])
    d_chunk = np.exp(length * log_gamma)
    return (d_intra.astype(np.float32), d_read.astype(np.float32)[..., None],
            d_write.astype(np.float32)[..., None], tuple(float(np.float32(x)) for x in d_chunk))


def _rope_tables(pos):
    half = RET_DK // 2
    inv_freq = ROPE_BASE ** (-jnp.arange(half, dtype=F32) / half)
    ang = pos.astype(F32)[:, None] * inv_freq[None, :]
    return jnp.cos(ang), jnp.sin(ang)


def _mixer_prompt(p, la, batch, seq):
    assert p.dtype == BF16 and seq % (CHUNK * MIXER_CHUNKS) == 0
    d_intra, d_read, d_write, d_chunk = _ret_consts(CHUNK)
    d_read = np.repeat(d_read[..., 0].T, RET_DK, axis=1)
    d_write = np.repeat(d_write[..., 0].T, RET_DK, axis=1)

    rows = CHUNK * MIXER_CHUNKS
    nstep = seq // rows

    def pcol(width, blk):
        return pl.BlockSpec((rows, width), lambda b, t: (b * nstep + t, blk))

    const3 = lambda shape: pl.BlockSpec(shape, lambda b, t: (0, 0, 0))
    tok = lambda width: pl.BlockSpec((rows, width), lambda b, t: (b * nstep + t, 0))
    return pl.pallas_call(
        functools.partial(_mixer_prompt_kernel, d_chunk),
        grid=(batch, nstep),
        in_specs=[pcol(RV, BLK_VR), pcol(RV, BLK_ZR), pcol(RK, BLK_QR), pcol(RK, BLK_KR),
                  pcol(GV, BLK_VG), pcol(GV, BLK_ZG), pcol(2 * GK, BLK_QKG), tok(GK),
                  const3((RET_HEADS, CHUNK, CHUNK)),
                  pl.BlockSpec((CHUNK, RK), lambda b, t: (0, 0)),
                  pl.BlockSpec((CHUNK, RK), lambda b, t: (0, 0))],
        out_specs=[tok(GV), tok(RV),
                   pl.BlockSpec((None, GLA_HEADS, GLA_DK, GLA_DV), lambda b, t: (b, 0, 0, 0)),
                   pl.BlockSpec((None, RET_HEADS, RET_DK, RET_DV), lambda b, t: (b, 0, 0, 0))],
        out_shape=[jax.ShapeDtypeStruct((batch * seq, GV), BF16),
                   jax.ShapeDtypeStruct((batch * seq, RV), BF16),
                   jax.ShapeDtypeStruct((batch, GLA_HEADS, GLA_DK, GLA_DV), F32),
                   jax.ShapeDtypeStruct((batch, RET_HEADS, RET_DK, RET_DV), F32)],
        compiler_params=pltpu.CompilerParams(dimension_semantics=("parallel", "arbitrary"),
                                             vmem_limit_bytes=VMEM_LIMIT),
        name="mixer_prompt",
    )(p, p, p, p, p, p, p, la, jnp.asarray(d_intra), jnp.asarray(d_read), jnp.asarray(d_write))


def _pad_rows(x, rows):
    return jnp.concatenate([x, jnp.zeros((rows - x.shape[0], x.shape[1]), x.dtype)], axis=0)


def _mixer_sample_kernel(seq, gammas, d_chunk, vr_ref, zr_ref, qr_ref, kr_ref, vg_ref, zg_ref, qkg_ref,
                         la_ref, dread_ref, dwrite_ref,
                         sg_in, sr_in, og_ref, or_ref, sg_ref, sr_ref):
    rows = SAMPLE_ROWS
    n_el = rows // seq
    row1 = lax.broadcasted_iota(jnp.int32, (rows, 1), 0)
    pos1 = row1 % seq
    tcol = lax.broadcasted_iota(jnp.int32, (LANES, LANES), 1)

    def seg_cumsum(x):
        s = 1
        while s < seq:
            x = x + jnp.where(pos1 >= s, pltpu.roll(x, s, 0), 0.0)
            s *= 2
        return x

    def intra(q, k, v, gamma):
        out = jnp.zeros_like(v)
        for d in range(seq):
            kd = k if d == 0 else pltpu.roll(k, d, 0)
            vd = v if d == 0 else pltpu.roll(v, d, 0)
            a = jnp.sum(q * kd, axis=1, keepdims=True) * (gamma ** d)
            out = out + jnp.where(pos1 >= d, a, 0.0) * vd
        return out

    def inter(q, states):
        qb = q.astype(BF16)
        out = None
        for e in range(n_el):
            oe = _dot(qb, states[e].astype(BF16))
            out = oe if out is None else jnp.where(row1 >= e * seq, oe, out)
        return out

    for h in range(GLA_HEADS):
        kq = slice(h * GLA_DK, (h + 1) * GLA_DK)
        kk = slice(GK + h * GLA_DK, GK + (h + 1) * GLA_DK)
        vv = slice(h * GLA_DV, (h + 1) * GLA_DV)
        q = qkg_ref[:, kq].astype(F32) * (GLA_DK ** -0.5)
        k = qkg_ref[:, kk].astype(F32)
        v = vg_ref[:, vv].astype(F32)
        la = la_ref[:, kq]
        b = seg_cumsum(la)
        q_dec = q * jnp.exp(b)
        k_dec = k * jnp.exp(-b)
        states = [sg_in[e, h] for e in range(n_el)]
        o = inter(q_dec, states) + intra(q_dec, k_dec, v, 1.0)
        og_ref[:, vv] = _rms_gate(o, zg_ref[:, vv]).astype(og_ref.dtype)
        b_last = jnp.zeros_like(b)
        for e in range(n_el):
            r = e * seq + seq - 1
            b_last = jnp.where((row1 >= e * seq) & (row1 < (e + 1) * seq),
                               jnp.broadcast_to(b[r:r + 1, :], b.shape), b_last)
        k_out = k * jnp.exp(b_last - b)
        k_outT = _pad_rows(k_out, LANES).T
        laT = _pad_rows(la, LANES).T
        vb = _pad_rows(v, LANES).astype(BF16)
        tl = tcol[:GLA_DK]
        for e in range(n_el):
            sel = (tl >= e * seq) & (tl < (e + 1) * seq)
            g_col = jnp.exp(jnp.sum(jnp.where(sel, laT, 0.0), axis=1, keepdims=True))
            sg_ref[e, h] = g_col * states[e] + _dot(jnp.where(sel, k_outT, 0.0).astype(BF16), vb)

    for h in range(RET_HEADS):
        kq = slice(h * RET_DK, (h + 1) * RET_DK)
        vv = slice(h * RET_DV, (h + 1) * RET_DV)
        q = qr_ref[:, kq].astype(F32)
        k = kr_ref[:, kq].astype(F32)
        v = vr_ref[:, vv].astype(F32)
        states = [sr_in[e, h] for e in range(n_el)]
        o = inter(q, states) * dread_ref[h] + intra(q, k, v, gammas[h])
        or_ref[:, vv] = _group_gate(o, zr_ref[:, vv]).astype(or_ref.dtype)
        kwT = _pad_rows(k * dwrite_ref[h], LANES).T
        vb = _pad_rows(v, LANES).astype(BF16)
        tl = jnp.concatenate([tcol, tcol], axis=0)
        for e in range(n_el):
            sel = (tl >= e * seq) & (tl < (e + 1) * seq)
            sr_ref[e, h] = d_chunk[h] * states[e] + _dot(jnp.where(sel, kwT, 0.0).astype(BF16), vb)


def _mixer_sample(p, la, state_gla, state_ret, batch, seq):
    rows = SAMPLE_ROWS
    n_el = rows // seq
    _, d_read, d_write, d_chunk = _ret_consts(seq)
    gammas = tuple(float(1.0 - 2.0 ** (-5.0 - h)) for h in range(RET_HEADS))
    d_read = np.tile(d_read, (1, n_el, 1))
    d_write = np.tile(d_write, (1, n_el, 1))

    pcol = lambda width, blk: pl.BlockSpec((rows, width), lambda i: (i, blk))
    tok = lambda width: pl.BlockSpec((rows, width), lambda i: (i, 0))
    full3 = lambda shape: pl.BlockSpec(shape, lambda i: (0, 0, 0))
    sg_spec = pl.BlockSpec((n_el, GLA_HEADS, GLA_DK, GLA_DV), lambda i: (i, 0, 0, 0))
    sr_spec = pl.BlockSpec((n_el, RET_HEADS, RET_DK, RET_DV), lambda i: (i, 0, 0, 0))
    return pl.pallas_call(
        functools.partial(_mixer_sample_kernel, seq, gammas, d_chunk),
        grid=(batch * seq // rows,),
        in_specs=[pcol(RV, BLK_VR), pcol(RV, BLK_ZR), pcol(RK, BLK_QR), pcol(RK, BLK_KR),
                  pcol(GV, BLK_VG), pcol(GV, BLK_ZG), pcol(2 * GK, BLK_QKG), tok(GK),
                  full3((RET_HEADS, rows, 1)), full3((RET_HEADS, rows, 1)),
                  sg_spec, sr_spec],
        out_specs=[tok(GV), tok(RV), sg_spec, sr_spec],
        out_shape=[jax.ShapeDtypeStruct((batch * seq, GV), F32),
                   jax.ShapeDtypeStruct((batch * seq, RV), F32),
                   jax.ShapeDtypeStruct(state_gla.shape, F32),
                   jax.ShapeDtypeStruct(state_ret.shape, F32)],
        compiler_params=pltpu.CompilerParams(dimension_semantics=("parallel",),
                                             vmem_limit_bytes=VMEM_LIMIT),
        name="mixer_sample",
    )(p, p, p, p, p, p, p, la, jnp.asarray(d_read), jnp.asarray(d_write), state_gla, state_ret)


def _out_kernel(og_ref, or_ref, mg_ref, mr_ref, x_ref, gate_ref, wbg_ref, wbr_ref, wo_ref, lng_ref, lnb_ref,
                y_ref):
    sub = min(OUT_SUB, y_ref.shape[0])
    blocks = [slice(r0, r0 + sub) for r0 in range(0, y_ref.shape[0], sub)]
    merged = []
    for rs in blocks:
        bg = _dot(og_ref[rs, :].astype(BF16), wbg_ref[...])
        br = _dot(or_ref[rs, :].astype(BF16), wbr_ref[...])
        merged.append((mg_ref[rs, :].astype(F32) * bg + mr_ref[rs, :].astype(F32) * br).astype(BF16))
    for rs, m in zip(blocks, merged):
        out = _dot(m, wo_ref[...])
        gate = gate_ref[...] if gate_ref.shape[0] == 1 else gate_ref[rs, :]
        r = DEEPNORM_ALPHA * x_ref[rs, :] + gate * out
        mu = jnp.mean(r, axis=-1, keepdims=True)
        c = r - mu
        var = jnp.mean(c * c, axis=-1, keepdims=True)
        y_ref[rs, :] = c * lax.rsqrt(var + LN_EPS) * lng_ref[...] + lnb_ref[...]


def _out_stage(og, orr, p, x2d, gate3, wbg, wbr, wo, ln_g, ln_b, *, tm, rows_per_mod):
    n = x2d.shape[0]
    if gate3.shape[1] != 1:
        gate_spec = pl.BlockSpec((None, tm, D_MODEL), lambda i: (0, i, 0))
    else:
        tiles_per_mod = rows_per_mod // tm
        gate_spec = pl.BlockSpec((None, 1, D_MODEL), lambda i: (i // tiles_per_mod, 0, 0))
    full = lambda shape: pl.BlockSpec(shape, lambda i: (0, 0), pipeline_mode=pl.Buffered(1))
    return pl.pallas_call(
        _out_kernel,
        grid=(n // tm,),
        in_specs=[pl.BlockSpec((tm, GV), lambda i: (i, 0)),
                  pl.BlockSpec((tm, RV), lambda i: (i, 0)),
                  pl.BlockSpec((tm, D_MODEL), lambda i: (i, BLK_MG)),
                  pl.BlockSpec((tm, D_MODEL), lambda i: (i, BLK_MR)),
                  pl.BlockSpec((tm, D_MODEL), lambda i: (i, 0)),
                  gate_spec,
                  full((GV, D_MODEL)), full((RV, D_MODEL)), full((D_MODEL, D_MODEL)),
                  full((1, D_MODEL)), full((1, D_MODEL))],
        out_specs=pl.BlockSpec((tm, D_MODEL), lambda i: (i, 0)),
        out_shape=jax.ShapeDtypeStruct((n, D_MODEL), F32),
        compiler_params=pltpu.CompilerParams(dimension_semantics=("parallel",),
                                             vmem_limit_bytes=VMEM_LIMIT),
        name="out_stage",
    )(og, orr, p, p, x2d, gate3, wbg, wbr, wo, ln_g, ln_b)


_W_SRC_BLK = (_O_VR // P_BLK, _O_VR // P_BLK + 1, _O_QR // P_BLK, _O_KR // P_BLK, _O_VG // P_BLK,
              _O_QG // P_BLK, _O_ZR // P_BLK, _O_ZR // P_BLK + 1, _O_ZG // P_BLK, _O_MG // P_BLK,
              _O_MR // P_BLK)
_LR_BLK = _O_LR // P_BLK


def _prep_w_kernel(src_ref, a_ref, b_ref, o_ref, lr_ref):
    blk = src_ref[pl.program_id(0)]
    keep = P_BLK - GLA_LOWRANK

    @pl.when(blk >= _LR_BLK)
    def _():
        o_ref[:keep, :] = a_ref[GLA_LOWRANK:, :].astype(BF16)
        o_ref[keep:, :] = b_ref[...].astype(BF16)

    @pl.when(blk < _LR_BLK)
    def _():
        o_ref[...] = a_ref[...].astype(BF16)

    @pl.when(blk == _LR_BLK)
    def _():
        lr_ref[:GLA_LOWRANK, :] = a_ref[:GLA_LOWRANK, :].astype(BF16)
        lr_ref[GLA_LOWRANK:, :] = jnp.zeros((LR_PAD - GLA_LOWRANK, D_MODEL), BF16)


def _prep_w_in(wt):
    assert all(o % P_BLK in (0, GLA_LOWRANK) for o in (_O_VR, _O_QR, _O_KR, _O_VG, _O_QG, _O_ZR, _O_ZG,
                                                       _O_MG, _O_MR)) and _O_LR % P_BLK == 0
    src = jnp.asarray(_W_SRC_BLK, jnp.int32)
    per_blk = P_BLK // GLA_LOWRANK
    return pl.pallas_call(
        _prep_w_kernel,
        grid_spec=pltpu.PrefetchScalarGridSpec(
            num_scalar_prefetch=1,
            grid=(len(_W_SRC_BLK),),
            in_specs=[pl.BlockSpec((P_BLK, D_MODEL), lambda j, src: (src[j], 0)),
                      pl.BlockSpec((GLA_LOWRANK, D_MODEL), lambda j, src: ((src[j] + 1) * per_blk, 0))],
            out_specs=[pl.BlockSpec((P_BLK, D_MODEL), lambda j, src: (j, 0)),
                       pl.BlockSpec((LR_PAD, D_MODEL), lambda j, src: (0, 0))]),
        out_shape=[jax.ShapeDtypeStruct((P_COLS, D_MODEL), BF16),
                   jax.ShapeDtypeStruct((LR_PAD, D_MODEL), BF16)],
        compiler_params=pltpu.CompilerParams(dimension_semantics=("arbitrary",),
                                             vmem_limit_bytes=VMEM_LIMIT),
        name="prep_w",
    )(src, wt, wt)


def kernel(x_prompt, x_sample, state_gla, state_ret, c_prompt, c_sample, w_ada, b_ada, w_in, w_lr2, b_lr2,
           gla_norm_g, ret_norm_g, w_branch_gla, w_branch_ret, w_out, ln_g, ln_b):
    assert w_ada.shape[0] == 1, "one layer"
    bp, tp, _ = x_prompt.shape
    bs, ts, _ = x_sample.shape

    mod = _ada(jnp.concatenate([c_prompt, c_sample], axis=0), w_ada[0], b_ada)
    shift, scale1, gate = mod[:, :D_MODEL], mod[:, D_MODEL:2 * D_MODEL], mod[:, 2 * D_MODEL:]
    per_group = lambda a: a[:bp][:, None, :]
    per_token = lambda a: jnp.repeat(a[bp:], ts, axis=0)[None]

    w_main, w_lr = _prep_w_in(w_in[0].T)
    w_lr2p = jnp.pad(w_lr2[0], ((0, LR_PAD - GLA_LOWRANK), (0, 0))).astype(BF16)
    wbg = (gla_norm_g[0][:, None] * w_branch_gla[0]).astype(BF16)
    wbr = (ret_norm_g[0][:, None] * w_branch_ret[0]).astype(BF16)
    wo = w_out[0].astype(BF16)

    xp = x_prompt.reshape(bp * tp, D_MODEL)
    cos_p, sin_p = _rope_tables(jnp.arange(tp, dtype=jnp.int32))
    p_p, la_p = _proj(xp, per_group(scale1), per_group(shift), w_main, w_lr, w_lr2p, b_lr2, cos_p, sin_p,
                      tm=1024, rows_per_mod=tp, p_dtype=BF16, cum_chunk=CHUNK)
    og_p, or_p, sg_p, sr_p = _mixer_prompt(p_p, la_p, bp, tp)
    y_p = _out_stage(og_p, or_p, p_p, xp, per_group(gate), wbg, wbr, wo, ln_g, ln_b, tm=1024, rows_per_mod=tp)

    xs = x_sample.reshape(bs * ts, D_MODEL)
    cos_s, sin_s = _rope_tables(PAST_LEN + jnp.arange(bs * ts, dtype=jnp.int32) % ts)
    p_s, la_s = _proj(xs, per_token(scale1), per_token(shift), w_main, w_lr, w_lr2p, b_lr2, cos_s, sin_s,
                      tm=bs * ts, rows_per_mod=ts, p_dtype=BF16)
    og_s, or_s, sg_s, sr_s = _mixer_sample(p_s, la_s, state_gla[0], state_ret[0], bs, ts)
    y_s = _out_stage(og_s, or_s, p_s, xs, per_token(gate), wbg, wbr, wo, ln_g, ln_b,
                     tm=bs * ts, rows_per_mod=ts)

    return (y_p.reshape(bp, tp, D_MODEL), y_s.reshape(bs, ts, D_MODEL),
            sg_p[None], sr_p[None], sg_s[None], sr_s[None])
```

```python
import functools

import numpy as np
import jax
import jax.numpy as jnp
from jax import lax
from jax.experimental import pallas as pl
from jax.experimental.pallas import tpu as pltpu

F32 = jnp.float32
BF16 = jnp.bfloat16

D_MODEL = 1024
PAST_LEN = 16384
GLA_HEADS, GLA_DK, GLA_DV = 4, 128, 256
RET_HEADS, RET_DK, RET_DV = 4, 256, 512
GK, GV = GLA_HEADS * GLA_DK, GLA_HEADS * GLA_DV
RK, RV = RET_HEADS * RET_DK, RET_HEADS * RET_DV
GLA_LOWRANK = 16
GLA_TAU = 16.0
GLA_SUB = 16
ROPE_BASE = 10000.0
DEEPNORM_ALPHA = 2.0 ** 0.25
LN_EPS = 1e-5
HEAD_NORM_EPS = 1e-5

CHUNK = 128
MIXER_CHUNKS = 4
SAMPLE_ROWS = 16
LANES = 128
LR_PAD = 256

P_BLK = 1024
P_COLS = 11 * P_BLK
BLK_VR, BLK_ZR = 0, 3
BLK_QR, BLK_KR, BLK_VG, BLK_QKG, BLK_ZG, BLK_MG, BLK_MR = 2, 3, 4, 5, 8, 9, 10
P_RUNS = ((0, "plain"), (BLK_QR * P_BLK, "rot_q"), (BLK_KR * P_BLK, "rot_k"), (BLK_VG * P_BLK, "plain"),
          (6 * P_BLK, "silu"), (BLK_MG * P_BLK, "sigmoid"), (P_COLS, None))
PROJ_TN = P_COLS // 4
PROJ_SUB = 1024
OUT_SUB = 256
_O_QG, _O_KG, _O_VG, _O_ZG, _O_LR = 0, GK, 2 * GK, 2 * GK + GV, 2 * GK + 2 * GV
_O_QR = _O_LR + GLA_LOWRANK
_O_KR = _O_QR + RK
_O_VR = _O_KR + RK
_O_ZR = _O_VR + RV
_O_MG = _O_ZR + RV
_O_MR = _O_MG + D_MODEL
_O_END = _O_MR + D_MODEL

STATE_SLOTS = 3
VMEM_LIMIT = 56 * 1024 * 1024


def _dot(a, b):
    return jnp.dot(a, b, preferred_element_type=F32)


def _dot_nt(a, b):
    return lax.dot_general(a, b, (((1,), (1,)), ((), ())), preferred_element_type=F32)


def _log_sigmoid(x):
    return jnp.minimum(x, 0.0) - jnp.log(1.0 + jnp.exp(-jnp.abs(x)))


def _sigmoid(x):
    return 0.5 * jnp.tanh(0.5 * x) + 0.5


def _silu(x):
    return x * _sigmoid(x)


def _ada_kernel(c_ref, w_ref, b_ref, o_ref):
    one = jnp.where(pl.program_id(0) == 1, 1.0, 0.0)
    o_ref[...] = _dot(c_ref[...].astype(BF16), w_ref[...].astype(BF16)) + (b_ref[...] + one)


def _ada(c, w_ada, b_ada):
    n = c.shape[0]
    return pl.pallas_call(
        _ada_kernel,
        grid=(3,),
        in_specs=[pl.BlockSpec((n, D_MODEL), lambda j: (0, 0)),
                  pl.BlockSpec((D_MODEL, D_MODEL), lambda j: (0, j)),
                  pl.BlockSpec((1, D_MODEL), lambda j: (0, j))],
        out_specs=pl.BlockSpec((n, D_MODEL), lambda j: (0, j)),
        out_shape=jax.ShapeDtypeStruct((n, 3 * D_MODEL), F32),
        compiler_params=pltpu.CompilerParams(dimension_semantics=("arbitrary",)),
        name="ada",
    )(c, w_ada, b_ada)


def _proj_kernel(cum_chunk, x_ref, scale1_ref, shift_ref, w_ref, wlr_ref, wlr2_ref, blr2_ref, cos_ref, sin_ref,
                 p_ref, la_ref, h_ref):
    j = pl.program_id(1)

    @pl.when(j == 0)
    def _():
        h = (x_ref[...] * scale1_ref[...] + shift_ref[...]).astype(BF16)
        h_ref[...] = h
        sub = min(PROJ_SUB, h.shape[0]) // 2
        lr = jnp.concatenate([_dot_nt(h[r:r + sub, :], wlr_ref[...]) for r in range(0, h.shape[0], sub)], axis=0)
        pre = _dot(lr.astype(BF16), wlr2_ref[...]) + blr2_ref[...]
        la = _log_sigmoid(pre) * (1.0 / GLA_TAU)
        if cum_chunk:
            for r in range(0, la.shape[0], cum_chunk):
                la_ref[r:r + cum_chunk, :] = _cumsum_rows_mxu(la[r:r + cum_chunk, :])
        else:
            la_ref[...] = la

    def epilogue(kind, a, rs):
        if kind in ("rot_q", "rot_k"):
            scale = 1.0 if kind == "rot_q" else RET_DK ** -0.5
            return _rotary_heads(a, cos_ref[rs, :], sin_ref[rs, :], a.shape[1] // RET_DK, scale)
        return {"plain": lambda v: v, "silu": _silu, "sigmoid": _sigmoid}[kind](a)

    def emit(step):
        tn = p_ref.shape[1]
        lo, hi = step * tn, (step + 1) * tn
        runs = [(max(a, lo) - lo, min(b, hi) - lo, kind)
                for (a, kind), (b, _) in zip(P_RUNS[:-1], P_RUNS[1:]) if max(a, lo) < min(b, hi)]
        sub = min(PROJ_SUB, p_ref.shape[0])
        for r in range(0, p_ref.shape[0], sub):
            rs = slice(r, r + sub)
            acc = _dot_nt(h_ref[rs, :], w_ref[...])
            for a, b, kind in runs:
                p_ref[rs, a:b] = epilogue(kind, acc[:, a:b], rs).astype(p_ref.dtype)

    for step in range(P_COLS // p_ref.shape[1]):
        pl.when(j == step)(functools.partial(emit, step))


def _proj(x2d, scale1_3, shift3, w_main, w_lr, w_lr2, b_lr2, cos, sin, *, tm, rows_per_mod, p_dtype, cum_chunk=0):
    assert cum_chunk == 0 or (tm % cum_chunk == 0 and rows_per_mod % cum_chunk == 0)
    n = x2d.shape[0]
    rope_tiles = cos.shape[0] // tm
    rope_spec = pl.BlockSpec((tm, RET_DK // 2), lambda i, j: (i % rope_tiles, 0))
    per_row = shift3.shape[1] != 1
    if per_row:
        mod_spec = pl.BlockSpec((None, tm, D_MODEL), lambda i, j: (0, i, 0))
    else:
        tiles_per_mod = rows_per_mod // tm
        mod_spec = pl.BlockSpec((None, 1, D_MODEL), lambda i, j: (i // tiles_per_mod, 0, 0))
    tn = PROJ_TN
    assert all(a % RET_DK == 0 for a, _ in P_RUNS) and tn % RET_DK == 0
    return pl.pallas_call(
        functools.partial(_proj_kernel, cum_chunk),
        grid=(n // tm, P_COLS // tn),
        in_specs=[pl.BlockSpec((tm, D_MODEL), lambda i, j: (i, 0)),
                  mod_spec, mod_spec,
                  pl.BlockSpec((tn, D_MODEL), lambda i, j: (j, 0)),
                  pl.BlockSpec((LR_PAD, D_MODEL), lambda i, j: (0, 0)),
                  pl.BlockSpec((LR_PAD, GK), lambda i, j: (0, 0)),
                  pl.BlockSpec((1, GK), lambda i, j: (0, 0)),
                  rope_spec, rope_spec],
        out_specs=[pl.BlockSpec((tm, tn), lambda i, j: (i, j)),
                   pl.BlockSpec((tm, GK), lambda i, j: (i, 0))],
        out_shape=[jax.ShapeDtypeStruct((n, P_COLS), p_dtype),
                   jax.ShapeDtypeStruct((n, GK), F32)],
        scratch_shapes=[pltpu.VMEM((tm, D_MODEL), BF16)],
        compiler_params=pltpu.CompilerParams(dimension_semantics=("parallel", "arbitrary"),
                                             vmem_limit_bytes=VMEM_LIMIT),
        name="proj",
    )(x2d, scale1_3, shift3, w_main, w_lr, w_lr2, b_lr2, cos, sin)


def _rms_gate(o, sz):
    o = o * lax.rsqrt(jnp.mean(o * o, axis=-1, keepdims=True) + HEAD_NORM_EPS)
    return o.astype(sz.dtype) * sz


def _group_gate(o, sz):
    mu = jnp.mean(o, axis=-1, keepdims=True)
    c = o - mu
    var = jnp.mean(c * c, axis=-1, keepdims=True)
    return (c * lax.rsqrt(var + HEAD_NORM_EPS)).astype(sz.dtype) * sz


def _cumsum_rows_mxu(x):
    n = x.shape[0]
    row = lax.broadcasted_iota(jnp.int32, (n, n), 0)
    col = lax.broadcasted_iota(jnp.int32, (n, n), 1)
    tri = jnp.where(col <= row, 1.0, 0.0).astype(BF16)
    hi = x.astype(BF16)
    rest = x - hi.astype(F32)
    mid = rest.astype(BF16)
    lo = (rest - mid.astype(F32)).astype(BF16)
    return _dot(jnp.concatenate([tri, tri, tri], axis=1), jnp.concatenate([hi, mid, lo], axis=0))


def _block_edge(b, w, prev):
    n, m = b.shape
    pieces = []
    for i in range(n // w):
        r = i * w - 1 if prev else i * w + w - 1
        if r < 0:
            pieces.append(jnp.zeros((w, m), F32))
        else:
            pieces.append(jnp.broadcast_to(b[r:r + 1, :], (w, m)))
    return jnp.concatenate(pieces, axis=0)


def _gla_levels(c):
    w = GLA_SUB
    while w < c:
        yield w
        w *= 2


def _gla_operands(q, k, b):
    c = q.shape[0]
    btot = b[c - 1:c, :]
    rel = b - _block_edge(b, GLA_SUB, True)
    ops = {"q_in": (q * jnp.exp(b)).astype(BF16),
           "k_out": (k * jnp.exp(btot - b)).astype(BF16),
           "qd": (q * jnp.exp(rel)).astype(BF16),
           "kd": (k * jnp.exp(-rel)).astype(BF16)}
    for w in _gla_levels(c):
        if w > GLA_SUB:
            ops["q%d" % w] = (q * jnp.exp(b - _block_edge(b, w, True))).astype(BF16)
        ops["k%d" % w] = (k * jnp.exp(_block_edge(b, w, False) - b)).astype(BF16)
    return ops


def _gla_level_map(c):
    row = lax.broadcasted_iota(jnp.int32, (c, c), 0)
    col = lax.broadcasted_iota(jnp.int32, (c, c), 1)
    sh = GLA_SUB.bit_length() - 1
    level = jnp.where(((row >> sh) == (col >> sh)) & (col <= row), GLA_SUB - 1, 0)
    for w in _gla_levels(c):
        sh = w.bit_length() - 1
        rb, cb = row >> sh, col >> sh
        level = jnp.where(((rb & 1) == 1) & (cb == rb - 1), w, level)
    return level


def _gla_scores(ops, hs, level, c):
    a = jnp.where(level == GLA_SUB - 1, _dot_nt(ops["qd"][:, hs], ops["kd"][:, hs]), 0.0)
    for w in _gla_levels(c):
        qw = ops["qd"] if w == GLA_SUB else ops["q%d" % w]
        a = jnp.where(level == w, _dot_nt(qw[:, hs], ops["k%d" % w][:, hs]), a)
    return a.astype(BF16)


def _rotary_heads(x, cos, sin, heads, scale=1.0):
    half = cos.shape[1]
    if scale != 1.0:
        cos, sin = cos * scale, sin * scale
    cos_t = jnp.concatenate([cos] * (2 * heads), axis=1)
    sin_t = jnp.concatenate([-sin, sin] * heads, axis=1)
    partner = [i + 1 - 2 * (i % 2) for i in range(2 * heads)]
    swapped = jnp.concatenate([x[:, j * half:(j + 1) * half] for j in partner], axis=1)
    return x * cos_t + swapped * sin_t


def _mixer_prompt_kernel(d_chunk, vr_ref, zr_ref, qr_ref, kr_ref, vg_ref, zg_ref, qkg_ref, b_ref,
                         dintra_ref, dread_ref, dwrite_ref,
                         og_ref, or_ref, sg_ref, sr_ref):
    @pl.when(pl.program_id(1) == 0)
    def _():
        sg_ref[...] = jnp.zeros_like(sg_ref)
        sr_ref[...] = jnp.zeros_like(sr_ref)

    c = CHUNK
    g_sl = [slice(h * GLA_DK, (h + 1) * GLA_DK) for h in range(GLA_HEADS)]
    gv_sl = [slice(h * GLA_DV, (h + 1) * GLA_DV) for h in range(GLA_HEADS)]
    r_sl = [slice(h * RET_DK, (h + 1) * RET_DK) for h in range(RET_HEADS)]
    rv_sl = [slice(h * RET_DV, (h + 1) * RET_DV) for h in range(RET_HEADS)]

    level = _gla_level_map(c)
    d_read = dread_ref[...].astype(BF16)
    d_write = dwrite_ref[...].astype(BF16)

    def chunk(i, carry):
        rows = pl.ds(pl.multiple_of(i * c, c), c)
        b = b_ref[rows, :]
        qb = qr_ref[rows, :]
        kb = kr_ref[rows, :]
        q_read = qb * d_read
        k_write = kb * d_write
        g_rows = jnp.exp(b[c - 1:c, :])
        g_cols = [jnp.broadcast_to(g_rows[:, s], (GLA_DK, GLA_DK)).T[:, :1] for s in g_sl]
        gops = _gla_operands(qkg_ref[rows, :GK].astype(F32) * (GLA_DK ** -0.5),
                             qkg_ref[rows, GK:].astype(F32), b)

        a_g = [_gla_scores(gops, s, level, c) for s in g_sl]
        a_r = [(_dot_nt(qb[:, s], kb[:, s]) * dintra_ref[h]).astype(BF16) for h, s in enumerate(r_sl)]

        for h in range(GLA_HEADS):
            o = _dot(gops["q_in"][:, g_sl[h]], sg_ref[h].astype(BF16)) + _dot(a_g[h], vg_ref[rows, gv_sl[h]])
            og_ref[rows, gv_sl[h]] = _rms_gate(o, zg_ref[rows, gv_sl[h]])
        for h in range(RET_HEADS):
            o = _dot(a_r[h], vr_ref[rows, rv_sl[h]]) + _dot(q_read[:, r_sl[h]], sr_ref[h].astype(BF16))
            or_ref[rows, rv_sl[h]] = _group_gate(o, zr_ref[rows, rv_sl[h]])

        for h in range(GLA_HEADS):
            sg_ref[h] = g_cols[h] * sg_ref[h] + _dot(gops["k_out"][:, g_sl[h]].T, vg_ref[rows, gv_sl[h]])
        for h in range(RET_HEADS):
            sr_ref[h] = d_chunk[h] * sr_ref[h] + _dot(k_write[:, r_sl[h]].T, vr_ref[rows, rv_sl[h]])
        return carry

    lax.fori_loop(0, MIXER_CHUNKS, chunk, 0)


def _ret_consts(length):
    h = np.arange(RET_HEADS, dtype=np.float64)
    log_gamma = np.log1p(-np.exp2(-5.0 - h))
    idx = np.arange(length, dtype=np.float64)
    diff = idx[:, None] - idx[None, :]
    d_intra = np.where(diff[None] >= 0, np.exp(np.maximum(diff, 0.0)[None] * log_gamma[:, None, None]), 0.0)
    d_read = np.exp((idx + 1.0)[None, :] * log_gamma[:, None])
    d_write = np.exp((length - 1.0 - idx)[None, :] * log_gamma[:, None])
    d_chunk = np.exp(length * log_gamma)
    return (d_intra.astype(np.float32), d_read.astype(np.float32)[..., None],
            d_write.astype(np.float32)[..., None], tuple(float(np.float32(x)) for x in d_chunk))


def _rope_tables(pos):
    half = RET_DK // 2
    inv_freq = ROPE_BASE ** (-jnp.arange(half, dtype=F32) / half)
    ang = pos.astype(F32)[:, None] * inv_freq[None, :]
    return jnp.cos(ang), jnp.sin(ang)


def _mixer_prompt(p, la, batch, seq):
    assert p.dtype == BF16 and seq % (CHUNK * MIXER_CHUNKS) == 0
    d_intra, d_read, d_write, d_chunk = _ret_consts(CHUNK)
    d_read = np.repeat(d_read[..., 0].T, RET_DK, axis=1)
    d_write = np.repeat(d_write[..., 0].T, RET_DK, axis=1)

    rows = CHUNK * MIXER_CHUNKS
    nstep = seq // rows

    def pcol(width, blk):
        return pl.BlockSpec((rows, width), lambda b, t: (b * nstep + t, blk))

    const3 = lambda shape: pl.BlockSpec(shape, lambda b, t: (0, 0, 0))
    tok = lambda width: pl.BlockSpec((rows, width), lambda b, t: (b * nstep + t, 0))
    return pl.pallas_call(
        functools.partial(_mixer_prompt_kernel, d_chunk),
        grid=(batch, nstep),
        in_specs=[pcol(RV, BLK_VR), pcol(RV, BLK_ZR), pcol(RK, BLK_QR), pcol(RK, BLK_KR),
                  pcol(GV, BLK_VG), pcol(GV, BLK_ZG), pcol(2 * GK, BLK_QKG), tok(GK),
                  const3((RET_HEADS, CHUNK, CHUNK)),
                  pl.BlockSpec((CHUNK, RK), lambda b, t: (0, 0)),
                  pl.BlockSpec((CHUNK, RK), lambda b, t: (0, 0))],
        out_specs=[tok(GV), tok(RV),
                   pl.BlockSpec((None, GLA_HEADS, GLA_DK, GLA_DV), lambda b, t: (b, 0, 0, 0)),
                   pl.BlockSpec((None, RET_HEADS, RET_DK, RET_DV), lambda b, t: (b, 0, 0, 0))],
        out_shape=[jax.ShapeDtypeStruct((batch * seq, GV), BF16),
                   jax.ShapeDtypeStruct((batch * seq, RV), BF16),
                   jax.ShapeDtypeStruct((batch, GLA_HEADS, GLA_DK, GLA_DV), F32),
                   jax.ShapeDtypeStruct((batch, RET_HEADS, RET_DK, RET_DV), F32)],
        compiler_params=pltpu.CompilerParams(dimension_semantics=("parallel", "arbitrary"),
                                             vmem_limit_bytes=VMEM_LIMIT),
        name="mixer_prompt",
    )(p, p, p, p, p, p, p, la, jnp.asarray(d_intra), jnp.asarray(d_read), jnp.asarray(d_write))


def _pad_rows(x, rows):
    return jnp.concatenate([x, jnp.zeros((rows - x.shape[0], x.shape[1]), x.dtype)], axis=0)


def _mixer_sample_kernel(seq, gammas, d_chunk, vr_ref, zr_ref, qr_ref, kr_ref, vg_ref, zg_ref, qkg_ref,
                         la_ref, dread_ref, dwrite_ref,
                         sg_hbm, sr_hbm, og_ref, or_ref, sg_ref, sr_ref, sg_buf, sr_buf, sem):
    rows = SAMPLE_ROWS
    n_el = rows // seq
    step = pl.program_id(0)
    ahead = STATE_SLOTS - 1

    def state_copies(s):
        slot = s % STATE_SLOTS
        src = pl.ds(s * n_el, n_el)
        return (pltpu.make_async_copy(sg_hbm.at[src], sg_buf.at[slot], sem.at[0, slot]),
                pltpu.make_async_copy(sr_hbm.at[src], sr_buf.at[slot], sem.at[1, slot]))

    @pl.when(step == 0)
    def _():
        for s in range(ahead):
            for cp in state_copies(s):
                cp.start()

    @pl.when(step + ahead < pl.num_programs(0))
    def _():
        for cp in state_copies(step + ahead):
            cp.start()

    for cp in state_copies(step):
        cp.wait()
    sg_in = sg_buf.at[step % STATE_SLOTS]
    sr_in = sr_buf.at[step % STATE_SLOTS]
    row1 = lax.broadcasted_iota(jnp.int32, (rows, 1), 0)
    pos1 = row1 % seq
    tcol = lax.broadcasted_iota(jnp.int32, (LANES, LANES), 1)

    def seg_cumsum(x):
        s = 1
        while s < seq:
            x = x + jnp.where(pos1 >= s, pltpu.roll(x, s, 0), 0.0)
            s *= 2
        return x

    def intra(q, k, v, gamma):
        out = jnp.zeros_like(v)
        for d in range(seq):
            kd = k if d == 0 else pltpu.roll(k, d, 0)
            vd = v if d == 0 else pltpu.roll(v, d, 0)
            a = jnp.sum(q * kd, axis=1, keepdims=True) * (gamma ** d)
            out = out + jnp.where(pos1 >= d, a, 0.0) * vd
        return out

    def inter(q, states):
        qb = q.astype(BF16)
        out = None
        for e in range(n_el):
            oe = _dot(qb, states[e].astype(BF16))
            out = oe if out is None else jnp.where(row1 >= e * seq, oe, out)
        return out

    for h in range(GLA_HEADS):
        kq = slice(h * GLA_DK, (h + 1) * GLA_DK)
        kk = slice(GK + h * GLA_DK, GK + (h + 1) * GLA_DK)
        vv = slice(h * GLA_DV, (h + 1) * GLA_DV)
        q = qkg_ref[:, kq].astype(F32) * (GLA_DK ** -0.5)
        k = qkg_ref[:, kk].astype(F32)
        v = vg_ref[:, vv].astype(F32)
        la = la_ref[:, kq]
        b = seg_cumsum(la)
        q_dec = q * jnp.exp(b)
        k_dec = k * jnp.exp(-b)
        states = [sg_in[e, h] for e in range(n_el)]
        o = inter(q_dec, states) + intra(q_dec, k_dec, v, 1.0)
        og_ref[:, vv] = _rms_gate(o, zg_ref[:, vv]).astype(og_ref.dtype)
        b_last = jnp.zeros_like(b)
        for e in range(n_el):
            r = e * seq + seq - 1
            b_last = jnp.where((row1 >= e * seq) & (row1 < (e + 1) * seq),
                               jnp.broadcast_to(b[r:r + 1, :], b.shape), b_last)
        k_out = k * jnp.exp(b_last - b)
        k_outT = _pad_rows(k_out, LANES).T
        laT = _pad_rows(la, LANES).T
        vb = _pad_rows(v, LANES).astype(BF16)
        tl = tcol[:GLA_DK]
        for e in range(n_el):
            sel = (tl >= e * seq) & (tl < (e + 1) * seq)
            g_col = jnp.exp(jnp.sum(jnp.where(sel, laT, 0.0), axis=1, keepdims=True))
            sg_ref[e, h] = g_col * states[e] + _dot(jnp.where(sel, k_outT, 0.0).astype(BF16), vb)

    for h in range(RET_HEADS):
        kq = slice(h * RET_DK, (h + 1) * RET_DK)
        vv = slice(h * RET_DV, (h + 1) * RET_DV)
        q = qr_ref[:, kq].astype(F32)
        k = kr_ref[:, kq].astype(F32)
        v = vr_ref[:, vv].astype(F32)
        states = [sr_in[e, h] for e in range(n_el)]
        o = inter(q, states) * dread_ref[h] + intra(q, k, v, gammas[h])
        or_ref[:, vv] = _group_gate(o, zr_ref[:, vv]).astype(or_ref.dtype)
        kwT = _pad_rows(k * dwrite_ref[h], LANES).T
        vb = _pad_rows(v, LANES).astype(BF16)
        tl = jnp.concatenate([tcol, tcol], axis=0)
        for e in range(n_el):
            sel = (tl >= e * seq) & (tl < (e + 1) * seq)
            sr_ref[e, h] = d_chunk[h] * states[e] + _dot(jnp.where(sel, kwT, 0.0).astype(BF16), vb)


def _mixer_sample(p, la, state_gla, state_ret, batch, seq):
    rows = SAMPLE_ROWS
    n_el = rows // seq
    _, d_read, d_write, d_chunk = _ret_consts(seq)
    gammas = tuple(float(1.0 - 2.0 ** (-5.0 - h)) for h in range(RET_HEADS))
    d_read = np.tile(d_read, (1, n_el, 1))
    d_write = np.tile(d_write, (1, n_el, 1))

    pcol = lambda width, blk: pl.BlockSpec((rows, width), lambda i: (i, blk))
    tok = lambda width: pl.BlockSpec((rows, width), lambda i: (i, 0))
    full3 = lambda shape: pl.BlockSpec(shape, lambda i: (0, 0, 0))
    sg_blk = (n_el, GLA_HEADS, GLA_DK, GLA_DV)
    sr_blk = (n_el, RET_HEADS, RET_DK, RET_DV)
    sg_spec = pl.BlockSpec(sg_blk, lambda i: (i, 0, 0, 0))
    sr_spec = pl.BlockSpec(sr_blk, lambda i: (i, 0, 0, 0))
    in_hbm = pl.BlockSpec(memory_space=pl.ANY)
    steps = batch * seq // rows
    assert steps >= STATE_SLOTS and state_gla.dtype == F32 and state_ret.dtype == F32
    return pl.pallas_call(
        functools.partial(_mixer_sample_kernel, seq, gammas, d_chunk),
        grid=(steps,),
        in_specs=[pcol(RV, BLK_VR), pcol(RV, BLK_ZR), pcol(RK, BLK_QR), pcol(RK, BLK_KR),
                  pcol(GV, BLK_VG), pcol(GV, BLK_ZG), pcol(2 * GK, BLK_QKG), tok(GK),
                  full3((RET_HEADS, rows, 1)), full3((RET_HEADS, rows, 1)),
                  in_hbm, in_hbm],
        out_specs=[tok(GV), tok(RV), sg_spec, sr_spec],
        out_shape=[jax.ShapeDtypeStruct((batch * seq, GV), F32),
                   jax.ShapeDtypeStruct((batch * seq, RV), F32),
                   jax.ShapeDtypeStruct(state_gla.shape, F32),
                   jax.ShapeDtypeStruct(state_ret.shape, F32)],
        scratch_shapes=[pltpu.VMEM((STATE_SLOTS,) + sg_blk, F32),
                        pltpu.VMEM((STATE_SLOTS,) + sr_blk, F32),
                        pltpu.SemaphoreType.DMA((2, STATE_SLOTS))],
        compiler_params=pltpu.CompilerParams(dimension_semantics=("arbitrary",),
                                             vmem_limit_bytes=VMEM_LIMIT),
        name="mixer_sample",
    )(p, p, p, p, p, p, p, la, jnp.asarray(d_read), jnp.asarray(d_write), state_gla, state_ret)


def _out_kernel(og_ref, or_ref, mg_ref, mr_ref, x_ref, gate_ref, wbg_ref, wbr_ref, wo_ref, lng_ref, lnb_ref,
                y_ref):
    sub = min(OUT_SUB, y_ref.shape[0])
    blocks = [slice(r0, r0 + sub) for r0 in range(0, y_ref.shape[0], sub)]
    merged = []
    for rs in blocks:
        bg = _dot(og_ref[rs, :].astype(BF16), wbg_ref[...])
        br = _dot(or_ref[rs, :].astype(BF16), wbr_ref[...])
        merged.append((mg_ref[rs, :].astype(F32) * bg + mr_ref[rs, :].astype(F32) * br).astype(BF16))
    for rs, m in zip(blocks, merged):
        out = _dot(m, wo_ref[...])
        gate = gate_ref[...] if gate_ref.shape[0] == 1 else gate_ref[rs, :]
        r = DEEPNORM_ALPHA * x_ref[rs, :] + gate * out
        mu = jnp.mean(r, axis=-1, keepdims=True)
        c = r - mu
        var = jnp.mean(c * c, axis=-1, keepdims=True)
        y_ref[rs, :] = c * lax.rsqrt(var + LN_EPS) * lng_ref[...] + lnb_ref[...]


def _out_stage(og, orr, p, x2d, gate3, wbg, wbr, wo, ln_g, ln_b, *, tm, rows_per_mod):
    n = x2d.shape[0]
    if gate3.shape[1] != 1:
        gate_spec = pl.BlockSpec((None, tm, D_MODEL), lambda i: (0, i, 0))
    else:
        tiles_per_mod = rows_per_mod // tm
        gate_spec = pl.BlockSpec((None, 1, D_MODEL), lambda i: (i // tiles_per_mod, 0, 0))
    full = lambda shape: pl.BlockSpec(shape, lambda i: (0, 0), pipeline_mode=pl.Buffered(1))
    return pl.pallas_call(
        _out_kernel,
        grid=(n // tm,),
        in_specs=[pl.BlockSpec((tm, GV), lambda i: (i, 0)),
                  pl.BlockSpec((tm, RV), lambda i: (i, 0)),
                  pl.BlockSpec((tm, D_MODEL), lambda i: (i, BLK_MG)),
                  pl.BlockSpec((tm, D_MODEL), lambda i: (i, BLK_MR)),
                  pl.BlockSpec((tm, D_MODEL), lambda i: (i, 0)),
                  gate_spec,
                  full((GV, D_MODEL)), full((RV, D_MODEL)), full((D_MODEL, D_MODEL)),
                  full((1, D_MODEL)), full((1, D_MODEL))],
        out_specs=pl.BlockSpec((tm, D_MODEL), lambda i: (i, 0)),
        out_shape=jax.ShapeDtypeStruct((n, D_MODEL), F32),
        compiler_params=pltpu.CompilerParams(dimension_semantics=("parallel",),
                                             vmem_limit_bytes=VMEM_LIMIT),
        name="out_stage",
    )(og, orr, p, p, x2d, gate3, wbg, wbr, wo, ln_g, ln_b)


_W_SRC_BLK = (_O_VR // P_BLK, _O_VR // P_BLK + 1, _O_QR // P_BLK, _O_KR // P_BLK, _O_VG // P_BLK,
              _O_QG // P_BLK, _O_ZR // P_BLK, _O_ZR // P_BLK + 1, _O_ZG // P_BLK, _O_MG // P_BLK,
              _O_MR // P_BLK)
_LR_BLK = _O_LR // P_BLK


def _prep_w_kernel(src_ref, a_ref, b_ref, o_ref, lr_ref):
    blk = src_ref[pl.program_id(0)]
    keep = P_BLK - GLA_LOWRANK

    @pl.when(blk >= _LR_BLK)
    def _():
        o_ref[:keep, :] = a_ref[GLA_LOWRANK:, :].astype(BF16)
        o_ref[keep:, :] = b_ref[...].astype(BF16)

    @pl.when(blk < _LR_BLK)
    def _():
        o_ref[...] = a_ref[...].astype(BF16)

    @pl.when(blk == _LR_BLK)
    def _():
        lr_ref[:GLA_LOWRANK, :] = a_ref[:GLA_LOWRANK, :].astype(BF16)
        lr_ref[GLA_LOWRANK:, :] = jnp.zeros((LR_PAD - GLA_LOWRANK, D_MODEL), BF16)


def _prep_w_in(wt):
    assert all(o % P_BLK in (0, GLA_LOWRANK) for o in (_O_VR, _O_QR, _O_KR, _O_VG, _O_QG, _O_ZR, _O_ZG,
                                                       _O_MG, _O_MR)) and _O_LR % P_BLK == 0
    src = jnp.asarray(_W_SRC_BLK, jnp.int32)
    per_blk = P_BLK // GLA_LOWRANK
    return pl.pallas_call(
        _prep_w_kernel,
        grid_spec=pltpu.PrefetchScalarGridSpec(
            num_scalar_prefetch=1,
            grid=(len(_W_SRC_BLK),),
            in_specs=[pl.BlockSpec((P_BLK, D_MODEL), lambda j, src: (src[j], 0)),
                      pl.BlockSpec((GLA_LOWRANK, D_MODEL), lambda j, src: ((src[j] + 1) * per_blk, 0))],
            out_specs=[pl.BlockSpec((P_BLK, D_MODEL), lambda j, src: (j, 0)),
                       pl.BlockSpec((LR_PAD, D_MODEL), lambda j, src: (0, 0))]),
        out_shape=[jax.ShapeDtypeStruct((P_COLS, D_MODEL), BF16),
                   jax.ShapeDtypeStruct((LR_PAD, D_MODEL), BF16)],
        compiler_params=pltpu.CompilerParams(dimension_semantics=("arbitrary",),
                                             vmem_limit_bytes=VMEM_LIMIT),
        name="prep_w",
    )(src, wt, wt)


def kernel(x_prompt, x_sample, state_gla, state_ret, c_prompt, c_sample, w_ada, b_ada, w_in, w_lr2, b_lr2,
           gla_norm_g, ret_norm_g, w_branch_gla, w_branch_ret, w_out, ln_g, ln_b):
    assert w_ada.shape[0] == 1, "one layer"
    bp, tp, _ = x_prompt.shape
    bs, ts, _ = x_sample.shape

    mod = _ada(jnp.concatenate([c_prompt, c_sample], axis=0), w_ada[0], b_ada)
    shift, scale1, gate = mod[:, :D_MODEL], mod[:, D_MODEL:2 * D_MODEL], mod[:, 2 * D_MODEL:]
    per_group = lambda a: a[:bp][:, None, :]
    per_token = lambda a: jnp.repeat(a[bp:], ts, axis=0)[None]

    w_main, w_lr = _prep_w_in(w_in[0].T)
    w_lr2p = jnp.pad(w_lr2[0], ((0, LR_PAD - GLA_LOWRANK), (0, 0))).astype(BF16)
    wbg = (gla_norm_g[0][:, None] * w_branch_gla[0]).astype(BF16)
    wbr = (ret_norm_g[0][:, None] * w_branch_ret[0]).astype(BF16)
    wo = w_out[0].astype(BF16)

    xp = x_prompt.reshape(bp * tp, D_MODEL)
    cos_p, sin_p = _rope_tables(jnp.arange(tp, dtype=jnp.int32))
    p_p, la_p = _proj(xp, per_group(scale1), per_group(shift), w_main, w_lr, w_lr2p, b_lr2, cos_p, sin_p,
                      tm=1024, rows_per_mod=tp, p_dtype=BF16, cum_chunk=CHUNK)
    og_p, or_p, sg_p, sr_p = _mixer_prompt(p_p, la_p, bp, tp)
    y_p = _out_stage(og_p, or_p, p_p, xp, per_group(gate), wbg, wbr, wo, ln_g, ln_b, tm=1024, rows_per_mod=tp)

    xs = x_sample.reshape(bs * ts, D_MODEL)
    cos_s, sin_s = _rope_tables(PAST_LEN + jnp.arange(bs * ts, dtype=jnp.int32) % ts)
    p_s, la_s = _proj(xs, per_token(scale1), per_token(shift), w_main, w_lr, w_lr2p, b_lr2, cos_s, sin_s,
                      tm=bs * ts, rows_per_mod=ts, p_dtype=BF16)
    og_s, or_s, sg_s, sr_s = _mixer_sample(p_s, la_s, state_gla[0], state_ret[0], bs, ts)
    y_s = _out_stage(og_s, or_s, p_s, xs, per_token(gate), wbg, wbr, wo, ln_g, ln_b,
                     tm=bs * ts, rows_per_mod=ts)

    return (y_p.reshape(bp, tp, D_MODEL), y_s.reshape(bs, ts, D_MODEL),
            sg_p[None], sr_p[None], sg_s[None], sr_s[None])
```

```python
import functools

import numpy as np
import jax
import jax.numpy as jnp
from jax import lax
from jax.experimental import pallas as pl
from jax.experimental.pallas import tpu as pltpu

F32 = jnp.float32
BF16 = jnp.bfloat16

D_MODEL = 1024
PAST_LEN = 16384
GLA_HEADS, GLA_DK, GLA_DV = 4, 128, 256
RET_HEADS, RET_DK, RET_DV = 4, 256, 512
GK, GV = GLA_HEADS * GLA_DK, GLA_HEADS * GLA_DV
RK, RV = RET_HEADS * RET_DK, RET_HEADS * RET_DV
GLA_LOWRANK = 16
GLA_TAU = 16.0
GLA_SUB = 16
ROPE_BASE = 10000.0
DEEPNORM_ALPHA = 2.0 ** 0.25
LN_EPS = 1e-5
HEAD_NORM_EPS = 1e-5

CHUNK = 128
MIXER_CHUNKS = 4
SAMPLE_ROWS = 16
LANES = 128
LR_PAD = 256

P_BLK = 1024
P_COLS = 11 * P_BLK
BLK_VR, BLK_ZR = 0, 3
BLK_QR, BLK_KR, BLK_VG, BLK_QKG, BLK_ZG, BLK_MG, BLK_MR = 2, 3, 4, 5, 8, 9, 10
P_RUNS = ((0, "plain"), (BLK_QR * P_BLK, "rot_q"), (BLK_KR * P_BLK, "rot_k"), (BLK_VG * P_BLK, "plain"),
          (6 * P_BLK, "silu"), (BLK_MG * P_BLK, "sigmoid"), (P_COLS, None))
PROJ_TN = P_COLS // 4
PROJ_SUB = 1024
OUT_SUB = 256
_O_QG, _O_KG, _O_VG, _O_ZG, _O_LR = 0, GK, 2 * GK, 2 * GK + GV, 2 * GK + 2 * GV
_O_QR = _O_LR + GLA_LOWRANK
_O_KR = _O_QR + RK
_O_VR = _O_KR + RK
_O_ZR = _O_VR + RV
_O_MG = _O_ZR + RV
_O_MR = _O_MG + D_MODEL
_O_END = _O_MR + D_MODEL

STATE_SLOTS = 3
VMEM_LIMIT = 56 * 1024 * 1024


def _dot(a, b):
    return jnp.dot(a, b, preferred_element_type=F32)


def _dot_nt(a, b):
    return lax.dot_general(a, b, (((1,), (1,)), ((), ())), preferred_element_type=F32)


def _log_sigmoid(x):
    return jnp.minimum(x, 0.0) - jnp.log(1.0 + jnp.exp(-jnp.abs(x)))


def _sigmoid(x):
    return 0.5 * jnp.tanh(0.5 * x) + 0.5


def _silu(x):
    return x * _sigmoid(x)


def _ada_kernel(c_ref, w_ref, b_ref, o_ref):
    one = jnp.where(pl.program_id(0) == 1, 1.0, 0.0)
    o_ref[...] = _dot(c_ref[...].astype(BF16), w_ref[...].astype(BF16)) + (b_ref[...] + one)


def _ada(c, w_ada, b_ada):
    n = c.shape[0]
    return pl.pallas_call(
        _ada_kernel,
        grid=(3,),
        in_specs=[pl.BlockSpec((n, D_MODEL), lambda j: (0, 0)),
                  pl.BlockSpec((D_MODEL, D_MODEL), lambda j: (0, j)),
                  pl.BlockSpec((1, D_MODEL), lambda j: (0, j))],
        out_specs=pl.BlockSpec((n, D_MODEL), lambda j: (0, j)),
        out_shape=jax.ShapeDtypeStruct((n, 3 * D_MODEL), F32),
        compiler_params=pltpu.CompilerParams(dimension_semantics=("arbitrary",)),
        name="ada",
    )(c, w_ada, b_ada)


def _proj_kernel(cum_chunk, x_ref, scale1_ref, shift_ref, w_ref, wlr_ref, wlr2_ref, blr2_ref, cos_ref, sin_ref,
                 p_ref, la_ref, h_ref):
    j = pl.program_id(1)

    @pl.when(j == 0)
    def _():
        h = (x_ref[...] * scale1_ref[...] + shift_ref[...]).astype(BF16)
        h_ref[...] = h
        sub = min(PROJ_SUB, h.shape[0]) // 2
        lr = jnp.concatenate([_dot_nt(h[r:r + sub, :], wlr_ref[...]) for r in range(0, h.shape[0], sub)], axis=0)
        pre = _dot(lr.astype(BF16), wlr2_ref[...]) + blr2_ref[...]
        la = _log_sigmoid(pre) * (1.0 / GLA_TAU)
        if cum_chunk:
            for r in range(0, la.shape[0], cum_chunk):
                la_ref[r:r + cum_chunk, :] = _cumsum_rows_mxu(la[r:r + cum_chunk, :])
        else:
            la_ref[...] = la

    def epilogue(kind, a, rs):
        if kind in ("rot_q", "rot_k"):
            scale = 1.0 if kind == "rot_q" else RET_DK ** -0.5
            return _rotary_heads(a, cos_ref[rs, :], sin_ref[rs, :], a.shape[1] // RET_DK, scale)
        return {"plain": lambda v: v, "silu": _silu, "sigmoid": _sigmoid}[kind](a)

    def emit(step):
        tn = p_ref.shape[1]
        lo, hi = step * tn, (step + 1) * tn
        runs = [(max(a, lo) - lo, min(b, hi) - lo, kind)
                for (a, kind), (b, _) in zip(P_RUNS[:-1], P_RUNS[1:]) if max(a, lo) < min(b, hi)]
        sub = min(PROJ_SUB, p_ref.shape[0])
        for r in range(0, p_ref.shape[0], sub):
            rs = slice(r, r + sub)
            acc = _dot_nt(h_ref[rs, :], w_ref[...])
            for a, b, kind in runs:
                p_ref[rs, a:b] = epilogue(kind, acc[:, a:b], rs).astype(p_ref.dtype)

    for step in range(P_COLS // p_ref.shape[1]):
        pl.when(j == step)(functools.partial(emit, step))


def _proj(x2d, scale1_3, shift3, w_main, w_lr, w_lr2, b_lr2, cos, sin, *, tm, rows_per_mod, p_dtype, cum_chunk=0):
    assert cum_chunk == 0 or (tm % cum_chunk == 0 and rows_per_mod % cum_chunk == 0)
    n = x2d.shape[0]
    rope_tiles = cos.shape[0] // tm
    rope_spec = pl.BlockSpec((tm, RET_DK // 2), lambda i, j: (i % rope_tiles, 0))
    per_row = shift3.shape[1] != 1
    if per_row:
        mod_spec = pl.BlockSpec((None, tm, D_MODEL), lambda i, j: (0, i, 0))
    else:
        tiles_per_mod = rows_per_mod // tm
        mod_spec = pl.BlockSpec((None, 1, D_MODEL), lambda i, j: (i // tiles_per_mod, 0, 0))
    tn = PROJ_TN
    assert all(a % RET_DK == 0 for a, _ in P_RUNS) and tn % RET_DK == 0
    return pl.pallas_call(
        functools.partial(_proj_kernel, cum_chunk),
        grid=(n // tm, P_COLS // tn),
        in_specs=[pl.BlockSpec((tm, D_MODEL), lambda i, j: (i, 0)),
                  mod_spec, mod_spec,
                  pl.BlockSpec((tn, D_MODEL), lambda i, j: (j, 0)),
                  pl.BlockSpec((LR_PAD, D_MODEL), lambda i, j: (0, 0)),
                  pl.BlockSpec((LR_PAD, GK), lambda i, j: (0, 0)),
                  pl.BlockSpec((1, GK), lambda i, j: (0, 0)),
                  rope_spec, rope_spec],
        out_specs=[pl.BlockSpec((tm, tn), lambda i, j: (i, j)),
                   pl.BlockSpec((tm, GK), lambda i, j: (i, 0))],
        out_shape=[jax.ShapeDtypeStruct((n, P_COLS), p_dtype),
                   jax.ShapeDtypeStruct((n, GK), F32)],
        scratch_shapes=[pltpu.VMEM((tm, D_MODEL), BF16)],
        compiler_params=pltpu.CompilerParams(dimension_semantics=("parallel", "arbitrary"),
                                             vmem_limit_bytes=VMEM_LIMIT),
        name="proj",
    )(x2d, scale1_3, shift3, w_main, w_lr, w_lr2, b_lr2, cos, sin)


def _rms_gate(o, sz):
    o = o * lax.rsqrt(jnp.mean(o * o, axis=-1, keepdims=True) + HEAD_NORM_EPS)
    return o.astype(sz.dtype) * sz


def _group_gate(o, sz):
    mu = jnp.mean(o, axis=-1, keepdims=True)
    c = o - mu
    var = jnp.mean(c * c, axis=-1, keepdims=True)
    return (c * lax.rsqrt(var + HEAD_NORM_EPS)).astype(sz.dtype) * sz


def _cumsum_rows_mxu(x):
    n = x.shape[0]
    row = lax.broadcasted_iota(jnp.int32, (n, n), 0)
    col = lax.broadcasted_iota(jnp.int32, (n, n), 1)
    tri = jnp.where(col <= row, 1.0, 0.0).astype(BF16)
    hi = x.astype(BF16)
    rest = x - hi.astype(F32)
    mid = rest.astype(BF16)
    lo = (rest - mid.astype(F32)).astype(BF16)
    return _dot(jnp.concatenate([tri, tri, tri], axis=1), jnp.concatenate([hi, mid, lo], axis=0))


def _block_edge(b, w, prev):
    n, m = b.shape
    pieces = []
    for i in range(n // w):
        r = i * w - 1 if prev else i * w + w - 1
        if r < 0:
            pieces.append(jnp.zeros((w, m), F32))
        else:
            pieces.append(jnp.broadcast_to(b[r:r + 1, :], (w, m)))
    return jnp.concatenate(pieces, axis=0)


def _gla_levels(c):
    w = GLA_SUB
    while w < c:
        yield w
        w *= 2


def _gla_operands(q, k, b):
    c = q.shape[0]
    btot = b[c - 1:c, :]
    rel = b - _block_edge(b, GLA_SUB, True)
    ops = {"q_in": (q * jnp.exp(b)).astype(BF16),
           "k_out": (k * jnp.exp(btot - b)).astype(BF16),
           "qd": (q * jnp.exp(rel)).astype(BF16),
           "kd": (k * jnp.exp(-rel)).astype(BF16)}
    for w in _gla_levels(c):
        if w > GLA_SUB:
            ops["q%d" % w] = (q * jnp.exp(b - _block_edge(b, w, True))).astype(BF16)
        ops["k%d" % w] = (k * jnp.exp(_block_edge(b, w, False) - b)).astype(BF16)
    return ops


def _gla_level_map(c):
    row = lax.broadcasted_iota(jnp.int32, (c, c), 0)
    col = lax.broadcasted_iota(jnp.int32, (c, c), 1)
    sh = GLA_SUB.bit_length() - 1
    level = jnp.where(((row >> sh) == (col >> sh)) & (col <= row), GLA_SUB - 1, 0)
    for w in _gla_levels(c):
        sh = w.bit_length() - 1
        rb, cb = row >> sh, col >> sh
        level = jnp.where(((rb & 1) == 1) & (cb == rb - 1), w, level)
    return level


def _gla_scores(ops, hs, level, c):
    a = jnp.where(level == GLA_SUB - 1, _dot_nt(ops["qd"][:, hs], ops["kd"][:, hs]), 0.0)
    for w in _gla_levels(c):
        qw = ops["qd"] if w == GLA_SUB else ops["q%d" % w]
        a = jnp.where(level == w, _dot_nt(qw[:, hs], ops["k%d" % w][:, hs]), a)
    return a.astype(BF16)


def _rotary_heads(x, cos, sin, heads, scale=1.0):
    half = cos.shape[1]
    if scale != 1.0:
        cos, sin = cos * scale, sin * scale
    cos_t = jnp.concatenate([cos] * (2 * heads), axis=1)
    sin_t = jnp.concatenate([-sin, sin] * heads, axis=1)
    partner = [i + 1 - 2 * (i % 2) for i in range(2 * heads)]
    swapped = jnp.concatenate([x[:, j * half:(j + 1) * half] for j in partner], axis=1)
    return x * cos_t + swapped * sin_t


def _mixer_prompt_kernel(d_chunk, vr_ref, zr_ref, qr_ref, kr_ref, vg_ref, zg_ref, qkg_ref, b_ref,
                         dintra_ref, dread_ref, dwrite_ref,
                         og_ref, or_ref, sg_ref, sr_ref):
    @pl.when(pl.program_id(1) == 0)
    def _():
        sg_ref[...] = jnp.zeros_like(sg_ref)
        sr_ref[...] = jnp.zeros_like(sr_ref)

    c = CHUNK
    g_sl = [slice(h * GLA_DK, (h + 1) * GLA_DK) for h in range(GLA_HEADS)]
    gv_sl = [slice(h * GLA_DV, (h + 1) * GLA_DV) for h in range(GLA_HEADS)]
    r_sl = [slice(h * RET_DK, (h + 1) * RET_DK) for h in range(RET_HEADS)]
    rv_sl = [slice(h * RET_DV, (h + 1) * RET_DV) for h in range(RET_HEADS)]

    level = _gla_level_map(c)
    d_read = dread_ref[...].astype(BF16)
    d_write = dwrite_ref[...].astype(BF16)

    def chunk(i, carry):
        rows = pl.ds(pl.multiple_of(i * c, c), c)
        b = b_ref[rows, :]
        qb = qr_ref[rows, :]
        kb = kr_ref[rows, :]
        q_read = qb * d_read
        k_write = kb * d_write
        g_rows = jnp.exp(b[c - 1:c, :])
        g_cols = [jnp.broadcast_to(g_rows[:, s], (GLA_DK, GLA_DK)).T[:, :1] for s in g_sl]
        gops = _gla_operands(qkg_ref[rows, :GK].astype(F32) * (GLA_DK ** -0.5),
                             qkg_ref[rows, GK:].astype(F32), b)

        a_g = [_gla_scores(gops, s, level, c) for s in g_sl]
        a_r = [(_dot_nt(qb[:, s], kb[:, s]) * dintra_ref[h]).astype(BF16) for h, s in enumerate(r_sl)]

        for h in range(GLA_HEADS):
            o = _dot(gops["q_in"][:, g_sl[h]], sg_ref[h].astype(BF16)) + _dot(a_g[h], vg_ref[rows, gv_sl[h]])
            og_ref[rows, gv_sl[h]] = _rms_gate(o, zg_ref[rows, gv_sl[h]])
        for h in range(RET_HEADS):
            o = _dot(a_r[h], vr_ref[rows, rv_sl[h]]) + _dot(q_read[:, r_sl[h]], sr_ref[h].astype(BF16))
            or_ref[rows, rv_sl[h]] = _group_gate(o, zr_ref[rows, rv_sl[h]])

        for h in range(GLA_HEADS):
            sg_ref[h] = g_cols[h] * sg_ref[h] + _dot(gops["k_out"][:, g_sl[h]].T, vg_ref[rows, gv_sl[h]])
        for h in range(RET_HEADS):
            sr_ref[h] = d_chunk[h] * sr_ref[h] + _dot(k_write[:, r_sl[h]].T, vr_ref[rows, rv_sl[h]])
        return carry

    lax.fori_loop(0, MIXER_CHUNKS, chunk, 0)


def _ret_consts(length):
    h = np.arange(RET_HEADS, dtype=np.float64)
    log_gamma = np.log1p(-np.exp2(-5.0 - h))
    idx = np.arange(length, dtype=np.float64)
    diff = idx[:, None] - idx[None, :]
    d_intra = np.where(diff[None] >= 0, np.exp(np.maximum(diff, 0.0)[None] * log_gamma[:, None, None]), 0.0)
    d_read = np.exp((idx + 1.0)[None, :] * log_gamma[:, None])
    d_write = np.exp((length - 1.0 - idx)[None, :] * log_gamma[:, None])
    d_chunk = np.exp(length * log_gamma)
    return (d_intra.astype(np.float32), d_read.astype(np.float32)[..., None],
            d_write.astype(np.float32)[..., None], tuple(float(np.float32(x)) for x in d_chunk))


def _rope_tables(pos):
    half = RET_DK // 2
    inv_freq = ROPE_BASE ** (-np.arange(half, dtype=np.float64) / half)
    ang = np.asarray(pos, np.float64)[:, None] * inv_freq[None, :]
    return jnp.asarray(np.cos(ang), F32), jnp.asarray(np.sin(ang), F32)


def _mixer_prompt(p, la, batch, seq):
    assert p.dtype == BF16 and seq % (CHUNK * MIXER_CHUNKS) == 0
    d_intra, d_read, d_write, d_chunk = _ret_consts(CHUNK)
    d_read = np.repeat(d_read[..., 0].T, RET_DK, axis=1)
    d_write = np.repeat(d_write[..., 0].T, RET_DK, axis=1)

    rows = CHUNK * MIXER_CHUNKS
    nstep = seq // rows

    def pcol(width, blk):
        return pl.BlockSpec((rows, width), lambda b, t: (b * nstep + t, blk))

    const3 = lambda shape: pl.BlockSpec(shape, lambda b, t: (0, 0, 0))
    tok = lambda width: pl.BlockSpec((rows, width), lambda b, t: (b * nstep + t, 0))
    return pl.pallas_call(
        functools.partial(_mixer_prompt_kernel, d_chunk),
        grid=(batch, nstep),
        in_specs=[pcol(RV, BLK_VR), pcol(RV, BLK_ZR), pcol(RK, BLK_QR), pcol(RK, BLK_KR),
                  pcol(GV, BLK_VG), pcol(GV, BLK_ZG), pcol(2 * GK, BLK_QKG), tok(GK),
                  const3((RET_HEADS, CHUNK, CHUNK)),
                  pl.BlockSpec((CHUNK, RK), lambda b, t: (0, 0)),
                  pl.BlockSpec((CHUNK, RK), lambda b, t: (0, 0))],
        out_specs=[tok(GV), tok(RV),
                   pl.BlockSpec((None, GLA_HEADS, GLA_DK, GLA_DV), lambda b, t: (b, 0, 0, 0)),
                   pl.BlockSpec((None, RET_HEADS, RET_DK, RET_DV), lambda b, t: (b, 0, 0, 0))],
        out_shape=[jax.ShapeDtypeStruct((batch * seq, GV), BF16),
                   jax.ShapeDtypeStruct((batch * seq, RV), BF16),
                   jax.ShapeDtypeStruct((batch, GLA_HEADS, GLA_DK, GLA_DV), F32),
                   jax.ShapeDtypeStruct((batch, RET_HEADS, RET_DK, RET_DV), F32)],
        compiler_params=pltpu.CompilerParams(dimension_semantics=("parallel", "arbitrary"),
                                             vmem_limit_bytes=VMEM_LIMIT),
        name="mixer_prompt",
    )(p, p, p, p, p, p, p, la, jnp.asarray(d_intra), jnp.asarray(d_read), jnp.asarray(d_write))


def _pad_rows(x, rows):
    return jnp.concatenate([x, jnp.zeros((rows - x.shape[0], x.shape[1]), x.dtype)], axis=0)


def _mixer_sample_kernel(seq, gammas, d_chunk, vr_ref, zr_ref, qr_ref, kr_ref, vg_ref, zg_ref, qkg_ref,
                         la_ref, dread_ref, dwrite_ref,
                         sg_hbm, sr_hbm, og_ref, or_ref, sg_ref, sr_ref, sg_buf, sr_buf, sem):
    rows = SAMPLE_ROWS
    n_el = rows // seq
    step = pl.program_id(0)
    ahead = STATE_SLOTS - 1

    def state_copies(s):
        slot = s % STATE_SLOTS
        src = pl.ds(s * n_el, n_el)
        return (pltpu.make_async_copy(sg_hbm.at[src], sg_buf.at[slot], sem.at[0, slot]),
                pltpu.make_async_copy(sr_hbm.at[src], sr_buf.at[slot], sem.at[1, slot]))

    @pl.when(step == 0)
    def _():
        for s in range(ahead):
            for cp in state_copies(s):
                cp.start()

    @pl.when(step + ahead < pl.num_programs(0))
    def _():
        for cp in state_copies(step + ahead):
            cp.start()

    for cp in state_copies(step):
        cp.wait()
    sg_in = sg_buf.at[step % STATE_SLOTS]
    sr_in = sr_buf.at[step % STATE_SLOTS]
    row1 = lax.broadcasted_iota(jnp.int32, (rows, 1), 0)
    pos1 = row1 % seq
    tcol = lax.broadcasted_iota(jnp.int32, (LANES, LANES), 1)

    def seg_cumsum(x):
        s = 1
        while s < seq:
            x = x + jnp.where(pos1 >= s, pltpu.roll(x, s, 0), 0.0)
            s *= 2
        return x

    def intra(q, k, v, gamma):
        out = jnp.zeros_like(v)
        for d in range(seq):
            kd = k if d == 0 else pltpu.roll(k, d, 0)
            vd = v if d == 0 else pltpu.roll(v, d, 0)
            a = jnp.sum(q * kd, axis=1, keepdims=True) * (gamma ** d)
            out = out + jnp.where(pos1 >= d, a, 0.0) * vd
        return out

    def inter(q, states):
        qb = q.astype(BF16)
        out = None
        for e in range(n_el):
            oe = _dot(qb, states[e].astype(BF16))
            out = oe if out is None else jnp.where(row1 >= e * seq, oe, out)
        return out

    for h in range(GLA_HEADS):
        kq = slice(h * GLA_DK, (h + 1) * GLA_DK)
        kk = slice(GK + h * GLA_DK, GK + (h + 1) * GLA_DK)
        vv = slice(h * GLA_DV, (h + 1) * GLA_DV)
        q = qkg_ref[:, kq].astype(F32) * (GLA_DK ** -0.5)
        k = qkg_ref[:, kk].astype(F32)
        v = vg_ref[:, vv].astype(F32)
        la = la_ref[:, kq]
        b = seg_cumsum(la)
        q_dec = q * jnp.exp(b)
        k_dec = k * jnp.exp(-b)
        states = [sg_in[e, h] for e in range(n_el)]
        o = inter(q_dec, states) + intra(q_dec, k_dec, v, 1.0)
        og_ref[:, vv] = _rms_gate(o, zg_ref[:, vv]).astype(og_ref.dtype)
        b_last = jnp.zeros_like(b)
        for e in range(n_el):
            r = e * seq + seq - 1
            b_last = jnp.where((row1 >= e * seq) & (row1 < (e + 1) * seq),
                               jnp.broadcast_to(b[r:r + 1, :], b.shape), b_last)
        k_out = k * jnp.exp(b_last - b)
        k_outT = _pad_rows(k_out, LANES).T
        laT = _pad_rows(la, LANES).T
        vb = _pad_rows(v, LANES).astype(BF16)
        tl = tcol[:GLA_DK]
        for e in range(n_el):
            sel = (tl >= e * seq) & (tl < (e + 1) * seq)
            g_col = jnp.exp(jnp.sum(jnp.where(sel, laT, 0.0), axis=1, keepdims=True))
            sg_ref[e, h] = g_col * states[e] + _dot(jnp.where(sel, k_outT, 0.0).astype(BF16), vb)

    for h in range(RET_HEADS):
        kq = slice(h * RET_DK, (h + 1) * RET_DK)
        vv = slice(h * RET_DV, (h + 1) * RET_DV)
        q = qr_ref[:, kq].astype(F32)
        k = kr_ref[:, kq].astype(F32)
        v = vr_ref[:, vv].astype(F32)
        states = [sr_in[e, h] for e in range(n_el)]
        o = inter(q, states) * dread_ref[h] + intra(q, k, v, gammas[h])
        or_ref[:, vv] = _group_gate(o, zr_ref[:, vv]).astype(or_ref.dtype)
        kwT = _pad_rows(k * dwrite_ref[h], LANES).T
        vb = _pad_rows(v, LANES).astype(BF16)
        tl = jnp.concatenate([tcol, tcol], axis=0)
        for e in range(n_el):
            sel = (tl >= e * seq) & (tl < (e + 1) * seq)
            sr_ref[e, h] = d_chunk[h] * states[e] + _dot(jnp.where(sel, kwT, 0.0).astype(BF16), vb)


def _mixer_sample(p, la, state_gla, state_ret, batch, seq):
    rows = SAMPLE_ROWS
    n_el = rows // seq
    _, d_read, d_write, d_chunk = _ret_consts(seq)
    gammas = tuple(float(1.0 - 2.0 ** (-5.0 - h)) for h in range(RET_HEADS))
    d_read = np.tile(d_read, (1, n_el, 1))
    d_write = np.tile(d_write, (1, n_el, 1))

    pcol = lambda width, blk: pl.BlockSpec((rows, width), lambda i: (i, blk))
    tok = lambda width: pl.BlockSpec((rows, width), lambda i: (i, 0))
    full3 = lambda shape: pl.BlockSpec(shape, lambda i: (0, 0, 0))
    sg_blk = (n_el, GLA_HEADS, GLA_DK, GLA_DV)
    sr_blk = (n_el, RET_HEADS, RET_DK, RET_DV)
    sg_spec = pl.BlockSpec(sg_blk, lambda i: (i, 0, 0, 0))
    sr_spec = pl.BlockSpec(sr_blk, lambda i: (i, 0, 0, 0))
    in_hbm = pl.BlockSpec(memory_space=pl.ANY)
    steps = batch * seq // rows
    assert steps >= STATE_SLOTS and state_gla.dtype == F32 and state_ret.dtype == F32
    return pl.pallas_call(
        functools.partial(_mixer_sample_kernel, seq, gammas, d_chunk),
        grid=(steps,),
        in_specs=[pcol(RV, BLK_VR), pcol(RV, BLK_ZR), pcol(RK, BLK_QR), pcol(RK, BLK_KR),
                  pcol(GV, BLK_VG), pcol(GV, BLK_ZG), pcol(2 * GK, BLK_QKG), tok(GK),
                  full3((RET_HEADS, rows, 1)), full3((RET_HEADS, rows, 1)),
                  in_hbm, in_hbm],
        out_specs=[tok(GV), tok(RV), sg_spec, sr_spec],
        out_shape=[jax.ShapeDtypeStruct((batch * seq, GV), F32),
                   jax.ShapeDtypeStruct((batch * seq, RV), F32),
                   jax.ShapeDtypeStruct(state_gla.shape, F32),
                   jax.ShapeDtypeStruct(state_ret.shape, F32)],
        scratch_shapes=[pltpu.VMEM((STATE_SLOTS,) + sg_blk, F32),
                        pltpu.VMEM((STATE_SLOTS,) + sr_blk, F32),
                        pltpu.SemaphoreType.DMA((2, STATE_SLOTS))],
        compiler_params=pltpu.CompilerParams(dimension_semantics=("arbitrary",),
                                             vmem_limit_bytes=VMEM_LIMIT),
        name="mixer_sample",
    )(p, p, p, p, p, p, p, la, jnp.asarray(d_read), jnp.asarray(d_write), state_gla, state_ret)


def _out_kernel(og_ref, or_ref, mg_ref, mr_ref, x_ref, gate_ref, wbg_ref, wbr_ref, wo_ref, lng_ref, lnb_ref,
                y_ref):
    sub = min(OUT_SUB, y_ref.shape[0])
    blocks = [slice(r0, r0 + sub) for r0 in range(0, y_ref.shape[0], sub)]
    merged = []
    for rs in blocks:
        bg = _dot(og_ref[rs, :].astype(BF16), wbg_ref[...])
        br = _dot(or_ref[rs, :].astype(BF16), wbr_ref[...])
        merged.append((mg_ref[rs, :].astype(F32) * bg + mr_ref[rs, :].astype(F32) * br).astype(BF16))
    for rs, m in zip(blocks, merged):
        out = _dot(m, wo_ref[...])
        gate = gate_ref[...] if gate_ref.shape[0] == 1 else gate_ref[rs, :]
        r = DEEPNORM_ALPHA * x_ref[rs, :] + gate * out
        mu = jnp.mean(r, axis=-1, keepdims=True)
        c = r - mu
        var = jnp.mean(c * c, axis=-1, keepdims=True)
        y_ref[rs, :] = c * lax.rsqrt(var + LN_EPS) * lng_ref[...] + lnb_ref[...]


def _out_stage(og, orr, p, x2d, gate3, wbg, wbr, wo, ln_g, ln_b, *, tm, rows_per_mod):
    n = x2d.shape[0]
    if gate3.shape[1] != 1:
        gate_spec = pl.BlockSpec((None, tm, D_MODEL), lambda i: (0, i, 0))
    else:
        tiles_per_mod = rows_per_mod // tm
        gate_spec = pl.BlockSpec((None, 1, D_MODEL), lambda i: (i // tiles_per_mod, 0, 0))
    full = lambda shape: pl.BlockSpec(shape, lambda i: (0, 0), pipeline_mode=pl.Buffered(1))
    return pl.pallas_call(
        _out_kernel,
        grid=(n // tm,),
        in_specs=[pl.BlockSpec((tm, GV), lambda i: (i, 0)),
                  pl.BlockSpec((tm, RV), lambda i: (i, 0)),
                  pl.BlockSpec((tm, D_MODEL), lambda i: (i, BLK_MG)),
                  pl.BlockSpec((tm, D_MODEL), lambda i: (i, BLK_MR)),
                  pl.BlockSpec((tm, D_MODEL), lambda i: (i, 0)),
                  gate_spec,
                  full((GV, D_MODEL)), full((RV, D_MODEL)), full((D_MODEL, D_MODEL)),
                  full((1, D_MODEL)), full((1, D_MODEL))],
        out_specs=pl.BlockSpec((tm, D_MODEL), lambda i: (i, 0)),
        out_shape=jax.ShapeDtypeStruct((n, D_MODEL), F32),
        compiler_params=pltpu.CompilerParams(dimension_semantics=("parallel",),
                                             vmem_limit_bytes=VMEM_LIMIT),
        name="out_stage",
    )(og, orr, p, p, x2d, gate3, wbg, wbr, wo, ln_g, ln_b)


_W_SRC_BLK = (_O_VR // P_BLK, _O_VR // P_BLK + 1, _O_QR // P_BLK, _O_KR // P_BLK, _O_VG // P_BLK,
              _O_QG // P_BLK, _O_ZR // P_BLK, _O_ZR // P_BLK + 1, _O_ZG // P_BLK, _O_MG // P_BLK,
              _O_MR // P_BLK)
_LR_BLK = _O_LR // P_BLK


def _prep_w_kernel(src_ref, a_ref, b_ref, o_ref, lr_ref):
    blk = src_ref[pl.program_id(0)]
    keep = P_BLK - GLA_LOWRANK

    @pl.when(blk >= _LR_BLK)
    def _():
        o_ref[:keep, :] = a_ref[GLA_LOWRANK:, :].astype(BF16)
        o_ref[keep:, :] = b_ref[...].astype(BF16)

    @pl.when(blk < _LR_BLK)
    def _():
        o_ref[...] = a_ref[...].astype(BF16)

    @pl.when(blk == _LR_BLK)
    def _():
        lr_ref[:GLA_LOWRANK, :] = a_ref[:GLA_LOWRANK, :].astype(BF16)
        lr_ref[GLA_LOWRANK:, :] = jnp.zeros((LR_PAD - GLA_LOWRANK, D_MODEL), BF16)


def _prep_w_in(wt):
    assert all(o % P_BLK in (0, GLA_LOWRANK) for o in (_O_VR, _O_QR, _O_KR, _O_VG, _O_QG, _O_ZR, _O_ZG,
                                                       _O_MG, _O_MR)) and _O_LR % P_BLK == 0
    src = jnp.asarray(_W_SRC_BLK, jnp.int32)
    per_blk = P_BLK // GLA_LOWRANK
    return pl.pallas_call(
        _prep_w_kernel,
        grid_spec=pltpu.PrefetchScalarGridSpec(
            num_scalar_prefetch=1,
            grid=(len(_W_SRC_BLK),),
            in_specs=[pl.BlockSpec((P_BLK, D_MODEL), lambda j, src: (src[j], 0)),
                      pl.BlockSpec((GLA_LOWRANK, D_MODEL), lambda j, src: ((src[j] + 1) * per_blk, 0))],
            out_specs=[pl.BlockSpec((P_BLK, D_MODEL), lambda j, src: (j, 0)),
                       pl.BlockSpec((LR_PAD, D_MODEL), lambda j, src: (0, 0))]),
        out_shape=[jax.ShapeDtypeStruct((P_COLS, D_MODEL), BF16),
                   jax.ShapeDtypeStruct((LR_PAD, D_MODEL), BF16)],
        compiler_params=pltpu.CompilerParams(dimension_semantics=("arbitrary",),
                                             vmem_limit_bytes=VMEM_LIMIT),
        name="prep_w",
    )(src, wt, wt)


def kernel(x_prompt, x_sample, state_gla, state_ret, c_prompt, c_sample, w_ada, b_ada, w_in, w_lr2, b_lr2,
           gla_norm_g, ret_norm_g, w_branch_gla, w_branch_ret, w_out, ln_g, ln_b):
    assert w_ada.shape[0] == 1, "one layer"
    bp, tp, _ = x_prompt.shape
    bs, ts, _ = x_sample.shape

    mod = _ada(jnp.concatenate([c_prompt, c_sample], axis=0), w_ada[0], b_ada)
    shift, scale1, gate = mod[:, :D_MODEL], mod[:, D_MODEL:2 * D_MODEL], mod[:, 2 * D_MODEL:]
    per_group = lambda a: a[:bp][:, None, :]
    per_token = lambda a: jnp.repeat(a[bp:], ts, axis=0)[None]

    w_main, w_lr = _prep_w_in(w_in[0].T)
    w_lr2p = jnp.pad(w_lr2[0], ((0, LR_PAD - GLA_LOWRANK), (0, 0))).astype(BF16)
    wbg = (gla_norm_g[0][:, None] * w_branch_gla[0]).astype(BF16)
    wbr = (ret_norm_g[0][:, None] * w_branch_ret[0]).astype(BF16)
    wo = w_out[0].astype(BF16)

    xp = x_prompt.reshape(bp * tp, D_MODEL)
    cos_p, sin_p = _rope_tables(np.arange(tp))
    p_p, la_p = _proj(xp, per_group(scale1), per_group(shift), w_main, w_lr, w_lr2p, b_lr2, cos_p, sin_p,
                      tm=1024, rows_per_mod=tp, p_dtype=BF16, cum_chunk=CHUNK)
    og_p, or_p, sg_p, sr_p = _mixer_prompt(p_p, la_p, bp, tp)
    y_p = _out_stage(og_p, or_p, p_p, xp, per_group(gate), wbg, wbr, wo, ln_g, ln_b, tm=1024, rows_per_mod=tp)

    xs = x_sample.reshape(bs * ts, D_MODEL)
    cos_s, sin_s = _rope_tables(PAST_LEN + np.arange(bs * ts) % ts)
    p_s, la_s = _proj(xs, per_token(scale1), per_token(shift), w_main, w_lr, w_lr2p, b_lr2, cos_s, sin_s,
                      tm=bs * ts, rows_per_mod=ts, p_dtype=BF16)
    og_s, or_s, sg_s, sr_s = _mixer_sample(p_s, la_s, state_gla[0], state_ret[0], bs, ts)
    y_s = _out_stage(og_s, or_s, p_s, xs, per_token(gate), wbg, wbr, wo, ln_g, ln_b,
                     tm=bs * ts, rows_per_mod=ts)

    return (y_p.reshape(bp, tp, D_MODEL), y_s.reshape(bs, ts, D_MODEL),
            sg_p[None], sr_p[None], sg_s[None], sr_s[None])
```

```python
import functools

import numpy as np
import jax
import jax.numpy as jnp
from jax import lax
from jax.experimental import pallas as pl
from jax.experimental.pallas import tpu as pltpu

F32 = jnp.float32
BF16 = jnp.bfloat16

D_MODEL = 1024
PAST_LEN = 16384
GLA_HEADS, GLA_DK, GLA_DV = 4, 128, 256
RET_HEADS, RET_DK, RET_DV = 4, 256, 512
GK, GV = GLA_HEADS * GLA_DK, GLA_HEADS * GLA_DV
RK, RV = RET_HEADS * RET_DK, RET_HEADS * RET_DV
GLA_LOWRANK = 16
GLA_TAU = 16.0
GLA_SUB = 16
ROPE_BASE = 10000.0
DEEPNORM_ALPHA = 2.0 ** 0.25
LN_EPS = 1e-5
HEAD_NORM_EPS = 1e-5

CHUNK = 128
MIXER_CHUNKS = 4
SAMPLE_ROWS = 16
LANES = 128
LR_PAD = 256

P_BLK = 1024
P_COLS = 11 * P_BLK
BLK_VR, BLK_ZR = 0, 3
BLK_QR, BLK_KR, BLK_VG, BLK_QKG, BLK_ZG, BLK_MG, BLK_MR = 2, 3, 4, 5, 8, 9, 10
P_RUNS = ((0, "plain"), (BLK_QR * P_BLK, "rot_q"), (BLK_KR * P_BLK, "rot_k"), (BLK_VG * P_BLK, "plain"),
          (6 * P_BLK, "silu"), (BLK_MG * P_BLK, "sigmoid"), (P_COLS, None))
PROJ_TN = P_COLS // 4
PROJ_SUB = 1024
OUT_SUB = 256
_O_QG, _O_KG, _O_VG, _O_ZG, _O_LR = 0, GK, 2 * GK, 2 * GK + GV, 2 * GK + 2 * GV
_O_QR = _O_LR + GLA_LOWRANK
_O_KR = _O_QR + RK
_O_VR = _O_KR + RK
_O_ZR = _O_VR + RV
_O_MG = _O_ZR + RV
_O_MR = _O_MG + D_MODEL
_O_END = _O_MR + D_MODEL

STATE_SLOTS = 3
VMEM_LIMIT = 56 * 1024 * 1024


def _dot(a, b):
    return jnp.dot(a, b, preferred_element_type=F32)


def _dot_nt(a, b):
    return lax.dot_general(a, b, (((1,), (1,)), ((), ())), preferred_element_type=F32)


def _log_sigmoid(x):
    return jnp.minimum(x, 0.0) - jnp.log(1.0 + jnp.exp(-jnp.abs(x)))


def _sigmoid(x):
    return 0.5 * jnp.tanh(0.5 * x) + 0.5


def _silu(x):
    return x * _sigmoid(x)


def _ada_kernel(c_ref, w_ref, b_ref, o_ref):
    one = jnp.where(pl.program_id(0) == 1, 1.0, 0.0)
    o_ref[...] = _dot(c_ref[...].astype(BF16), w_ref[...].astype(BF16)) + (b_ref[...] + one)


def _ada(c, w_ada, b_ada):
    n = c.shape[0]
    return pl.pallas_call(
        _ada_kernel,
        grid=(3,),
        in_specs=[pl.BlockSpec((n, D_MODEL), lambda j: (0, 0)),
                  pl.BlockSpec((D_MODEL, D_MODEL), lambda j: (0, j)),
                  pl.BlockSpec((1, D_MODEL), lambda j: (0, j))],
        out_specs=pl.BlockSpec((n, D_MODEL), lambda j: (0, j)),
        out_shape=jax.ShapeDtypeStruct((n, 3 * D_MODEL), F32),
        compiler_params=pltpu.CompilerParams(dimension_semantics=("arbitrary",)),
        name="ada",
    )(c, w_ada, b_ada)


def _proj_kernel(cum_chunk, x_ref, scale1_ref, shift_ref, w_ref, wlr_ref, wlr2_ref, blr2_ref, cos_ref, sin_ref,
                 p_ref, la_ref, h_ref):
    j = pl.program_id(1)

    @pl.when(j == 0)
    def _():
        h = (x_ref[...] * scale1_ref[...] + shift_ref[...]).astype(BF16)
        h_ref[...] = h
        sub = min(PROJ_SUB, h.shape[0]) // 2
        lr = jnp.concatenate([_dot_nt(h[r:r + sub, :], wlr_ref[...]) for r in range(0, h.shape[0], sub)], axis=0)
        pre = _dot(lr.astype(BF16), wlr2_ref[...]) + blr2_ref[...]
        la = _log_sigmoid(pre) * (1.0 / GLA_TAU)
        if cum_chunk:
            for r in range(0, la.shape[0], cum_chunk):
                la_ref[r:r + cum_chunk, :] = _cumsum_rows_mxu(la[r:r + cum_chunk, :])
        else:
            la_ref[...] = la

    def epilogue(kind, a, rs):
        if kind in ("rot_q", "rot_k"):
            scale = 1.0 if kind == "rot_q" else RET_DK ** -0.5
            return _rotary_heads(a, cos_ref[rs, :], sin_ref[rs, :], a.shape[1] // RET_DK, scale)
        return {"plain": lambda v: v, "silu": _silu, "sigmoid": _sigmoid}[kind](a)

    def emit(step):
        tn = p_ref.shape[1]
        lo, hi = step * tn, (step + 1) * tn
        runs = [(max(a, lo) - lo, min(b, hi) - lo, kind)
                for (a, kind), (b, _) in zip(P_RUNS[:-1], P_RUNS[1:]) if max(a, lo) < min(b, hi)]
        sub = min(PROJ_SUB, p_ref.shape[0])
        for r in range(0, p_ref.shape[0], sub):
            rs = slice(r, r + sub)
            acc = _dot_nt(h_ref[rs, :], w_ref[...])
            for a, b, kind in runs:
                p_ref[rs, a:b] = epilogue(kind, acc[:, a:b], rs).astype(p_ref.dtype)

    for step in range(P_COLS // p_ref.shape[1]):
        pl.when(j == step)(functools.partial(emit, step))


def _proj(x2d, scale1_3, shift3, w_main, w_lr, w_lr2, b_lr2, cos, sin, *, tm, rows_per_mod, p_dtype, cum_chunk=0,
          mod_cols=(0, 0)):
    assert cum_chunk == 0 or (tm % cum_chunk == 0 and rows_per_mod % cum_chunk == 0)
    n = x2d.shape[0]
    rope_tiles = cos.shape[0] // tm
    rope_spec = pl.BlockSpec((tm, RET_DK // 2), lambda i, j: (i % rope_tiles, 0))
    per_row = shift3.shape[1] != 1
    tiles_per_mod = max(rows_per_mod // tm, 1)

    def mod_spec(col):
        if per_row:
            return pl.BlockSpec((None, tm, D_MODEL), lambda i, j: (0, i, col))
        return pl.BlockSpec((None, 1, D_MODEL), lambda i, j: (i // tiles_per_mod, 0, col))

    tn = PROJ_TN
    assert all(a % RET_DK == 0 for a, _ in P_RUNS) and tn % RET_DK == 0
    return pl.pallas_call(
        functools.partial(_proj_kernel, cum_chunk),
        grid=(n // tm, P_COLS // tn),
        in_specs=[pl.BlockSpec((tm, D_MODEL), lambda i, j: (i, 0)),
                  mod_spec(mod_cols[0]), mod_spec(mod_cols[1]),
                  pl.BlockSpec((tn, D_MODEL), lambda i, j: (j, 0)),
                  pl.BlockSpec((LR_PAD, D_MODEL), lambda i, j: (0, 0)),
                  pl.BlockSpec((LR_PAD, GK), lambda i, j: (0, 0)),
                  pl.BlockSpec((1, GK), lambda i, j: (0, 0)),
                  rope_spec, rope_spec],
        out_specs=[pl.BlockSpec((tm, tn), lambda i, j: (i, j)),
                   pl.BlockSpec((tm, GK), lambda i, j: (i, 0))],
        out_shape=[jax.ShapeDtypeStruct((n, P_COLS), p_dtype),
                   jax.ShapeDtypeStruct((n, GK), F32)],
        scratch_shapes=[pltpu.VMEM((tm, D_MODEL), BF16)],
        compiler_params=pltpu.CompilerParams(dimension_semantics=("parallel", "arbitrary"),
                                             vmem_limit_bytes=VMEM_LIMIT),
        name="proj",
    )(x2d, scale1_3, shift3, w_main, w_lr, w_lr2, b_lr2, cos, sin)


def _rms_gate(o, sz):
    o = o * lax.rsqrt(jnp.mean(o * o, axis=-1, keepdims=True) + HEAD_NORM_EPS)
    return o.astype(sz.dtype) * sz


def _group_gate(o, sz):
    mu = jnp.mean(o, axis=-1, keepdims=True)
    c = o - mu
    var = jnp.mean(c * c, axis=-1, keepdims=True)
    return (c * lax.rsqrt(var + HEAD_NORM_EPS)).astype(sz.dtype) * sz


def _cumsum_rows_mxu(x):
    n = x.shape[0]
    row = lax.broadcasted_iota(jnp.int32, (n, n), 0)
    col = lax.broadcasted_iota(jnp.int32, (n, n), 1)
    tri = jnp.where(col <= row, 1.0, 0.0).astype(BF16)
    hi = x.astype(BF16)
    rest = x - hi.astype(F32)
    mid = rest.astype(BF16)
    lo = (rest - mid.astype(F32)).astype(BF16)
    return _dot(jnp.concatenate([tri, tri, tri], axis=1), jnp.concatenate([hi, mid, lo], axis=0))


def _block_edge(b, w, prev):
    n, m = b.shape
    pieces = []
    for i in range(n // w):
        r = i * w - 1 if prev else i * w + w - 1
        if r < 0:
            pieces.append(jnp.zeros((w, m), F32))
        else:
            pieces.append(jnp.broadcast_to(b[r:r + 1, :], (w, m)))
    return jnp.concatenate(pieces, axis=0)


def _gla_levels(c):
    w = GLA_SUB
    while w < c:
        yield w
        w *= 2


def _gla_operands(q, k, b):
    c = q.shape[0]
    btot = b[c - 1:c, :]
    rel = b - _block_edge(b, GLA_SUB, True)
    ops = {"q_in": (q * jnp.exp(b)).astype(BF16),
           "k_out": (k * jnp.exp(btot - b)).astype(BF16),
           "qd": (q * jnp.exp(rel)).astype(BF16),
           "kd": (k * jnp.exp(-rel)).astype(BF16)}
    for w in _gla_levels(c):
        if w > GLA_SUB:
            ops["q%d" % w] = (q * jnp.exp(b - _block_edge(b, w, True))).astype(BF16)
        ops["k%d" % w] = (k * jnp.exp(_block_edge(b, w, False) - b)).astype(BF16)
    return ops


def _gla_level_map(c):
    row = lax.broadcasted_iota(jnp.int32, (c, c), 0)
    col = lax.broadcasted_iota(jnp.int32, (c, c), 1)
    sh = GLA_SUB.bit_length() - 1
    level = jnp.where(((row >> sh) == (col >> sh)) & (col <= row), GLA_SUB - 1, 0)
    for w in _gla_levels(c):
        sh = w.bit_length() - 1
        rb, cb = row >> sh, col >> sh
        level = jnp.where(((rb & 1) == 1) & (cb == rb - 1), w, level)
    return level


def _gla_scores(ops, hs, level, c):
    a = jnp.where(level == GLA_SUB - 1, _dot_nt(ops["qd"][:, hs], ops["kd"][:, hs]), 0.0)
    for w in _gla_levels(c):
        qw = ops["qd"] if w == GLA_SUB else ops["q%d" % w]
        a = jnp.where(level == w, _dot_nt(qw[:, hs], ops["k%d" % w][:, hs]), a)
    return a.astype(BF16)


def _rotary_heads(x, cos, sin, heads, scale=1.0):
    half = cos.shape[1]
    if scale != 1.0:
        cos, sin = cos * scale, sin * scale
    cos_t = jnp.concatenate([cos] * (2 * heads), axis=1)
    sin_t = jnp.concatenate([-sin, sin] * heads, axis=1)
    partner = [i + 1 - 2 * (i % 2) for i in range(2 * heads)]
    swapped = jnp.concatenate([x[:, j * half:(j + 1) * half] for j in partner], axis=1)
    return x * cos_t + swapped * sin_t


def _mixer_prompt_kernel(d_chunk, vr_ref, zr_ref, qr_ref, kr_ref, vg_ref, zg_ref, qkg_ref, b_ref,
                         dintra_ref, dread_ref, dwrite_ref,
                         og_ref, or_ref, sg_ref, sr_ref):
    @pl.when(pl.program_id(1) == 0)
    def _():
        sg_ref[...] = jnp.zeros_like(sg_ref)
        sr_ref[...] = jnp.zeros_like(sr_ref)

    c = CHUNK
    g_sl = [slice(h * GLA_DK, (h + 1) * GLA_DK) for h in range(GLA_HEADS)]
    gv_sl = [slice(h * GLA_DV, (h + 1) * GLA_DV) for h in range(GLA_HEADS)]
    r_sl = [slice(h * RET_DK, (h + 1) * RET_DK) for h in range(RET_HEADS)]
    rv_sl = [slice(h * RET_DV, (h + 1) * RET_DV) for h in range(RET_HEADS)]

    level = _gla_level_map(c)
    d_read = dread_ref[...].astype(BF16)
    d_write = dwrite_ref[...].astype(BF16)

    def chunk(i, carry):
        rows = pl.ds(pl.multiple_of(i * c, c), c)
        b = b_ref[rows, :]
        qb = qr_ref[rows, :]
        kb = kr_ref[rows, :]
        q_read = qb * d_read
        k_write = kb * d_write
        g_rows = jnp.exp(b[c - 1:c, :])
        g_cols = [jnp.broadcast_to(g_rows[:, s], (GLA_DK, GLA_DK)).T[:, :1] for s in g_sl]
        gops = _gla_operands(qkg_ref[rows, :GK].astype(F32) * (GLA_DK ** -0.5),
                             qkg_ref[rows, GK:].astype(F32), b)

        a_g = [_gla_scores(gops, s, level, c) for s in g_sl]
        a_r = [(_dot_nt(qb[:, s], kb[:, s]) * dintra_ref[h]).astype(BF16) for h, s in enumerate(r_sl)]

        for h in range(GLA_HEADS):
            o = _dot(gops["q_in"][:, g_sl[h]], sg_ref[h].astype(BF16)) + _dot(a_g[h], vg_ref[rows, gv_sl[h]])
            og_ref[rows, gv_sl[h]] = _rms_gate(o, zg_ref[rows, gv_sl[h]])
        for h in range(RET_HEADS):
            o = _dot(a_r[h], vr_ref[rows, rv_sl[h]]) + _dot(q_read[:, r_sl[h]], sr_ref[h].astype(BF16))
            or_ref[rows, rv_sl[h]] = _group_gate(o, zr_ref[rows, rv_sl[h]])

        for h in range(GLA_HEADS):
            sg_ref[h] = g_cols[h] * sg_ref[h] + _dot(gops["k_out"][:, g_sl[h]].T, vg_ref[rows, gv_sl[h]])
        for h in range(RET_HEADS):
            sr_ref[h] = d_chunk[h] * sr_ref[h] + _dot(k_write[:, r_sl[h]].T, vr_ref[rows, rv_sl[h]])
        return carry

    lax.fori_loop(0, MIXER_CHUNKS, chunk, 0)


def _ret_consts(length):
    h = np.arange(RET_HEADS, dtype=np.float64)
    log_gamma = np.log1p(-np.exp2(-5.0 - h))
    idx = np.arange(length, dtype=np.float64)
    diff = idx[:, None] - idx[None, :]
    d_intra = np.where(diff[None] >= 0, np.exp(np.maximum(diff, 0.0)[None] * log_gamma[:, None, None]), 0.0)
    d_read = np.exp((idx + 1.0)[None, :] * log_gamma[:, None])
    d_write = np.exp((length - 1.0 - idx)[None, :] * log_gamma[:, None])
    d_chunk = np.exp(length * log_gamma)
    return (d_intra.astype(np.float32), d_read.astype(np.float32)[..., None],
            d_write.astype(np.float32)[..., None], tuple(float(np.float32(x)) for x in d_chunk))


def _rope_tables(pos):
    half = RET_DK // 2
    inv_freq = ROPE_BASE ** (-np.arange(half, dtype=np.float64) / half)
    ang = np.asarray(pos, np.float64)[:, None] * inv_freq[None, :]
    return jnp.asarray(np.cos(ang), F32), jnp.asarray(np.sin(ang), F32)


def _mixer_prompt(p, la, batch, seq):
    assert p.dtype == BF16 and seq % (CHUNK * MIXER_CHUNKS) == 0
    d_intra, d_read, d_write, d_chunk = _ret_consts(CHUNK)
    d_read = np.repeat(d_read[..., 0].T, RET_DK, axis=1)
    d_write = np.repeat(d_write[..., 0].T, RET_DK, axis=1)

    rows = CHUNK * MIXER_CHUNKS
    nstep = seq // rows

    def pcol(width, blk):
        return pl.BlockSpec((rows, width), lambda b, t: (b * nstep + t, blk))

    const3 = lambda shape: pl.BlockSpec(shape, lambda b, t: (0, 0, 0))
    tok = lambda width: pl.BlockSpec((rows, width), lambda b, t: (b * nstep + t, 0))
    return pl.pallas_call(
        functools.partial(_mixer_prompt_kernel, d_chunk),
        grid=(batch, nstep),
        in_specs=[pcol(RV, BLK_VR), pcol(RV, BLK_ZR), pcol(RK, BLK_QR), pcol(RK, BLK_KR),
                  pcol(GV, BLK_VG), pcol(GV, BLK_ZG), pcol(2 * GK, BLK_QKG), tok(GK),
                  const3((RET_HEADS, CHUNK, CHUNK)),
                  pl.BlockSpec((CHUNK, RK), lambda b, t: (0, 0)),
                  pl.BlockSpec((CHUNK, RK), lambda b, t: (0, 0))],
        out_specs=[tok(GV), tok(RV),
                   pl.BlockSpec((None, GLA_HEADS, GLA_DK, GLA_DV), lambda b, t: (b, 0, 0, 0)),
                   pl.BlockSpec((None, RET_HEADS, RET_DK, RET_DV), lambda b, t: (b, 0, 0, 0))],
        out_shape=[jax.ShapeDtypeStruct((batch * seq, GV), BF16),
                   jax.ShapeDtypeStruct((batch * seq, RV), BF16),
                   jax.ShapeDtypeStruct((batch, GLA_HEADS, GLA_DK, GLA_DV), F32),
                   jax.ShapeDtypeStruct((batch, RET_HEADS, RET_DK, RET_DV), F32)],
        compiler_params=pltpu.CompilerParams(dimension_semantics=("parallel", "arbitrary"),
                                             vmem_limit_bytes=VMEM_LIMIT),
        name="mixer_prompt",
    )(p, p, p, p, p, p, p, la, jnp.asarray(d_intra), jnp.asarray(d_read), jnp.asarray(d_write))


def _pad_rows(x, rows):
    return jnp.concatenate([x, jnp.zeros((rows - x.shape[0], x.shape[1]), x.dtype)], axis=0)


def _mixer_sample_kernel(seq, gammas, d_chunk, vr_ref, zr_ref, qr_ref, kr_ref, vg_ref, zg_ref, qkg_ref,
                         la_ref, dread_ref, dwrite_ref,
                         sg_hbm, sr_hbm, og_ref, or_ref, sg_ref, sr_ref, sg_buf, sr_buf, sem):
    rows = SAMPLE_ROWS
    n_el = rows // seq
    step = pl.program_id(0)
    ahead = STATE_SLOTS - 1

    def state_copies(s):
        slot = s % STATE_SLOTS
        src = pl.ds(s * n_el, n_el)
        return (pltpu.make_async_copy(sg_hbm.at[src], sg_buf.at[slot], sem.at[0, slot]),
                pltpu.make_async_copy(sr_hbm.at[src], sr_buf.at[slot], sem.at[1, slot]))

    @pl.when(step == 0)
    def _():
        for s in range(ahead):
            for cp in state_copies(s):
                cp.start()

    @pl.when(step + ahead < pl.num_programs(0))
    def _():
        for cp in state_copies(step + ahead):
            cp.start()

    for cp in state_copies(step):
        cp.wait()
    sg_in = sg_buf.at[step % STATE_SLOTS]
    sr_in = sr_buf.at[step % STATE_SLOTS]
    row1 = lax.broadcasted_iota(jnp.int32, (rows, 1), 0)
    pos1 = row1 % seq
    tcol = lax.broadcasted_iota(jnp.int32, (LANES, LANES), 1)

    def seg_cumsum(x):
        s = 1
        while s < seq:
            x = x + jnp.where(pos1 >= s, pltpu.roll(x, s, 0), 0.0)
            s *= 2
        return x

    def intra(q, k, v, gamma):
        out = jnp.zeros_like(v)
        for d in range(seq):
            kd = k if d == 0 else pltpu.roll(k, d, 0)
            vd = v if d == 0 else pltpu.roll(v, d, 0)
            a = jnp.sum(q * kd, axis=1, keepdims=True) * (gamma ** d)
            out = out + jnp.where(pos1 >= d, a, 0.0) * vd
        return out

    def inter(q, states):
        qb = q.astype(BF16)
        out = None
        for e in range(n_el):
            oe = _dot(qb, states[e].astype(BF16))
            out = oe if out is None else jnp.where(row1 >= e * seq, oe, out)
        return out

    for h in range(GLA_HEADS):
        kq = slice(h * GLA_DK, (h + 1) * GLA_DK)
        kk = slice(GK + h * GLA_DK, GK + (h + 1) * GLA_DK)
        vv = slice(h * GLA_DV, (h + 1) * GLA_DV)
        q = qkg_ref[:, kq].astype(F32) * (GLA_DK ** -0.5)
        k = qkg_ref[:, kk].astype(F32)
        v = vg_ref[:, vv].astype(F32)
        la = la_ref[:, kq]
        b = seg_cumsum(la)
        q_dec = q * jnp.exp(b)
        k_dec = k * jnp.exp(-b)
        states = [sg_in[e, h] for e in range(n_el)]
        o = inter(q_dec, states) + intra(q_dec, k_dec, v, 1.0)
        og_ref[:, vv] = _rms_gate(o, zg_ref[:, vv]).astype(og_ref.dtype)
        b_last = jnp.zeros_like(b)
        for e in range(n_el):
            r = e * seq + seq - 1
            b_last = jnp.where((row1 >= e * seq) & (row1 < (e + 1) * seq),
                               jnp.broadcast_to(b[r:r + 1, :], b.shape), b_last)
        k_out = k * jnp.exp(b_last - b)
        k_outT = _pad_rows(k_out, LANES).T
        laT = _pad_rows(la, LANES).T
        vb = _pad_rows(v, LANES).astype(BF16)
        tl = tcol[:GLA_DK]
        for e in range(n_el):
            sel = (tl >= e * seq) & (tl < (e + 1) * seq)
            g_col = jnp.exp(jnp.sum(jnp.where(sel, laT, 0.0), axis=1, keepdims=True))
            sg_ref[e, h] = g_col * states[e] + _dot(jnp.where(sel, k_outT, 0.0).astype(BF16), vb)

    for h in range(RET_HEADS):
        kq = slice(h * RET_DK, (h + 1) * RET_DK)
        vv = slice(h * RET_DV, (h + 1) * RET_DV)
        q = qr_ref[:, kq].astype(F32)
        k = kr_ref[:, kq].astype(F32)
        v = vr_ref[:, vv].astype(F32)
        states = [sr_in[e, h] for e in range(n_el)]
        o = inter(q, states) * dread_ref[h] + intra(q, k, v, gammas[h])
        or_ref[:, vv] = _group_gate(o, zr_ref[:, vv]).astype(or_ref.dtype)
        kwT = _pad_rows(k * dwrite_ref[h], LANES).T
        vb = _pad_rows(v, LANES).astype(BF16)
        tl = jnp.concatenate([tcol, tcol], axis=0)
        for e in range(n_el):
            sel = (tl >= e * seq) & (tl < (e + 1) * seq)
            sr_ref[e, h] = d_chunk[h] * states[e] + _dot(jnp.where(sel, kwT, 0.0).astype(BF16), vb)


def _mixer_sample(p, la, state_gla, state_ret, batch, seq):
    rows = SAMPLE_ROWS
    n_el = rows // seq
    _, d_read, d_write, d_chunk = _ret_consts(seq)
    gammas = tuple(float(1.0 - 2.0 ** (-5.0 - h)) for h in range(RET_HEADS))
    d_read = np.tile(d_read, (1, n_el, 1))
    d_write = np.tile(d_write, (1, n_el, 1))

    pcol = lambda width, blk: pl.BlockSpec((rows, width), lambda i: (i, blk))
    tok = lambda width: pl.BlockSpec((rows, width), lambda i: (i, 0))
    full3 = lambda shape: pl.BlockSpec(shape, lambda i: (0, 0, 0))
    sg_blk = (n_el, GLA_HEADS, GLA_DK, GLA_DV)
    sr_blk = (n_el, RET_HEADS, RET_DK, RET_DV)
    sg_spec = pl.BlockSpec(sg_blk, lambda i: (i, 0, 0, 0))
    sr_spec = pl.BlockSpec(sr_blk, lambda i: (i, 0, 0, 0))
    in_hbm = pl.BlockSpec(memory_space=pl.ANY)
    steps = batch * seq // rows
    assert steps >= STATE_SLOTS and state_gla.dtype == F32 and state_ret.dtype == F32
    return pl.pallas_call(
        functools.partial(_mixer_sample_kernel, seq, gammas, d_chunk),
        grid=(steps,),
        in_specs=[pcol(RV, BLK_VR), pcol(RV, BLK_ZR), pcol(RK, BLK_QR), pcol(RK, BLK_KR),
                  pcol(GV, BLK_VG), pcol(GV, BLK_ZG), pcol(2 * GK, BLK_QKG), tok(GK),
                  full3((RET_HEADS, rows, 1)), full3((RET_HEADS, rows, 1)),
                  in_hbm, in_hbm],
        out_specs=[tok(GV), tok(RV), sg_spec, sr_spec],
        out_shape=[jax.ShapeDtypeStruct((batch * seq, GV), F32),
                   jax.ShapeDtypeStruct((batch * seq, RV), F32),
                   jax.ShapeDtypeStruct(state_gla.shape, F32),
                   jax.ShapeDtypeStruct(state_ret.shape, F32)],
        scratch_shapes=[pltpu.VMEM((STATE_SLOTS,) + sg_blk, F32),
                        pltpu.VMEM((STATE_SLOTS,) + sr_blk, F32),
                        pltpu.SemaphoreType.DMA((2, STATE_SLOTS))],
        compiler_params=pltpu.CompilerParams(dimension_semantics=("arbitrary",),
                                             vmem_limit_bytes=VMEM_LIMIT),
        name="mixer_sample",
    )(p, p, p, p, p, p, p, la, jnp.asarray(d_read), jnp.asarray(d_write), state_gla, state_ret)


def _out_kernel(og_ref, or_ref, mg_ref, mr_ref, x_ref, gate_ref, wbg_ref, wbr_ref, wo_ref, lng_ref, lnb_ref,
                y_ref):
    sub = min(OUT_SUB, y_ref.shape[0])
    blocks = [slice(r0, r0 + sub) for r0 in range(0, y_ref.shape[0], sub)]
    merged = []
    for rs in blocks:
        bg = _dot(og_ref[rs, :].astype(BF16), wbg_ref[...])
        br = _dot(or_ref[rs, :].astype(BF16), wbr_ref[...])
        merged.append((mg_ref[rs, :].astype(F32) * bg + mr_ref[rs, :].astype(F32) * br).astype(BF16))
    for rs, m in zip(blocks, merged):
        out = _dot(m, wo_ref[...])
        gate = gate_ref[...] if gate_ref.shape[0] == 1 else gate_ref[rs, :]
        r = DEEPNORM_ALPHA * x_ref[rs, :] + gate * out
        mu = jnp.mean(r, axis=-1, keepdims=True)
        c = r - mu
        var = jnp.mean(c * c, axis=-1, keepdims=True)
        y_ref[rs, :] = c * lax.rsqrt(var + LN_EPS) * lng_ref[...] + lnb_ref[...]


def _out_stage(og, orr, p, x2d, gate3, wbg, wbr, wo, ln_g, ln_b, *, tm, rows_per_mod, gate_col=0):
    n = x2d.shape[0]
    if gate3.shape[1] != 1:
        gate_spec = pl.BlockSpec((None, tm, D_MODEL), lambda i: (0, i, gate_col))
    else:
        tiles_per_mod = rows_per_mod // tm
        gate_spec = pl.BlockSpec((None, 1, D_MODEL), lambda i: (i // tiles_per_mod, 0, gate_col))
    full = lambda shape: pl.BlockSpec(shape, lambda i: (0, 0), pipeline_mode=pl.Buffered(1))
    return pl.pallas_call(
        _out_kernel,
        grid=(n // tm,),
        in_specs=[pl.BlockSpec((tm, GV), lambda i: (i, 0)),
                  pl.BlockSpec((tm, RV), lambda i: (i, 0)),
                  pl.BlockSpec((tm, D_MODEL), lambda i: (i, BLK_MG)),
                  pl.BlockSpec((tm, D_MODEL), lambda i: (i, BLK_MR)),
                  pl.BlockSpec((tm, D_MODEL), lambda i: (i, 0)),
                  gate_spec,
                  full((GV, D_MODEL)), full((RV, D_MODEL)), full((D_MODEL, D_MODEL)),
                  full((1, D_MODEL)), full((1, D_MODEL))],
        out_specs=pl.BlockSpec((tm, D_MODEL), lambda i: (i, 0)),
        out_shape=jax.ShapeDtypeStruct((n, D_MODEL), F32),
        compiler_params=pltpu.CompilerParams(dimension_semantics=("parallel",),
                                             vmem_limit_bytes=VMEM_LIMIT),
        name="out_stage",
    )(og, orr, p, p, x2d, gate3, wbg, wbr, wo, ln_g, ln_b)


_W_SRC_BLK = (_O_VR // P_BLK, _O_VR // P_BLK + 1, _O_QR // P_BLK, _O_KR // P_BLK, _O_VG // P_BLK,
              _O_QG // P_BLK, _O_ZR // P_BLK, _O_ZR // P_BLK + 1, _O_ZG // P_BLK, _O_MG // P_BLK,
              _O_MR // P_BLK)
_LR_BLK = _O_LR // P_BLK


def _prep_w_kernel(src_ref, a_ref, b_ref, o_ref, lr_ref):
    blk = src_ref[pl.program_id(0)]
    keep = P_BLK - GLA_LOWRANK

    @pl.when(blk >= _LR_BLK)
    def _():
        o_ref[:keep, :] = a_ref[GLA_LOWRANK:, :].astype(BF16)
        o_ref[keep:, :] = b_ref[...].astype(BF16)

    @pl.when(blk < _LR_BLK)
    def _():
        o_ref[...] = a_ref[...].astype(BF16)

    @pl.when(blk == _LR_BLK)
    def _():
        lr_ref[:GLA_LOWRANK, :] = a_ref[:GLA_LOWRANK, :].astype(BF16)
        lr_ref[GLA_LOWRANK:, :] = jnp.zeros((LR_PAD - GLA_LOWRANK, D_MODEL), BF16)


def _prep_w_in(wt):
    assert all(o % P_BLK in (0, GLA_LOWRANK) for o in (_O_VR, _O_QR, _O_KR, _O_VG, _O_QG, _O_ZR, _O_ZG,
                                                       _O_MG, _O_MR)) and _O_LR % P_BLK == 0
    src = jnp.asarray(_W_SRC_BLK, jnp.int32)
    per_blk = P_BLK // GLA_LOWRANK
    return pl.pallas_call(
        _prep_w_kernel,
        grid_spec=pltpu.PrefetchScalarGridSpec(
            num_scalar_prefetch=1,
            grid=(len(_W_SRC_BLK),),
            in_specs=[pl.BlockSpec((P_BLK, D_MODEL), lambda j, src: (src[j], 0)),
                      pl.BlockSpec((GLA_LOWRANK, D_MODEL), lambda j, src: ((src[j] + 1) * per_blk, 0))],
            out_specs=[pl.BlockSpec((P_BLK, D_MODEL), lambda j, src: (j, 0)),
                       pl.BlockSpec((LR_PAD, D_MODEL), lambda j, src: (0, 0))]),
        out_shape=[jax.ShapeDtypeStruct((P_COLS, D_MODEL), BF16),
                   jax.ShapeDtypeStruct((LR_PAD, D_MODEL), BF16)],
        compiler_params=pltpu.CompilerParams(dimension_semantics=("arbitrary",),
                                             vmem_limit_bytes=VMEM_LIMIT),
        name="prep_w",
    )(src, wt, wt)


def kernel(x_prompt, x_sample, state_gla, state_ret, c_prompt, c_sample, w_ada, b_ada, w_in, w_lr2, b_lr2,
           gla_norm_g, ret_norm_g, w_branch_gla, w_branch_ret, w_out, ln_g, ln_b):
    assert w_ada.shape[0] == 1, "one layer"
    bp, tp, _ = x_prompt.shape
    bs, ts, _ = x_sample.shape

    mod = _ada(jnp.concatenate([c_prompt, c_sample], axis=0), w_ada[0], b_ada)
    shift, scale1, gate = mod[:, :D_MODEL], mod[:, D_MODEL:2 * D_MODEL], mod[:, 2 * D_MODEL:]
    per_group = lambda a: a[:bp][:, None, :]
    mod_tok = jnp.repeat(mod[bp:], ts, axis=0)[None]

    w_main, w_lr = _prep_w_in(w_in[0].T)
    w_lr2p = jnp.pad(w_lr2[0], ((0, LR_PAD - GLA_LOWRANK), (0, 0))).astype(BF16)
    wbg = (gla_norm_g[0][:, None] * w_branch_gla[0]).astype(BF16)
    wbr = (ret_norm_g[0][:, None] * w_branch_ret[0]).astype(BF16)
    wo = w_out[0].astype(BF16)

    xp = x_prompt.reshape(bp * tp, D_MODEL)
    cos_p, sin_p = _rope_tables(np.arange(tp))
    p_p, la_p = _proj(xp, per_group(scale1), per_group(shift), w_main, w_lr, w_lr2p, b_lr2, cos_p, sin_p,
                      tm=1024, rows_per_mod=tp, p_dtype=BF16, cum_chunk=CHUNK)
    og_p, or_p, sg_p, sr_p = _mixer_prompt(p_p, la_p, bp, tp)
    y_p = _out_stage(og_p, or_p, p_p, xp, per_group(gate), wbg, wbr, wo, ln_g, ln_b, tm=1024, rows_per_mod=tp)

    xs = x_sample.reshape(bs * ts, D_MODEL)
    cos_s, sin_s = _rope_tables(PAST_LEN + np.arange(bs * ts) % ts)
    p_s, la_s = _proj(xs, mod_tok, mod_tok, w_main, w_lr, w_lr2p, b_lr2, cos_s, sin_s,
                      tm=bs * ts, rows_per_mod=ts, p_dtype=BF16, mod_cols=(1, 0))
    og_s, or_s, sg_s, sr_s = _mixer_sample(p_s, la_s, state_gla[0], state_ret[0], bs, ts)
    y_s = _out_stage(og_s, or_s, p_s, xs, mod_tok, wbg, wbr, wo, ln_g, ln_b,
                     tm=bs * ts, rows_per_mod=ts, gate_col=2)

    return (y_p.reshape(bp, tp, D_MODEL), y_s.reshape(bs, ts, D_MODEL),
            sg_p[None], sr_p[None], sg_s[None], sr_s[None])
```
